```python
import jax, jax.numpy as jnp
from jax import lax
import numpy as np

D_MODEL = 1024
BATCH = 4
SEQ = 4096
DEPTH = 1

RET_HEADS = 8
RET_DK = 64
RET_DV = 128
DN_HEADS = 8
DN_DK = 128
DN_DV = 128
CHUNK = 128
SHORT_CONV = 4
FFN_CONV = 3
D_FF = 2816
ROPE_BASE = 10000.0
EPS = 1e-6
GN_EPS = 1e-5

RET_QK = RET_HEADS * RET_DK
RET_V = RET_HEADS * RET_DV
DN_QK = DN_HEADS * DN_DK
DN_V = DN_HEADS * DN_DV
SPLITS = (RET_QK, RET_QK, RET_V, RET_V, DN_QK, DN_QK, DN_V, DN_V, DN_HEADS, DN_HEADS, D_MODEL, D_MODEL)
D_IN = sum(SPLITS)
DN_CONV_CH = 2 * DN_QK + DN_V

kernel_name = "retention_gated_deltanet_convffn_hybrid"


def rmsnorm(x, g):
    xf = x.astype(jnp.float32)
    r = lax.rsqrt(jnp.mean(xf * xf, axis=-1, keepdims=True) + EPS)
    return (xf * r) * g


def l2norm(x):
    xf = x.astype(jnp.float32)
    return xf * lax.rsqrt(jnp.sum(xf * xf, axis=-1, keepdims=True) + EPS)


def rotary(x):
    T, d = x.shape[1], x.shape[-1]
    inv = ROPE_BASE ** (-jnp.arange(0, d, 2, dtype=jnp.float32) / d)
    ang = jnp.arange(T, dtype=jnp.float32)[:, None] * inv[None, :]
    cos = jnp.cos(ang)[None, :, None, :]
    sin = jnp.sin(ang)[None, :, None, :]
    x1, x2 = x[..., : d // 2], x[..., d // 2:]
    return jnp.concatenate([x1 * cos - x2 * sin, x1 * sin + x2 * cos], axis=-1)


def causal_dwconv(x, w):
    K, T = w.shape[0], x.shape[1]
    xp = jnp.pad(x, ((0, 0), (K - 1, 0), (0, 0)))
    y = xp[:, K - 1:K - 1 + T] * w[K - 1]
    for k in range(K - 1):
        y = y + xp[:, k:k + T] * w[k]
    return y


def retention_chunkwise(q, k, v):
    B, T, H, dk = q.shape
    dv = v.shape[-1]
    N = T // CHUNK
    f32 = jnp.float32
    gamma = 1.0 - 2.0 ** (-5.0 - jnp.arange(H, dtype=f32))
    log_g = jnp.log(gamma)
    idx = jnp.arange(CHUNK, dtype=f32)
    qc = q.astype(f32).reshape(B, N, CHUNK, H, dk)
    kc = k.astype(f32).reshape(B, N, CHUNK, H, dk)
    vc = v.astype(f32).reshape(B, N, CHUNK, H, dv)
    diff = idx[:, None] - idx[None, :]
    causal = diff >= 0
    dmat = jnp.where(causal[None], jnp.exp(log_g[:, None, None] * jnp.where(causal, diff, 0.0)[None]), 0.0)
    scores = jnp.einsum('bnihd,bnjhd->bnhij', qc, kc) * dmat
    inner = jnp.einsum('bnhij,bnjhv->bnihv', scores, vc)
    zeta = jnp.exp(log_g[:, None] * (CHUNK - 1.0 - idx)[None, :])
    kv = jnp.einsum('bnjhd,hj,bnjhv->bnhdv', kc, zeta, vc)
    chunk_decay = jnp.exp(log_g * CHUNK)[:, None, None]

    def step(S, kv_n):
        return S * chunk_decay + kv_n, S

    _, s_prev = lax.scan(step, jnp.zeros((B, H, dk, dv), f32), jnp.moveaxis(kv, 1, 0))
    xi = jnp.exp(log_g[:, None] * (idx + 1.0)[None, :])
    cross = jnp.einsum('bnihd,nbhdv,hi->bnihv', qc, s_prev, xi)
    return (inner + cross).reshape(B, T, H, dv)


def gated_delta_chunkwise(q, k, v, g, beta):
    B, T, H, dk = q.shape
    dv = v.shape[-1]
    N = T // CHUNK
    f32 = jnp.float32

    def to_chunks(t):
        return jnp.moveaxis(t.astype(f32).reshape((B, N, CHUNK, H) + t.shape[3:]), 3, 1)

    qc = to_chunks(q) * (dk ** -0.5)
    kc = to_chunks(k)
    vc = to_chunks(v)
    gc = to_chunks(g)
    bc = to_chunks(beta)
    G = jnp.cumsum(gc, axis=-1)
    idx = jnp.arange(CHUNK)
    causal = idx[:, None] >= idx[None, :]
    strict = idx[:, None] > idx[None, :]
    diff = G[..., :, None] - G[..., None, :]
    L = jnp.where(causal, jnp.exp(jnp.where(causal, diff, 0.0)), 0.0)
    k_beta = kc * bc[..., None]
    A = jnp.where(strict, jnp.einsum('bhnid,bhnjd->bhnij', k_beta, kc) * L, 0.0)
    eye = jnp.eye(CHUNK, dtype=f32)
    rhs = jnp.concatenate([vc * bc[..., None], k_beta * jnp.exp(G)[..., None]], axis=-1)
    sol = lax.linalg.triangular_solve(eye + A, rhs, left_side=True, lower=True, unit_diagonal=True)
    u, w = sol[..., :dv], sol[..., dv:]
    attn = jnp.where(causal, jnp.einsum('bhnid,bhnjd->bhnij', qc, kc) * L, 0.0)
    q_dec = qc * jnp.exp(G)[..., None]
    k_dec = kc * jnp.exp(G[..., -1:] - G)[..., None]
    chunk_dec = jnp.exp(G[..., -1])

    def step(S, inp):
        u_n, w_n, q_n, k_n, a_n, d_n = inp
        v_new = u_n - jnp.einsum('bhcd,bhdv->bhcv', w_n, S)
        o = jnp.einsum('bhcd,bhdv->bhcv', q_n, S) + jnp.einsum('bhij,bhjv->bhiv', a_n, v_new)
        S = S * d_n[..., None, None] + jnp.einsum('bhcd,bhcv->bhdv', k_n, v_new)
        return S, o

    xs = tuple(jnp.moveaxis(t, 2, 0) for t in (u, w, q_dec, k_dec, attn, chunk_dec))
    _, o = lax.scan(step, jnp.zeros((B, H, dk, dv), f32), xs)
    return o.transpose(1, 0, 3, 2, 4).reshape(B, T, H, dv)


def setup_inputs(seed: int = 0) -> dict:
    key = jax.random.key(seed)
    ks = jax.random.split(key, 20)
    f32 = jnp.float32
    nrm = lambda k, shape, s: jax.random.normal(k, shape, f32) * s
    dt = jnp.exp(jax.random.uniform(ks[6], (DEPTH, DN_HEADS), f32, np.log(1e-3), np.log(1e-1)))
    return {
        "x": jax.random.normal(ks[0], (BATCH, SEQ, D_MODEL), f32),
        "g_mix": 1.0 + nrm(ks[1], (DEPTH, D_MODEL), 0.05),
        "w_in": nrm(ks[2], (DEPTH, D_MODEL, D_IN), D_MODEL ** -0.5),
        "ret_norm_g": 1.0 + nrm(ks[3], (DEPTH, RET_V), 0.05),
        "dn_conv_w": nrm(ks[4], (DEPTH, SHORT_CONV, DN_CONV_CH), SHORT_CONV ** -0.5),
        "dn_a_log": jnp.log(jax.random.uniform(ks[5], (DEPTH, DN_HEADS), f32, 1.0, 16.0)),
        "dn_dt_bias": dt + jnp.log(-jnp.expm1(-dt)),
        "dn_norm_g": 1.0 + nrm(ks[7], (DEPTH, DN_DV), 0.05),
        "w_ret_br": nrm(ks[8], (DEPTH, RET_V, D_MODEL), RET_V ** -0.5),
        "w_dn_br": nrm(ks[9], (DEPTH, DN_V, D_MODEL), DN_V ** -0.5),
        "w_o": nrm(ks[10], (DEPTH, D_MODEL, D_MODEL), D_MODEL ** -0.5),
        "g_ffn": 1.0 + nrm(ks[11], (DEPTH, D_MODEL), 0.05),
        "w_up": nrm(ks[12], (DEPTH, D_MODEL, 2 * D_FF), D_MODEL ** -0.5),
        "ffn_conv_w": nrm(ks[13], (DEPTH, FFN_CONV, 2 * D_FF), FFN_CONV ** -0.5),
        "ffn_conv_b": nrm(ks[14], (DEPTH, 2 * D_FF), 0.01),
        "w_down": nrm(ks[15], (DEPTH, D_FF, D_MODEL), D_FF ** -0.5),
        "g_final": 1.0 + nrm(ks[16], (D_MODEL,), 0.05),
    }


def reference(x, g_mix, w_in, ret_norm_g, dn_conv_w, dn_a_log, dn_dt_bias, dn_norm_g,
              w_ret_br, w_dn_br, w_o, g_ffn, w_up, ffn_conv_w, ffn_conv_b, w_down, g_final):
    B, T, _ = x.shape
    h = x.astype(jnp.float32)
    split_at = np.cumsum(SPLITS)[:-1].tolist()
    for l in range(DEPTH):
        u = rmsnorm(h, g_mix[l])
        proj = u @ w_in[l]
        (rq, rk, rv, rgate, dq, dk_, dv_, dz, db, da, gate_r, gate_d) = jnp.split(proj, split_at, axis=-1)

        rq = rotary(rq.reshape(B, T, RET_HEADS, RET_DK))
        rk = rotary(rk.reshape(B, T, RET_HEADS, RET_DK)) * (RET_DK ** -0.5)
        ro = retention_chunkwise(rq, rk, rv.reshape(B, T, RET_HEADS, RET_DV))
        mu = jnp.mean(ro, axis=-1, keepdims=True)
        var = jnp.mean(jnp.square(ro - mu), axis=-1, keepdims=True)
        ro = ((ro - mu) * lax.rsqrt(var + GN_EPS)).reshape(B, T, RET_V) * ret_norm_g[l]
        y_ret = (jax.nn.silu(rgate) * ro) @ w_ret_br[l]

        qkv = jax.nn.silu(causal_dwconv(jnp.concatenate([dq, dk_, dv_], axis=-1), dn_conv_w[l]))
        cq, ck, cv = jnp.split(qkv, [DN_QK, 2 * DN_QK], axis=-1)
        cq = l2norm(cq.reshape(B, T, DN_HEADS, DN_DK))
        ck = l2norm(ck.reshape(B, T, DN_HEADS, DN_DK))
        cv = cv.reshape(B, T, DN_HEADS, DN_DV)
        g = -jnp.exp(dn_a_log[l]) * jax.nn.softplus(da.astype(jnp.float32) + dn_dt_bias[l])
        beta = jax.nn.sigmoid(db.astype(jnp.float32))
        do = gated_delta_chunkwise(cq, ck, cv, g, beta)
        do = do * lax.rsqrt(jnp.mean(do * do, axis=-1, keepdims=True) + EPS) * dn_norm_g[l]
        do = do.reshape(B, T, DN_V) * jax.nn.silu(dz)
        y_dn = do @ w_dn_br[l]

        merged = jax.nn.sigmoid(gate_r) * y_ret + jax.nn.sigmoid(gate_d) * y_dn
        h = h + merged @ w_o[l]

        u2 = rmsnorm(h, g_ffn[l])
        up = causal_dwconv(u2 @ w_up[l], ffn_conv_w[l]) + ffn_conv_b[l]
        a, b = jnp.split(up, 2, axis=-1)
        h = h + (jax.nn.silu(a) * b) @ w_down[l]
    return rmsnorm(h, g_final).astype(x.dtype)
```

```python
import functools

import numpy as np
import jax
import jax.numpy as jnp
from jax import lax
from jax.experimental import pallas as pl
from jax.experimental.pallas import tpu as pltpu

D_MODEL = 1024
RET_HEADS = 8
RET_DK = 64
RET_DV = 128
DN_HEADS = 8
DN_DK = 128
DN_DV = 128
CHUNK = 128
SHORT_CONV = 4
FFN_CONV = 3
D_FF = 2816
ROPE_BASE = 10000.0
EPS = 1e-6
GN_EPS = 1e-5

RET_QK = RET_HEADS * RET_DK
RET_V = RET_HEADS * RET_DV
DN_QK = DN_HEADS * DN_DK
DN_V = DN_HEADS * DN_DV

LANES = 128
SUBLANES = 8
COLBLK = 1024
CB_RQK, CB_RV, CB_RGATE, CB_DQ, CB_DK, CB_DV, CB_DZ, CB_GATE_R, CB_GATE_D = range(9)
N_COLBLK = 9
SMALL_BETA0 = 0
SMALL_DECAY0 = 8

VMEM_LIMIT = 56 * 1024 * 1024

F32 = jnp.float32
BF16 = jnp.bfloat16


def _dot(a, b):
    return jnp.dot(a, b, preferred_element_type=F32)


def _dot_nt(a, b):
    return lax.dot_general(a, b, (((1,), (1,)), ((), ())), preferred_element_type=F32)


def _dot_tn(a, b):
    return lax.dot_general(a, b, (((0,), (0,)), ((), ())), preferred_element_type=F32)


def _split3(x):
    hi = x.astype(BF16)
    r1 = x - hi.astype(F32)
    mid = r1.astype(BF16)
    lo = (r1 - mid.astype(F32)).astype(BF16)
    return hi, mid, lo


def _dot_split_lhs(x, b):
    hi, mid, lo = _split3(x)
    return _dot(hi, b) + _dot(mid, b) + _dot(lo, b)


def _dot_split_rhs(a, x):
    hi, mid, lo = _split3(x)
    return _dot(a, hi) + _dot(a, mid) + _dot(a, lo)


def _sigmoid(x):
    return 1.0 / (1.0 + jnp.exp(-x))


def _silu(x):
    return x * _sigmoid(x)


def _softplus(x):
    return jnp.maximum(x, 0.0) + jnp.log(1.0 + jnp.exp(-jnp.abs(x)))


def _shift_rows(x, prev, s):
    r = pltpu.roll(x, s, 0)
    row = lax.broadcasted_iota(jnp.int32, (SUBLANES, x.shape[1]), 0)
    head = jnp.where(row < s, pltpu.roll(prev, s, 0), r[:SUBLANES])
    return jnp.concatenate([head, r[SUBLANES:]], axis=0)


def _proj_kernel(x_ref, g_ref, w_ref, ws_ref, p_ref, ps_ref, u_ref):
    @pl.when(pl.program_id(1) == 0)
    def _():
        x = x_ref[...]
        r = lax.rsqrt(jnp.mean(x * x, axis=-1, keepdims=True) + EPS)
        u = ((x * r) * g_ref[...]).astype(BF16)
        u_ref[...] = u
        ps_ref[...] = _dot(u, ws_ref[...])

    p_ref[...] = _dot(u_ref[...], w_ref[...]).astype(BF16)


def _proj_call(x2, g_mix, w_main, w_small, tm):
    m = x2.shape[0]
    return pl.pallas_call(
        _proj_kernel,
        grid=(m // tm, N_COLBLK),
        in_specs=[
            pl.BlockSpec((tm, D_MODEL), lambda i, j: (i, 0)),
            pl.BlockSpec((1, D_MODEL), lambda i, j: (0, 0)),
            pl.BlockSpec((D_MODEL, COLBLK), lambda i, j: (0, j)),
            pl.BlockSpec((D_MODEL, LANES), lambda i, j: (0, 0)),
        ],
        out_specs=[
            pl.BlockSpec((tm, COLBLK), lambda i, j: (i, j)),
            pl.BlockSpec((tm, LANES), lambda i, j: (i, 0)),
        ],
        out_shape=[
            jax.ShapeDtypeStruct((m, N_COLBLK * COLBLK), BF16),
            jax.ShapeDtypeStruct((m, LANES), F32),
        ],
        scratch_shapes=[pltpu.VMEM((tm, D_MODEL), BF16)],
        compiler_params=pltpu.CompilerParams(
            dimension_semantics=("arbitrary", "arbitrary"), vmem_limit_bytes=VMEM_LIMIT),
        name="proj",
    )(x2, g_mix, w_main, w_small)


def _ret_kernel(decays, nchunk, qk_ref, v_ref, gate_ref, gr_ref, cos_ref, sin_ref, dmat_ref, xi_ref,
                zeta_ref, gn_ref, wbr_ref, mr_ref, s_ref, stage_ref):
    @pl.when(pl.program_id(1) == 0)
    def _():
        s_ref[...] = jnp.zeros_like(s_ref)

    lane = lax.broadcasted_iota(jnp.int32, (CHUNK, LANES), 1)
    first_half = (lane % RET_DK) < (RET_DK // 2)
    head_lo = lane < RET_DK

    def rot(x, cos_t, sin_t):
        partner = jnp.where(first_half, pltpu.roll(x, LANES - RET_DK // 2, 1), pltpu.roll(x, RET_DK // 2, 1))
        return x * cos_t + partner * sin_t

    for c in range(nchunk):
        rows = slice(c * CHUNK, (c + 1) * CHUNK)
        cos_t = cos_ref[rows, :]
        sin_t = sin_ref[rows, :]
        for p in range(RET_HEADS // 2):
            q = rot(qk_ref[rows, p * LANES:(p + 1) * LANES].astype(F32), cos_t, sin_t)
            k = rot(qk_ref[rows, RET_QK + p * LANES:RET_QK + (p + 1) * LANES].astype(F32), cos_t, sin_t)
            k = k * (RET_DK ** -0.5)
            qb = q.astype(BF16)
            kb = k.astype(BF16)
            qx = (q * xi_ref[p]).astype(BF16)
            kz = (k * zeta_ref[p]).astype(BF16)
            for hh in range(2):
                h = 2 * p + hh
                mask = head_lo if hh == 0 else jnp.logical_not(head_lo)
                cols = slice(h * RET_DV, (h + 1) * RET_DV)
                qm = jnp.where(mask, qb, jnp.zeros_like(qb))
                qxm = jnp.where(mask, qx, jnp.zeros_like(qx))
                scores = _dot_nt(qm, kb) * dmat_ref[h]
                vh = v_ref[rows, cols]
                s = s_ref[h]
                ro = _dot(scores.astype(BF16), vh) + _dot(qxm, s.astype(BF16))
                s_ref[h] = s * decays[h] + _dot_tn(kz, vh)
                mu = jnp.mean(ro, axis=-1, keepdims=True)
                d = ro - mu
                var = jnp.mean(d * d, axis=-1, keepdims=True)
                y = (d * lax.rsqrt(var + GN_EPS)) * gn_ref[:, cols] * _silu(gate_ref[rows, cols].astype(F32))
                stage_ref[rows, cols] = y.astype(BF16)

    y_ret = _dot(stage_ref[...], wbr_ref[...])
    mr_ref[...] = (_sigmoid(gr_ref[...].astype(F32)) * y_ret).astype(BF16)


def _ret_tables():
    h = np.arange(RET_HEADS, dtype=np.float64)
    gamma = 1.0 - 2.0 ** (-5.0 - h)
    log_g = np.log(gamma)
    idx = np.arange(CHUNK, dtype=np.float64)
    diff = idx[:, None] - idx[None, :]
    dmat = np.where(diff >= 0, np.exp(log_g[:, None, None] * np.maximum(diff, 0.0)[None]), 0.0)
    zeta = np.exp(log_g[:, None] * (CHUNK - 1.0 - idx)[None, :])
    xi = np.exp(log_g[:, None] * (idx + 1.0)[None, :])
    decays = tuple(float(v) for v in np.exp(log_g * CHUNK))

    def pair_table(t):
        t = t.reshape(RET_HEADS // 2, 2, CHUNK)
        return np.repeat(np.transpose(t, (0, 2, 1)), RET_DK, axis=2)

    return (jnp.asarray(dmat, F32), jnp.asarray(pair_table(xi), F32), jnp.asarray(pair_table(zeta), F32), decays)


def _rope_tables(t_len):
    half = RET_DK // 2
    inv = ROPE_BASE ** (-np.arange(0, RET_DK, 2, dtype=np.float64) / RET_DK)
    ang = np.arange(t_len, dtype=np.float64)[:, None] * inv[None, :]
    cos, sin = np.cos(ang), np.sin(ang)
    cos_t = np.concatenate([cos, cos], axis=1)
    sin_t = np.concatenate([-sin, sin], axis=1)
    assert cos_t.shape[1] == 2 * half
    reps = LANES // RET_DK
    return jnp.asarray(np.tile(cos_t, (1, reps)), F32), jnp.asarray(np.tile(sin_t, (1, reps)), F32)


def _ret_call(p_main, cos_t, sin_t, ret_norm_g, w_ret_br, batch, t_len, tt):
    m = batch * t_len
    nt = t_len // tt
    dmat, xi, zeta, decays = _ret_tables()
    row = lambda cb: pl.BlockSpec((tt, COLBLK), lambda b, t: (b * nt + t, cb))
    const2 = lambda shape: pl.BlockSpec(shape, lambda b, t: (0, 0))
    const3 = lambda shape: pl.BlockSpec(shape, lambda b, t: (0, 0, 0))
    return pl.pallas_call(
        functools.partial(_ret_kernel, decays, tt // CHUNK),
        grid=(batch, nt),
        in_specs=[
            row(CB_RQK), row(CB_RV), row(CB_RGATE), row(CB_GATE_R),
            pl.BlockSpec((tt, LANES), lambda b, t: (t, 0)),
            pl.BlockSpec((tt, LANES), lambda b, t: (t, 0)),
            const3((RET_HEADS, CHUNK, CHUNK)),
            const3((RET_HEADS // 2, CHUNK, LANES)),
            const3((RET_HEADS // 2, CHUNK, LANES)),
            const2((1, RET_V)),
            const2((RET_V, D_MODEL)),
        ],
        out_specs=pl.BlockSpec((tt, D_MODEL), lambda b, t: (b * nt + t, 0)),
        out_shape=jax.ShapeDtypeStruct((m, D_MODEL), BF16),
        scratch_shapes=[
            pltpu.VMEM((RET_HEADS, LANES, RET_DV), F32),
            pltpu.VMEM((tt, RET_V), BF16),
        ],
        compiler_params=pltpu.CompilerParams(
            dimension_semantics=("arbitrary", "arbitrary"), vmem_limit_bytes=VMEM_LIMIT),
        name="retention",
    )(p_main, p_main, p_main, p_main, cos_t, sin_t, dmat, xi, zeta, ret_norm_g, w_ret_br)


def _inv_unit_lower(a, qmask_ref):
    row = lax.broadcasted_iota(jnp.int32, (CHUNK, CHUNK), 0)
    col = lax.broadcasted_iota(jnp.int32, (CHUNK, CHUNK), 1)
    eye = jnp.where(row == col, 1.0, 0.0).astype(F32)
    d = eye - a * qmask_ref[0]
    level = 1
    b = 2
    while b < CHUNK:
        qb = (a * qmask_ref[level]).astype(BF16)
        db = d.astype(BF16)
        d = d - _dot(_dot(db, qb).astype(BF16), db)
        level += 1
        b *= 2
    return d


def _dn_kernel(nchunk, q_ref, k_ref, v_ref, z_ref, gd_ref, mr_ref, x_ref, ps_ref, convw_ref, alog_ref, dtb_ref,
               eb_ref, eg_ref, tri_ref, qmask_ref, ng_ref, wbr_ref, wo_ref, h_ref, s_ref, carry_ref, stage_ref):
    @pl.when(pl.program_id(1) == 0)
    def _():
        s_ref[...] = jnp.zeros_like(s_ref)
        carry_ref[...] = jnp.zeros_like(carry_ref)

    tt = nchunk * CHUNK

    def conv_silu(ref, g):
        x = ref[...].astype(F32)
        prev = carry_ref[g]
        w = convw_ref[:, g * DN_QK:(g + 1) * DN_QK]
        y = x * w[SHORT_CONV - 1:SHORT_CONV]
        for s in range(1, SHORT_CONV):
            y = y + _shift_rows(x, prev, s) * w[SHORT_CONV - 1 - s:SHORT_CONV - s]
        carry_ref[g] = x[tt - SUBLANES:]
        return _silu(y)

    cq = conv_silu(q_ref, 0)
    ck = conv_silu(k_ref, 1)
    cv = conv_silu(v_ref, 2)

    ps = ps_ref[...]
    beta_all = _sigmoid(ps)
    g_all = -jnp.exp(alog_ref[...]) * _softplus(ps + dtb_ref[...])
    beta_b = _dot_split_lhs(beta_all, eb_ref[...])

    row = lax.broadcasted_iota(jnp.int32, (CHUNK, CHUNK), 0)
    col = lax.broadcasted_iota(jnp.int32, (CHUNK, CHUNK), 1)
    causal = row >= col
    strict = row > col
    tri = tri_ref[...]

    for c in range(nchunk):
        rows = slice(c * CHUNK, (c + 1) * CHUNK)
        g_cum = _dot_split_rhs(tri, g_all[rows])
        g_cum_b = _dot_split_lhs(g_cum, eg_ref[...])
        g_cum_t = g_cum.T
        for h in range(DN_HEADS):
            cols = slice(h * DN_DV, (h + 1) * DN_DV)
            gc = g_cum_b[:, cols]
            gr = jnp.broadcast_to(g_cum_t[SMALL_DECAY0 + h:SMALL_DECAY0 + h + 1, :], (CHUNK, CHUNK))
            g_last = gc[CHUNK - 1:CHUNK, :]
            decay = jnp.where(causal, jnp.exp(gc - gr), 0.0)
            exp_g = jnp.exp(gc)
            q = cq[rows, cols]
            k = ck[rows, cols]
            q = q * lax.rsqrt(jnp.sum(q * q, axis=-1, keepdims=True) + EPS) * (DN_DK ** -0.5)
            k = k * lax.rsqrt(jnp.sum(k * k, axis=-1, keepdims=True) + EPS)
            v = cv[rows, cols]
            beta = beta_b[rows, cols]
            k_beta = k * beta
            kb16 = k.astype(BF16)
            a = jnp.where(strict, _dot_nt(k_beta.astype(BF16), kb16) * decay, 0.0)
            minv = _inv_unit_lower(a, qmask_ref)
            rhs = jnp.concatenate([v * beta, k_beta * exp_g], axis=1).astype(BF16)
            sol = _dot(minv.astype(BF16), rhs)
            u = sol[:, :DN_DV]
            w = sol[:, DN_DV:]
            attn = jnp.where(causal, _dot_nt(q.astype(BF16), kb16) * decay, 0.0)
            q_dec = (q * exp_g).astype(BF16)
            k_dec = (k * jnp.exp(g_last - gc)).astype(BF16)
            s = s_ref[h]
            s16 = s.astype(BF16)
            v_new = u - _dot(w.astype(BF16), s16)
            vn16 = v_new.astype(BF16)
            o = _dot(q_dec, s16) + _dot(attn.astype(BF16), vn16)
            s_ref[h] = s * jnp.exp(g_last) + _dot_tn(k_dec, vn16)
            o = o * lax.rsqrt(jnp.mean(o * o, axis=-1, keepdims=True) + EPS) * ng_ref[...]
            stage_ref[rows, cols] = (o * _silu(z_ref[rows, cols].astype(F32))).astype(BF16)

    y_dn = _dot(stage_ref[...], wbr_ref[...])
    merged = mr_ref[...].astype(F32) + _sigmoid(gd_ref[...].astype(F32)) * y_dn
    h_ref[...] = x_ref[...] + _dot(merged.astype(BF16), wo_ref[...])


def _dn_tables():
    idx = np.arange(CHUNK)
    r, c = idx[:, None], idx[None, :]
    qmasks = []
    b = 1
    while b < CHUNK:
        qmasks.append(((r // (2 * b)) == (c // (2 * b))) & ((r // b) % 2 == 1) & ((c // b) % 2 == 0))
        b *= 2
    qmask = np.stack(qmasks).astype(np.float32)
    tri = (r >= c).astype(np.float32)
    eb = np.zeros((LANES, DN_V), np.float32)
    eg = np.zeros((LANES, DN_V), np.float32)
    for h in range(DN_HEADS):
        eb[SMALL_BETA0 + h, h * DN_DV:(h + 1) * DN_DV] = 1.0
        eg[SMALL_DECAY0 + h, h * DN_DV:(h + 1) * DN_DV] = 1.0
    return jnp.asarray(eb, BF16), jnp.asarray(eg, BF16), jnp.asarray(tri, BF16), jnp.asarray(qmask, F32)


def _dn_call(p_main, p_small, mr, x2, conv_w, alog_row, dtb_row, dn_norm_g, w_dn_br, w_o, batch, t_len, tt):
    m = batch * t_len
    nt = t_len // tt
    eb, eg, tri, qmask = _dn_tables()
    nlevels = qmask.shape[0]
    row = lambda cb: pl.BlockSpec((tt, COLBLK), lambda b, t: (b * nt + t, cb))
    tok = lambda n: pl.BlockSpec((tt, n), lambda b, t: (b * nt + t, 0))
    const2 = lambda shape: pl.BlockSpec(shape, lambda b, t: (0, 0))
    return pl.pallas_call(
        functools.partial(_dn_kernel, tt // CHUNK),
        grid=(batch, nt),
        in_specs=[
            row(CB_DQ), row(CB_DK), row(CB_DV), row(CB_DZ), row(CB_GATE_D),
            tok(D_MODEL), tok(D_MODEL), tok(LANES),
            const2((SHORT_CONV, 3 * DN_QK)),
            const2((1, LANES)), const2((1, LANES)),
            const2((LANES, DN_V)), const2((LANES, DN_V)),
            const2((CHUNK, CHUNK)),
            pl.BlockSpec((nlevels, CHUNK, CHUNK), lambda b, t: (0, 0, 0)),
            const2((1, DN_DV)),
            const2((DN_V, D_MODEL)), const2((D_MODEL, D_MODEL)),
        ],
        out_specs=tok(D_MODEL),
        out_shape=jax.ShapeDtypeStruct((m, D_MODEL), F32),
        scratch_shapes=[
            pltpu.VMEM((DN_HEADS, DN_DK, DN_DV), F32),
            pltpu.VMEM((3, SUBLANES, DN_QK), F32),
            pltpu.VMEM((tt, DN_V), BF16),
        ],
        compiler_params=pltpu.CompilerParams(
            dimension_semantics=("arbitrary", "arbitrary"), vmem_limit_bytes=VMEM_LIMIT),
        name="deltanet",
    )(p_main, p_main, p_main, p_main, p_main, mr, x2, p_small, conv_w, alog_row, dtb_row, eb, eg, tri, qmask,
      dn_norm_g, w_dn_br, w_o)


FFN_COLS = 256


def _ffn_kernel(h_ref, g_ref, wup_ref, cw_ref, cb_ref, wdn_ref, gf_ref, o_ref, carry_ref):
    @pl.when(pl.program_id(1) == 0)
    def _():
        carry_ref[...] = jnp.zeros_like(carry_ref)

    tm = h_ref.shape[0]
    h = h_ref[...]
    u = ((h * lax.rsqrt(jnp.mean(h * h, axis=-1, keepdims=True) + EPS)) * g_ref[...]).astype(BF16)

    def branch(c0):
        cols = slice(c0, c0 + FFN_COLS)
        up = _dot(u, wup_ref[:, cols])
        prev = carry_ref[:, cols]
        w = cw_ref[:, cols]
        y = up * w[FFN_CONV - 1:FFN_CONV] + cb_ref[:, cols]
        for s in range(1, FFN_CONV):
            y = y + _shift_rows(up, prev, s) * w[FFN_CONV - 1 - s:FFN_CONV - s]
        carry_ref[:, cols] = up[tm - SUBLANES:]
        return y

    acc = h
    for j in range(D_FF // FFN_COLS):
        a = branch(j * FFN_COLS)
        b = branch(D_FF + j * FFN_COLS)
        act = (_silu(a) * b).astype(BF16)
        acc = acc + _dot(act, wdn_ref[j * FFN_COLS:(j + 1) * FFN_COLS, :])

    o_ref[...] = (acc * lax.rsqrt(jnp.mean(acc * acc, axis=-1, keepdims=True) + EPS)) * gf_ref[...]


def _ffn_call(h2, g_ffn, w_up, conv_w, conv_b, w_down, g_final, batch, t_len, tm):
    m = batch * t_len
    nt = t_len // tm
    tok = pl.BlockSpec((tm, D_MODEL), lambda b, t: (b * nt + t, 0))
    const2 = lambda shape: pl.BlockSpec(shape, lambda b, t: (0, 0))
    resident = lambda shape: pl.BlockSpec(shape, lambda b, t: (0, 0), pipeline_mode=pl.Buffered(1))
    return pl.pallas_call(
        _ffn_kernel,
        grid=(batch, nt),
        in_specs=[
            tok, const2((1, D_MODEL)),
            resident((D_MODEL, 2 * D_FF)),
            const2((FFN_CONV, 2 * D_FF)), const2((1, 2 * D_FF)),
            resident((D_FF, D_MODEL)),
            const2((1, D_MODEL)),
        ],
        out_specs=tok,
        out_shape=jax.ShapeDtypeStruct((m, D_MODEL), F32),
        scratch_shapes=[pltpu.VMEM((SUBLANES, 2 * D_FF), F32)],
        compiler_params=pltpu.CompilerParams(
            dimension_semantics=("arbitrary", "arbitrary"), vmem_limit_bytes=VMEM_LIMIT),
        name="convffn",
    )(h2, g_ffn, w_up, conv_w, conv_b, w_down, g_final)


def _pick_tile(t_len, want):
    tile = min(want, t_len)
    assert t_len % tile == 0 and tile % CHUNK == 0
    return tile


def kernel(x, g_mix, w_in, ret_norm_g, dn_conv_w, dn_a_log, dn_dt_bias, dn_norm_g, w_ret_br, w_dn_br, w_o, g_ffn,
           w_up, ffn_conv_w, ffn_conv_b, w_down, g_final):
    batch, t_len, d_model = x.shape
    assert d_model == D_MODEL and g_mix.shape[0] == 1 and t_len % CHUNK == 0
    m = batch * t_len
    x2 = x.astype(F32).reshape(m, D_MODEL)

    small0 = 2 * RET_QK + 2 * RET_V + 2 * DN_QK + 2 * DN_V
    small1 = small0 + 2 * DN_HEADS
    w = w_in[0]
    w_main = jnp.concatenate([w[:, :small0], w[:, small1:]], axis=1).astype(BF16)
    w_small = jnp.pad(w[:, small0:small1], ((0, 0), (0, LANES - 2 * DN_HEADS))).astype(BF16)

    p_main, p_small = _proj_call(x2, g_mix, w_main, w_small, _pick_tile(m, 1024))

    cos_t, sin_t = _rope_tables(t_len)
    mr = _ret_call(p_main, cos_t, sin_t, ret_norm_g, w_ret_br[0].astype(BF16), batch, t_len, _pick_tile(t_len, 256))

    alog_row = jnp.zeros((1, LANES), F32).at[0, SMALL_DECAY0:SMALL_DECAY0 + DN_HEADS].set(dn_a_log[0])
    dtb_row = jnp.zeros((1, LANES), F32).at[0, SMALL_DECAY0:SMALL_DECAY0 + DN_HEADS].set(dn_dt_bias[0])
    h = _dn_call(p_main, p_small, mr, x2, dn_conv_w[0], alog_row, dtb_row, dn_norm_g, w_dn_br[0].astype(BF16),
                 w_o[0].astype(BF16), batch, t_len, _pick_tile(t_len, 256))

    out = _ffn_call(h, g_ffn, w_up[0].astype(BF16), ffn_conv_w[0], ffn_conv_b, w_down[0].astype(BF16),
                    g_final.reshape(1, D_MODEL), batch, t_len, _pick_tile(t_len, 512))
    return out.reshape(batch, t_len, D_MODEL).astype(x.dtype)
```

```python
import functools

import numpy as np
import jax
import jax.numpy as jnp
from jax import lax
from jax.experimental import pallas as pl
from jax.experimental.pallas import tpu as pltpu

D_MODEL = 1024
RET_HEADS = 8
RET_DK = 64
RET_DV = 128
DN_HEADS = 8
DN_DK = 128
DN_DV = 128
CHUNK = 128
SHORT_CONV = 4
FFN_CONV = 3
D_FF = 2816
ROPE_BASE = 10000.0
EPS = 1e-6
GN_EPS = 1e-5

RET_QK = RET_HEADS * RET_DK
RET_V = RET_HEADS * RET_DV
DN_QK = DN_HEADS * DN_DK
DN_V = DN_HEADS * DN_DV

LANES = 128
SUBLANES = 8
COLBLK = 1024
CB_RQK, CB_RV, CB_RGATE, CB_DQ, CB_DK, CB_DV, CB_DZ, CB_GATE_R, CB_GATE_D = range(9)
N_COLBLK = 9
SMALL_BETA0 = 0
SMALL_DECAY0 = 8

VMEM_LIMIT = 56 * 1024 * 1024

F32 = jnp.float32
BF16 = jnp.bfloat16


def _dot(a, b):
    return jnp.dot(a, b, preferred_element_type=F32)


def _dot_nt(a, b):
    return lax.dot_general(a, b, (((1,), (1,)), ((), ())), preferred_element_type=F32)


def _dot_tn(a, b):
    return lax.dot_general(a, b, (((0,), (0,)), ((), ())), preferred_element_type=F32)


def _split3(x):
    hi = x.astype(BF16)
    r1 = x - hi.astype(F32)
    mid = r1.astype(BF16)
    lo = (r1 - mid.astype(F32)).astype(BF16)
    return hi, mid, lo


def _dot_split_lhs(x, b):
    hi, mid, lo = _split3(x)
    return _dot(hi, b) + _dot(mid, b) + _dot(lo, b)


def _dot_split_rhs(a, x):
    hi, mid, lo = _split3(x)
    return _dot(a, hi) + _dot(a, mid) + _dot(a, lo)


def _sigmoid(x):
    return 1.0 / (1.0 + jnp.exp(-x))


def _silu(x):
    return x * _sigmoid(x)


def _softplus(x):
    return jnp.maximum(x, 0.0) + jnp.log(1.0 + jnp.exp(-jnp.abs(x)))


def _shift_rows(x, prev, s):
    r = pltpu.roll(x, s, 0)
    row = lax.broadcasted_iota(jnp.int32, (SUBLANES, x.shape[1]), 0)
    head = jnp.where(row < s, pltpu.roll(prev, s, 0), r[:SUBLANES])
    return jnp.concatenate([head, r[SUBLANES:]], axis=0)


def _proj_kernel(x_ref, g_ref, w_ref, ws_ref, p_ref, ps_ref, u_ref):
    @pl.when(pl.program_id(1) == 0)
    def _():
        x = x_ref[...]
        r = lax.rsqrt(jnp.mean(x * x, axis=-1, keepdims=True) + EPS)
        u = ((x * r) * g_ref[...]).astype(BF16)
        u_ref[...] = u
        ps_ref[...] = _dot(u, ws_ref[...])

    p_ref[...] = _dot(u_ref[...], w_ref[...]).astype(BF16)


def _proj_call(x2, g_mix, w_main, w_small, tm):
    m = x2.shape[0]
    return pl.pallas_call(
        _proj_kernel,
        grid=(m // tm, N_COLBLK),
        in_specs=[
            pl.BlockSpec((tm, D_MODEL), lambda i, j: (i, 0)),
            pl.BlockSpec((1, D_MODEL), lambda i, j: (0, 0)),
            pl.BlockSpec((D_MODEL, COLBLK), lambda i, j: (0, j)),
            pl.BlockSpec((D_MODEL, LANES), lambda i, j: (0, 0)),
        ],
        out_specs=[
            pl.BlockSpec((tm, COLBLK), lambda i, j: (i, j)),
            pl.BlockSpec((tm, LANES), lambda i, j: (i, 0)),
        ],
        out_shape=[
            jax.ShapeDtypeStruct((m, N_COLBLK * COLBLK), BF16),
            jax.ShapeDtypeStruct((m, LANES), F32),
        ],
        scratch_shapes=[pltpu.VMEM((tm, D_MODEL), BF16)],
        compiler_params=pltpu.CompilerParams(
            dimension_semantics=("arbitrary", "arbitrary"), vmem_limit_bytes=VMEM_LIMIT),
        name="proj",
    )(x2, g_mix, w_main, w_small)


def _ret_kernel(decays, nchunk, qk_ref, v_ref, gate_ref, gr_ref, cos_ref, sin_ref, dmat_ref, xi_ref,
                zeta_ref, gn_ref, wbr_ref, mr_ref, s_ref, stage_ref):
    @pl.when(pl.program_id(1) == 0)
    def _():
        s_ref[...] = jnp.zeros_like(s_ref)

    lane = lax.broadcasted_iota(jnp.int32, (CHUNK, LANES), 1)
    first_half = (lane % RET_DK) < (RET_DK // 2)
    head_lo = lane < RET_DK

    def rot(x, cos_t, sin_t):
        partner = jnp.where(first_half, pltpu.roll(x, LANES - RET_DK // 2, 1), pltpu.roll(x, RET_DK // 2, 1))
        return x * cos_t + partner * sin_t

    for c in range(nchunk):
        rows = slice(c * CHUNK, (c + 1) * CHUNK)
        cos_t = cos_ref[rows, :]
        sin_t = sin_ref[rows, :]
        for p in range(RET_HEADS // 2):
            q = rot(qk_ref[rows, p * LANES:(p + 1) * LANES].astype(F32), cos_t, sin_t)
            k = rot(qk_ref[rows, RET_QK + p * LANES:RET_QK + (p + 1) * LANES].astype(F32), cos_t, sin_t)
            k = k * (RET_DK ** -0.5)
            qb = q.astype(BF16)
            kb = k.astype(BF16)
            qx = (q * xi_ref[p]).astype(BF16)
            kz = (k * zeta_ref[p]).astype(BF16)
            for hh in range(2):
                h = 2 * p + hh
                mask = head_lo if hh == 0 else jnp.logical_not(head_lo)
                cols = slice(h * RET_DV, (h + 1) * RET_DV)
                qm = jnp.where(mask, qb, jnp.zeros_like(qb))
                qxm = jnp.where(mask, qx, jnp.zeros_like(qx))
                scores = _dot_nt(qm, kb) * dmat_ref[h]
                vh = v_ref[rows, cols]
                s = s_ref[h]
                ro = _dot(scores.astype(BF16), vh) + _dot(qxm, s.astype(BF16))
                s_ref[h] = s * decays[h] + _dot_tn(kz, vh)
                mu = jnp.mean(ro, axis=-1, keepdims=True)
                d = ro - mu
                var = jnp.mean(d * d, axis=-1, keepdims=True)
                y = (d * lax.rsqrt(var + GN_EPS)) * gn_ref[:, cols] * _silu(gate_ref[rows, cols].astype(F32))
                stage_ref[rows, cols] = y.astype(BF16)

    y_ret = _dot(stage_ref[...], wbr_ref[...])
    mr_ref[...] = (_sigmoid(gr_ref[...].astype(F32)) * y_ret).astype(BF16)


def _ret_tables():
    h = np.arange(RET_HEADS, dtype=np.float64)
    gamma = 1.0 - 2.0 ** (-5.0 - h)
    log_g = np.log(gamma)
    idx = np.arange(CHUNK, dtype=np.float64)
    diff = idx[:, None] - idx[None, :]
    dmat = np.where(diff >= 0, np.exp(log_g[:, None, None] * np.maximum(diff, 0.0)[None]), 0.0)
    zeta = np.exp(log_g[:, None] * (CHUNK - 1.0 - idx)[None, :])
    xi = np.exp(log_g[:, None] * (idx + 1.0)[None, :])
    decays = tuple(float(v) for v in np.exp(log_g * CHUNK))

    def pair_table(t):
        t = t.reshape(RET_HEADS // 2, 2, CHUNK)
        return np.repeat(np.transpose(t, (0, 2, 1)), RET_DK, axis=2)

    return (jnp.asarray(dmat, F32), jnp.asarray(pair_table(xi), F32), jnp.asarray(pair_table(zeta), F32), decays)


def _rope_tables(t_len):
    half = RET_DK // 2
    inv = ROPE_BASE ** (-np.arange(0, RET_DK, 2, dtype=np.float64) / RET_DK)
    ang = np.arange(t_len, dtype=np.float64)[:, None] * inv[None, :]
    cos, sin = np.cos(ang), np.sin(ang)
    cos_t = np.concatenate([cos, cos], axis=1)
    sin_t = np.concatenate([-sin, sin], axis=1)
    assert cos_t.shape[1] == 2 * half
    reps = LANES // RET_DK
    return jnp.asarray(np.tile(cos_t, (1, reps)), F32), jnp.asarray(np.tile(sin_t, (1, reps)), F32)


def _ret_call(p_main, cos_t, sin_t, ret_norm_g, w_ret_br, batch, t_len, tt):
    m = batch * t_len
    nt = t_len // tt
    dmat, xi, zeta, decays = _ret_tables()
    row = lambda cb: pl.BlockSpec((tt, COLBLK), lambda b, t: (b * nt + t, cb))
    const2 = lambda shape: pl.BlockSpec(shape, lambda b, t: (0, 0))
    const3 = lambda shape: pl.BlockSpec(shape, lambda b, t: (0, 0, 0))
    return pl.pallas_call(
        functools.partial(_ret_kernel, decays, tt // CHUNK),
        grid=(batch, nt),
        in_specs=[
            row(CB_RQK), row(CB_RV), row(CB_RGATE), row(CB_GATE_R),
            pl.BlockSpec((tt, LANES), lambda b, t: (t, 0)),
            pl.BlockSpec((tt, LANES), lambda b, t: (t, 0)),
            const3((RET_HEADS, CHUNK, CHUNK)),
            const3((RET_HEADS // 2, CHUNK, LANES)),
            const3((RET_HEADS // 2, CHUNK, LANES)),
            const2((1, RET_V)),
            const2((RET_V, D_MODEL)),
        ],
        out_specs=pl.BlockSpec((tt, D_MODEL), lambda b, t: (b * nt + t, 0)),
        out_shape=jax.ShapeDtypeStruct((m, D_MODEL), BF16),
        scratch_shapes=[
            pltpu.VMEM((RET_HEADS, LANES, RET_DV), F32),
            pltpu.VMEM((tt, RET_V), BF16),
        ],
        compiler_params=pltpu.CompilerParams(
            dimension_semantics=("arbitrary", "arbitrary"), vmem_limit_bytes=VMEM_LIMIT),
        name="retention",
    )(p_main, p_main, p_main, p_main, cos_t, sin_t, dmat, xi, zeta, ret_norm_g, w_ret_br)


def _inv_unit_lower_many(a_list, qmask_ref):
    row = lax.broadcasted_iota(jnp.int32, (CHUNK, CHUNK), 0)
    col = lax.broadcasted_iota(jnp.int32, (CHUNK, CHUNK), 1)
    eye = jnp.where(row == col, 1.0, 0.0).astype(F32)
    a16 = [a.astype(BF16) for a in a_list]
    d = [eye - a * qmask_ref[0] for a in a_list]
    level = 1
    b = 2
    while b < CHUNK:
        mask = qmask_ref[level]
        d16 = [x.astype(BF16) for x in d]
        t16 = [_dot(x, a).astype(BF16) for x, a in zip(d16, a16)]
        d = [x - _dot(t, y) * mask for x, t, y in zip(d, t16, d16)]
        level += 1
        b *= 2
    return d


def _dn_kernel(nchunk, q_ref, k_ref, v_ref, z_ref, gd_ref, mr_ref, x_ref, ps_ref, convw_ref, alog_ref, dtb_ref,
               eb_ref, eg_ref, tri_ref, qmask_ref, ng_ref, wbr_ref, wo_ref, h_ref, s_ref, carry_ref, stage_ref):
    @pl.when(pl.program_id(1) == 0)
    def _():
        s_ref[...] = jnp.zeros_like(s_ref)
        carry_ref[...] = jnp.zeros_like(carry_ref)

    tt = nchunk * CHUNK

    def conv_silu(ref, g):
        x = ref[...].astype(F32)
        prev = carry_ref[g]
        w = convw_ref[:, g * DN_QK:(g + 1) * DN_QK]
        y = x * w[SHORT_CONV - 1:SHORT_CONV]
        for s in range(1, SHORT_CONV):
            y = y + _shift_rows(x, prev, s) * w[SHORT_CONV - 1 - s:SHORT_CONV - s]
        carry_ref[g] = x[tt - SUBLANES:]
        return _silu(y)

    cq = conv_silu(q_ref, 0)
    ck = conv_silu(k_ref, 1)
    cv = conv_silu(v_ref, 2)

    ps = ps_ref[...]
    beta_all = _sigmoid(ps)
    g_all = -jnp.exp(alog_ref[...]) * _softplus(ps + dtb_ref[...])
    beta_b = _dot_split_lhs(beta_all, eb_ref[...])

    row = lax.broadcasted_iota(jnp.int32, (CHUNK, CHUNK), 0)
    col = lax.broadcasted_iota(jnp.int32, (CHUNK, CHUNK), 1)
    causal = row >= col
    strict = row > col
    tri = tri_ref[...]

    g_cum_b, g_cum_t = [], []
    for c in range(nchunk):
        g_cum = _dot_split_rhs(tri, g_all[c * CHUNK:(c + 1) * CHUNK])
        g_cum_b.append(_dot_split_lhs(g_cum, eg_ref[...]))
        g_cum_t.append(g_cum.T)

    items = [(c, h) for c in range(nchunk) for h in range(DN_HEADS)]

    a_list, attn16, rhs16, qdec16, kdec16, sdec = [], [], [], [], [], []
    for c, h in items:
        rows = slice(c * CHUNK, (c + 1) * CHUNK)
        cols = slice(h * DN_DV, (h + 1) * DN_DV)
        gc = g_cum_b[c][:, cols]
        gr = jnp.broadcast_to(g_cum_t[c][SMALL_DECAY0 + h:SMALL_DECAY0 + h + 1, :], (CHUNK, CHUNK))
        g_last = gc[CHUNK - 1:CHUNK, :]
        decay = jnp.where(causal, jnp.exp(gc - gr), 0.0)
        exp_g = jnp.exp(gc)
        q = cq[rows, cols]
        k = ck[rows, cols]
        q = q * (lax.rsqrt(jnp.sum(q * q, axis=-1, keepdims=True) + EPS) * (DN_DK ** -0.5))
        k = k * lax.rsqrt(jnp.sum(k * k, axis=-1, keepdims=True) + EPS)
        beta = beta_b[rows, cols]
        k_beta = k * beta
        k16 = k.astype(BF16)
        a_list.append(jnp.where(strict, _dot_nt(k_beta.astype(BF16), k16) * decay, 0.0))
        attn16.append(jnp.where(causal, _dot_nt(q.astype(BF16), k16) * decay, 0.0).astype(BF16))
        rhs16.append(jnp.concatenate([cv[rows, cols] * beta, k_beta * exp_g], axis=1).astype(BF16))
        qdec16.append((q * exp_g).astype(BF16))
        kdec16.append((k * jnp.exp(g_last - gc)).astype(BF16))
        sdec.append(jnp.exp(g_last))

    minv = _inv_unit_lower_many(a_list, qmask_ref)
    sol = [_dot(m.astype(BF16), r) for m, r in zip(minv, rhs16)]

    for c in range(nchunk):
        idx = [c * DN_HEADS + h for h in range(DN_HEADS)]
        s = [s_ref[h] for h in range(DN_HEADS)]
        s16 = [x.astype(BF16) for x in s]
        ws = [_dot(sol[i][:, DN_DV:].astype(BF16), s16[h]) for h, i in enumerate(idx)]
        qs = [_dot(qdec16[i], s16[h]) for h, i in enumerate(idx)]
        vn16 = [(sol[i][:, :DN_DV] - ws[h]).astype(BF16) for h, i in enumerate(idx)]
        o = [qs[h] + _dot(attn16[i], vn16[h]) for h, i in enumerate(idx)]
        for h, i in enumerate(idx):
            s_ref[h] = s[h] * sdec[i] + _dot_tn(kdec16[i], vn16[h])
        rows = slice(c * CHUNK, (c + 1) * CHUNK)
        for h in range(DN_HEADS):
            cols = slice(h * DN_DV, (h + 1) * DN_DV)
            on = o[h] * lax.rsqrt(jnp.mean(o[h] * o[h], axis=-1, keepdims=True) + EPS) * ng_ref[...]
            stage_ref[rows, cols] = (on * _silu(z_ref[rows, cols].astype(F32))).astype(BF16)

    y_dn = _dot(stage_ref[...], wbr_ref[...])
    merged = mr_ref[...].astype(F32) + _sigmoid(gd_ref[...].astype(F32)) * y_dn
    h_ref[...] = x_ref[...] + _dot(merged.astype(BF16), wo_ref[...])


def _dn_tables():
    idx = np.arange(CHUNK)
    r, c = idx[:, None], idx[None, :]
    qmasks = []
    b = 1
    while b < CHUNK:
        qmasks.append(((r // (2 * b)) == (c // (2 * b))) & ((r // b) % 2 == 1) & ((c // b) % 2 == 0))
        b *= 2
    qmask = np.stack(qmasks).astype(np.float32)
    tri = (r >= c).astype(np.float32)
    eb = np.zeros((LANES, DN_V), np.float32)
    eg = np.zeros((LANES, DN_V), np.float32)
    for h in range(DN_HEADS):
        eb[SMALL_BETA0 + h, h * DN_DV:(h + 1) * DN_DV] = 1.0
        eg[SMALL_DECAY0 + h, h * DN_DV:(h + 1) * DN_DV] = 1.0
    return jnp.asarray(eb, BF16), jnp.asarray(eg, BF16), jnp.asarray(tri, BF16), jnp.asarray(qmask, F32)


def _dn_call(p_main, p_small, mr, x2, conv_w, alog_row, dtb_row, dn_norm_g, w_dn_br, w_o, batch, t_len, tt):
    m = batch * t_len
    nt = t_len // tt
    eb, eg, tri, qmask = _dn_tables()
    nlevels = qmask.shape[0]
    row = lambda cb: pl.BlockSpec((tt, COLBLK), lambda b, t: (b * nt + t, cb))
    tok = lambda n: pl.BlockSpec((tt, n), lambda b, t: (b * nt + t, 0))
    const2 = lambda shape: pl.BlockSpec(shape, lambda b, t: (0, 0))
    return pl.pallas_call(
        functools.partial(_dn_kernel, tt // CHUNK),
        grid=(batch, nt),
        in_specs=[
            row(CB_DQ), row(CB_DK), row(CB_DV), row(CB_DZ), row(CB_GATE_D),
            tok(D_MODEL), tok(D_MODEL), tok(LANES),
            const2((SHORT_CONV, 3 * DN_QK)),
            const2((1, LANES)), const2((1, LANES)),
            const2((LANES, DN_V)), const2((LANES, DN_V)),
            const2((CHUNK, CHUNK)),
            pl.BlockSpec((nlevels, CHUNK, CHUNK), lambda b, t: (0, 0, 0)),
            const2((1, DN_DV)),
            const2((DN_V, D_MODEL)), const2((D_MODEL, D_MODEL)),
        ],
        out_specs=tok(D_MODEL),
        out_shape=jax.ShapeDtypeStruct((m, D_MODEL), F32),
        scratch_shapes=[
            pltpu.VMEM((DN_HEADS, DN_DK, DN_DV), F32),
            pltpu.VMEM((3, SUBLANES, DN_QK), F32),
            pltpu.VMEM((tt, DN_V), BF16),
        ],
        compiler_params=pltpu.CompilerParams(
            dimension_semantics=("arbitrary", "arbitrary"), vmem_limit_bytes=VMEM_LIMIT),
        name="deltanet",
    )(p_main, p_main, p_main, p_main, p_main, mr, x2, p_small, conv_w, alog_row, dtb_row, eb, eg, tri, qmask,
      dn_norm_g, w_dn_br, w_o)


FFN_COLS = 256


def _ffn_kernel(h_ref, g_ref, wup_ref, cw_ref, cb_ref, wdn_ref, gf_ref, o_ref, carry_ref):
    @pl.when(pl.program_id(1) == 0)
    def _():
        carry_ref[...] = jnp.zeros_like(carry_ref)

    tm = h_ref.shape[0]
    h = h_ref[...]
    u = ((h * lax.rsqrt(jnp.mean(h * h, axis=-1, keepdims=True) + EPS)) * g_ref[...]).astype(BF16)

    def branch(c0):
        cols = slice(c0, c0 + FFN_COLS)
        up = _dot(u, wup_ref[:, cols])
        prev = carry_ref[:, cols]
        w = cw_ref[:, cols]
        y = up * w[FFN_CONV - 1:FFN_CONV] + cb_ref[:, cols]
        for s in range(1, FFN_CONV):
            y = y + _shift_rows(up, prev, s) * w[FFN_CONV - 1 - s:FFN_CONV - s]
        carry_ref[:, cols] = up[tm - SUBLANES:]
        return y

    acc = h
    for j in range(D_FF // FFN_COLS):
        a = branch(j * FFN_COLS)
        b = branch(D_FF + j * FFN_COLS)
        act = (_silu(a) * b).astype(BF16)
        acc = acc + _dot(act, wdn_ref[j * FFN_COLS:(j + 1) * FFN_COLS, :])

    o_ref[...] = (acc * lax.rsqrt(jnp.mean(acc * acc, axis=-1, keepdims=True) + EPS)) * gf_ref[...]


def _ffn_call(h2, g_ffn, w_up, conv_w, conv_b, w_down, g_final, batch, t_len, tm):
    m = batch * t_len
    nt = t_len // tm
    tok = pl.BlockSpec((tm, D_MODEL), lambda b, t: (b * nt + t, 0))
    const2 = lambda shape: pl.BlockSpec(shape, lambda b, t: (0, 0))
    resident = lambda shape: pl.BlockSpec(shape, lambda b, t: (0, 0), pipeline_mode=pl.Buffered(1))
    return pl.pallas_call(
        _ffn_kernel,
        grid=(batch, nt),
        in_specs=[
            tok, const2((1, D_MODEL)),
            resident((D_MODEL, 2 * D_FF)),
            const2((FFN_CONV, 2 * D_FF)), const2((1, 2 * D_FF)),
            resident((D_FF, D_MODEL)),
            const2((1, D_MODEL)),
        ],
        out_specs=tok,
        out_shape=jax.ShapeDtypeStruct((m, D_MODEL), F32),
        scratch_shapes=[pltpu.VMEM((SUBLANES, 2 * D_FF), F32)],
        compiler_params=pltpu.CompilerParams(
            dimension_semantics=("arbitrary", "arbitrary"), vmem_limit_bytes=VMEM_LIMIT),
        name="convffn",
    )(h2, g_ffn, w_up, conv_w, conv_b, w_down, g_final)


def _pick_tile(t_len, want):
    tile = min(want, t_len)
    assert t_len % tile == 0 and tile % CHUNK == 0
    return tile


def kernel(x, g_mix, w_in, ret_norm_g, dn_conv_w, dn_a_log, dn_dt_bias, dn_norm_g, w_ret_br, w_dn_br, w_o, g_ffn,
           w_up, ffn_conv_w, ffn_conv_b, w_down, g_final):
    batch, t_len, d_model = x.shape
    assert d_model == D_MODEL and g_mix.shape[0] == 1 and t_len % CHUNK == 0
    m = batch * t_len
    x2 = x.astype(F32).reshape(m, D_MODEL)

    small0 = 2 * RET_QK + 2 * RET_V + 2 * DN_QK + 2 * DN_V
    small1 = small0 + 2 * DN_HEADS
    w = w_in[0]
    w_main = jnp.concatenate([w[:, :small0], w[:, small1:]], axis=1).astype(BF16)
    w_small = jnp.pad(w[:, small0:small1], ((0, 0), (0, LANES - 2 * DN_HEADS))).astype(BF16)

    p_main, p_small = _proj_call(x2, g_mix, w_main, w_small, _pick_tile(m, 1024))

    cos_t, sin_t = _rope_tables(t_len)
    mr = _ret_call(p_main, cos_t, sin_t, ret_norm_g, w_ret_br[0].astype(BF16), batch, t_len, _pick_tile(t_len, 256))

    alog_row = jnp.zeros((1, LANES), F32).at[0, SMALL_DECAY0:SMALL_DECAY0 + DN_HEADS].set(dn_a_log[0])
    dtb_row = jnp.zeros((1, LANES), F32).at[0, SMALL_DECAY0:SMALL_DECAY0 + DN_HEADS].set(dn_dt_bias[0])
    h = _dn_call(p_main, p_small, mr, x2, dn_conv_w[0], alog_row, dtb_row, dn_norm_g, w_dn_br[0].astype(BF16),
                 w_o[0].astype(BF16), batch, t_len, _pick_tile(t_len, 256))

    out = _ffn_call(h, g_ffn, w_up[0].astype(BF16), ffn_conv_w[0], ffn_conv_b, w_down[0].astype(BF16),
                    g_final.reshape(1, D_MODEL), batch, t_len, _pick_tile(t_len, 512))
    return out.reshape(batch, t_len, D_MODEL).astype(x.dtype)
```

```python
import functools

import numpy as np
import jax
import jax.numpy as jnp
from jax import lax
from jax.experimental import pallas as pl
from jax.experimental.pallas import tpu as pltpu

D_MODEL = 1024
RET_HEADS = 8
RET_DK = 64
RET_DV = 128
DN_HEADS = 8
DN_DK = 128
DN_DV = 128
CHUNK = 128
SHORT_CONV = 4
FFN_CONV = 3
D_FF = 2816
ROPE_BASE = 10000.0
EPS = 1e-6
GN_EPS = 1e-5

RET_QK = RET_HEADS * RET_DK
RET_V = RET_HEADS * RET_DV
DN_QK = DN_HEADS * DN_DK
DN_V = DN_HEADS * DN_DV

LANES = 128
SUBLANES = 8
COLBLK = 1024
CB_RQK, CB_RV, CB_RGATE, CB_DQ, CB_DK, CB_DV, CB_DZ, CB_GATE_R, CB_GATE_D = range(9)
N_COLBLK = 9
SMALL_BETA0 = 0
SMALL_DECAY0 = 8

VMEM_LIMIT = 56 * 1024 * 1024

F32 = jnp.float32
BF16 = jnp.bfloat16


def _dot(a, b):
    return jnp.dot(a, b, preferred_element_type=F32)


def _dot_nt(a, b):
    return lax.dot_general(a, b, (((1,), (1,)), ((), ())), preferred_element_type=F32)


def _dot_tn(a, b):
    return lax.dot_general(a, b, (((0,), (0,)), ((), ())), preferred_element_type=F32)


def _split3(x):
    hi = x.astype(BF16)
    r1 = x - hi.astype(F32)
    mid = r1.astype(BF16)
    lo = (r1 - mid.astype(F32)).astype(BF16)
    return hi, mid, lo


def _dot_split_lhs(x, b):
    hi, mid, lo = _split3(x)
    return _dot(hi, b) + _dot(mid, b) + _dot(lo, b)


def _dot_split_rhs(a, x):
    hi, mid, lo = _split3(x)
    return _dot(a, hi) + _dot(a, mid) + _dot(a, lo)


def _sigmoid(x):
    return 1.0 / (1.0 + jnp.exp(-x))


def _silu(x):
    return x * _sigmoid(x)


def _softplus(x):
    return jnp.maximum(x, 0.0) + jnp.log(1.0 + jnp.exp(-jnp.abs(x)))


def _shift_rows(x, prev, s):
    r = pltpu.roll(x, s, 0)
    row = lax.broadcasted_iota(jnp.int32, (SUBLANES, x.shape[1]), 0)
    head = jnp.where(row < s, pltpu.roll(prev, s, 0), r[:SUBLANES])
    return jnp.concatenate([head, r[SUBLANES:]], axis=0)


def _proj_kernel(x_ref, g_ref, w_ref, ws_ref, p_ref, ps_ref, u_ref):
    @pl.when(pl.program_id(1) == 0)
    def _():
        x = x_ref[...]
        r = lax.rsqrt(jnp.mean(x * x, axis=-1, keepdims=True) + EPS)
        u = ((x * r) * g_ref[...]).astype(BF16)
        u_ref[...] = u
        ps_ref[...] = _dot(u, ws_ref[...])

    p_ref[...] = _dot(u_ref[...], w_ref[...]).astype(BF16)


def _proj_call(x2, g_mix, w_main, w_small, tm):
    m = x2.shape[0]
    return pl.pallas_call(
        _proj_kernel,
        grid=(m // tm, N_COLBLK),
        in_specs=[
            pl.BlockSpec((tm, D_MODEL), lambda i, j: (i, 0)),
            pl.BlockSpec((1, D_MODEL), lambda i, j: (0, 0)),
            pl.BlockSpec((D_MODEL, COLBLK), lambda i, j: (0, j)),
            pl.BlockSpec((D_MODEL, LANES), lambda i, j: (0, 0)),
        ],
        out_specs=[
            pl.BlockSpec((tm, COLBLK), lambda i, j: (i, j)),
            pl.BlockSpec((tm, LANES), lambda i, j: (i, 0)),
        ],
        out_shape=[
            jax.ShapeDtypeStruct((m, N_COLBLK * COLBLK), BF16),
            jax.ShapeDtypeStruct((m, LANES), F32),
        ],
        scratch_shapes=[pltpu.VMEM((tm, D_MODEL), BF16)],
        compiler_params=pltpu.CompilerParams(
            dimension_semantics=("arbitrary", "arbitrary"), vmem_limit_bytes=VMEM_LIMIT),
        name="proj",
    )(x2, g_mix, w_main, w_small)


def _ret_kernel(decays, nchunk, qk_ref, v_ref, gate_ref, gr_ref, cos_ref, sin_ref, dmat_ref, xi_ref,
                zeta_ref, gn_ref, wbr_ref, mr_ref, s_ref, stage_ref):
    @pl.when(pl.program_id(1) == 0)
    def _():
        s_ref[...] = jnp.zeros_like(s_ref)

    lane = lax.broadcasted_iota(jnp.int32, (CHUNK, LANES), 1)
    first_half = (lane % RET_DK) < (RET_DK // 2)
    head_lo = lane < RET_DK

    def rot(x, cos_t, sin_t):
        partner = jnp.where(first_half, pltpu.roll(x, LANES - RET_DK // 2, 1), pltpu.roll(x, RET_DK // 2, 1))
        return x * cos_t + partner * sin_t

    for c in range(nchunk):
        rows = slice(c * CHUNK, (c + 1) * CHUNK)
        cos_t = cos_ref[rows, :]
        sin_t = sin_ref[rows, :]
        for p in range(RET_HEADS // 2):
            q = rot(qk_ref[rows, p * LANES:(p + 1) * LANES].astype(F32), cos_t, sin_t)
            k = rot(qk_ref[rows, RET_QK + p * LANES:RET_QK + (p + 1) * LANES].astype(F32), cos_t, sin_t)
            k = k * (RET_DK ** -0.5)
            qb = q.astype(BF16)
            kb = k.astype(BF16)
            qx = (q * xi_ref[p]).astype(BF16)
            kz = (k * zeta_ref[p]).astype(BF16)
            for hh in range(2):
                h = 2 * p + hh
                mask = head_lo if hh == 0 else jnp.logical_not(head_lo)
                cols = slice(h * RET_DV, (h + 1) * RET_DV)
                qm = jnp.where(mask, qb, jnp.zeros_like(qb))
                qxm = jnp.where(mask, qx, jnp.zeros_like(qx))
                scores = _dot_nt(qm, kb) * dmat_ref[h]
                vh = v_ref[rows, cols]
                s = s_ref[h]
                ro = _dot(scores.astype(BF16), vh) + _dot(qxm, s.astype(BF16))
                s_ref[h] = s * decays[h] + _dot_tn(kz, vh)
                mu = jnp.mean(ro, axis=-1, keepdims=True)
                d = ro - mu
                var = jnp.mean(d * d, axis=-1, keepdims=True)
                y = (d * lax.rsqrt(var + GN_EPS)) * gn_ref[:, cols] * _silu(gate_ref[rows, cols].astype(F32))
                stage_ref[rows, cols] = y.astype(BF16)

    y_ret = _dot(stage_ref[...], wbr_ref[...])
    mr_ref[...] = (_sigmoid(gr_ref[...].astype(F32)) * y_ret).astype(BF16)


def _ret_tables():
    h = np.arange(RET_HEADS, dtype=np.float64)
    gamma = 1.0 - 2.0 ** (-5.0 - h)
    log_g = np.log(gamma)
    idx = np.arange(CHUNK, dtype=np.float64)
    diff = idx[:, None] - idx[None, :]
    dmat = np.where(diff >= 0, np.exp(log_g[:, None, None] * np.maximum(diff, 0.0)[None]), 0.0)
    zeta = np.exp(log_g[:, None] * (CHUNK - 1.0 - idx)[None, :])
    xi = np.exp(log_g[:, None] * (idx + 1.0)[None, :])
    decays = tuple(float(v) for v in np.exp(log_g * CHUNK))

    def pair_table(t):
        t = t.reshape(RET_HEADS // 2, 2, CHUNK)
        return np.repeat(np.transpose(t, (0, 2, 1)), RET_DK, axis=2)

    return (jnp.asarray(dmat, F32), jnp.asarray(pair_table(xi), F32), jnp.asarray(pair_table(zeta), F32), decays)


def _rope_tables(t_len):
    half = RET_DK // 2
    inv = ROPE_BASE ** (-np.arange(0, RET_DK, 2, dtype=np.float64) / RET_DK)
    ang = np.arange(t_len, dtype=np.float64)[:, None] * inv[None, :]
    cos, sin = np.cos(ang), np.sin(ang)
    cos_t = np.concatenate([cos, cos], axis=1)
    sin_t = np.concatenate([-sin, sin], axis=1)
    assert cos_t.shape[1] == 2 * half
    reps = LANES // RET_DK
    return jnp.asarray(np.tile(cos_t, (1, reps)), F32), jnp.asarray(np.tile(sin_t, (1, reps)), F32)


def _ret_call(p_main, cos_t, sin_t, ret_norm_g, w_ret_br, batch, t_len, tt):
    m = batch * t_len
    nt = t_len // tt
    dmat, xi, zeta, decays = _ret_tables()
    row = lambda cb: pl.BlockSpec((tt, COLBLK), lambda b, t: (b * nt + t, cb))
    const2 = lambda shape: pl.BlockSpec(shape, lambda b, t: (0, 0))
    const3 = lambda shape: pl.BlockSpec(shape, lambda b, t: (0, 0, 0))
    return pl.pallas_call(
        functools.partial(_ret_kernel, decays, tt // CHUNK),
        grid=(batch, nt),
        in_specs=[
            row(CB_RQK), row(CB_RV), row(CB_RGATE), row(CB_GATE_R),
            pl.BlockSpec((tt, LANES), lambda b, t: (t, 0)),
            pl.BlockSpec((tt, LANES), lambda b, t: (t, 0)),
            const3((RET_HEADS, CHUNK, CHUNK)),
            const3((RET_HEADS // 2, CHUNK, LANES)),
            const3((RET_HEADS // 2, CHUNK, LANES)),
            const2((1, RET_V)),
            const2((RET_V, D_MODEL)),
        ],
        out_specs=pl.BlockSpec((tt, D_MODEL), lambda b, t: (b * nt + t, 0)),
        out_shape=jax.ShapeDtypeStruct((m, D_MODEL), BF16),
        scratch_shapes=[
            pltpu.VMEM((RET_HEADS, LANES, RET_DV), F32),
            pltpu.VMEM((tt, RET_V), BF16),
        ],
        compiler_params=pltpu.CompilerParams(
            dimension_semantics=("arbitrary", "arbitrary"), vmem_limit_bytes=VMEM_LIMIT),
        name="retention",
    )(p_main, p_main, p_main, p_main, cos_t, sin_t, dmat, xi, zeta, ret_norm_g, w_ret_br)


def _inv_unit_lower_many(a_list, qmask_ref):
    row = lax.broadcasted_iota(jnp.int32, (CHUNK, CHUNK), 0)
    col = lax.broadcasted_iota(jnp.int32, (CHUNK, CHUNK), 1)
    eye = jnp.where(row == col, 1.0, 0.0).astype(F32)
    a16 = [a.astype(BF16) for a in a_list]
    d = [eye - a * qmask_ref[0] for a in a_list]
    level = 1
    b = 2
    while b < CHUNK:
        mask = qmask_ref[level]
        d16 = [x.astype(BF16) for x in d]
        t16 = [_dot(x, a).astype(BF16) for x, a in zip(d16, a16)]
        d = [x - _dot(t, y) * mask for x, t, y in zip(d, t16, d16)]
        level += 1
        b *= 2
    return d


def _dn_kernel(nchunk, q_ref, k_ref, v_ref, z_ref, gd_ref, mr_ref, x_ref, ps_ref, convw_ref, alog_ref, dtb_ref,
               eb_ref, eg_ref, tri_ref, qmask_ref, ng_ref, wbr_ref, wo_ref, h_ref, s_ref, carry_ref, stage_ref):
    @pl.when(pl.program_id(1) == 0)
    def _():
        s_ref[...] = jnp.zeros_like(s_ref)
        carry_ref[...] = jnp.zeros_like(carry_ref)

    tt = nchunk * CHUNK

    def conv_silu(ref, g):
        x = ref[...].astype(F32)
        prev = carry_ref[g]
        w = convw_ref[:, g * DN_QK:(g + 1) * DN_QK]
        y = x * w[SHORT_CONV - 1:SHORT_CONV]
        for s in range(1, SHORT_CONV):
            y = y + _shift_rows(x, prev, s) * w[SHORT_CONV - 1 - s:SHORT_CONV - s]
        carry_ref[g] = x[tt - SUBLANES:]
        return _silu(y)

    cq = conv_silu(q_ref, 0)
    ck = conv_silu(k_ref, 1)
    cv = conv_silu(v_ref, 2)

    ps = ps_ref[...]
    beta_all = _sigmoid(ps)
    g_all = -jnp.exp(alog_ref[...]) * _softplus(ps + dtb_ref[...])
    beta_b = _dot_split_lhs(beta_all, eb_ref[...])

    row = lax.broadcasted_iota(jnp.int32, (CHUNK, CHUNK), 0)
    col = lax.broadcasted_iota(jnp.int32, (CHUNK, CHUNK), 1)
    causal = row >= col
    strict = row > col
    tri = tri_ref[...]

    g_cum_b, g_cum_t = [], []
    for c in range(nchunk):
        g_cum = _dot_split_rhs(tri, g_all[c * CHUNK:(c + 1) * CHUNK])
        g_cum_b.append(_dot_split_lhs(g_cum, eg_ref[...]))
        g_cum_t.append(g_cum.T)

    items = [(c, h) for c in range(nchunk) for h in range(DN_HEADS)]

    a_list, attn16, rhs16, qdec16, kdec16, sdec = [], [], [], [], [], []
    for c, h in items:
        rows = slice(c * CHUNK, (c + 1) * CHUNK)
        cols = slice(h * DN_DV, (h + 1) * DN_DV)
        gc = g_cum_b[c][:, cols]
        gr = jnp.broadcast_to(g_cum_t[c][SMALL_DECAY0 + h:SMALL_DECAY0 + h + 1, :], (CHUNK, CHUNK))
        g_last = gc[CHUNK - 1:CHUNK, :]
        decay = jnp.where(causal, jnp.exp(gc - gr), 0.0)
        exp_g = jnp.exp(gc)
        q = cq[rows, cols]
        k = ck[rows, cols]
        q = q * (lax.rsqrt(jnp.sum(q * q, axis=-1, keepdims=True) + EPS) * (DN_DK ** -0.5))
        k = k * lax.rsqrt(jnp.sum(k * k, axis=-1, keepdims=True) + EPS)
        beta = beta_b[rows, cols]
        k_beta = k * beta
        k16 = k.astype(BF16)
        a_list.append(jnp.where(strict, _dot_nt(k_beta.astype(BF16), k16) * decay, 0.0))
        attn16.append(jnp.where(causal, _dot_nt(q.astype(BF16), k16) * decay, 0.0).astype(BF16))
        rhs16.append(jnp.concatenate([cv[rows, cols] * beta, k_beta * exp_g], axis=1).astype(BF16))
        qdec16.append((q * exp_g).astype(BF16))
        kdec16.append((k * jnp.exp(g_last - gc)).astype(BF16))
        sdec.append(jnp.exp(g_last))

    minv = _inv_unit_lower_many(a_list, qmask_ref)
    sol = [_dot(m.astype(BF16), r) for m, r in zip(minv, rhs16)]

    for c in range(nchunk):
        idx = [c * DN_HEADS + h for h in range(DN_HEADS)]
        s = [s_ref[h] for h in range(DN_HEADS)]
        s16 = [x.astype(BF16) for x in s]
        ws = [_dot(sol[i][:, DN_DV:].astype(BF16), s16[h]) for h, i in enumerate(idx)]
        qs = [_dot(qdec16[i], s16[h]) for h, i in enumerate(idx)]
        vn16 = [(sol[i][:, :DN_DV] - ws[h]).astype(BF16) for h, i in enumerate(idx)]
        o = [qs[h] + _dot(attn16[i], vn16[h]) for h, i in enumerate(idx)]
        for h, i in enumerate(idx):
            s_ref[h] = s[h] * sdec[i] + _dot_tn(kdec16[i], vn16[h])
        rows = slice(c * CHUNK, (c + 1) * CHUNK)
        for h in range(DN_HEADS):
            cols = slice(h * DN_DV, (h + 1) * DN_DV)
            on = o[h] * lax.rsqrt(jnp.mean(o[h] * o[h], axis=-1, keepdims=True) + EPS) * ng_ref[...]
            stage_ref[rows, cols] = (on * _silu(z_ref[rows, cols].astype(F32))).astype(BF16)

    y_dn = _dot(stage_ref[...], wbr_ref[...])
    merged = mr_ref[...].astype(F32) + _sigmoid(gd_ref[...].astype(F32)) * y_dn
    h_ref[...] = x_ref[...] + _dot(merged.astype(BF16), wo_ref[...])


def _dn_tables():
    idx = np.arange(CHUNK)
    r, c = idx[:, None], idx[None, :]
    qmasks = []
    b = 1
    while b < CHUNK:
        qmasks.append(((r // (2 * b)) == (c // (2 * b))) & ((r // b) % 2 == 1) & ((c // b) % 2 == 0))
        b *= 2
    qmask = np.stack(qmasks).astype(np.float32)
    tri = (r >= c).astype(np.float32)
    eb = np.zeros((LANES, DN_V), np.float32)
    eg = np.zeros((LANES, DN_V), np.float32)
    for h in range(DN_HEADS):
        eb[SMALL_BETA0 + h, h * DN_DV:(h + 1) * DN_DV] = 1.0
        eg[SMALL_DECAY0 + h, h * DN_DV:(h + 1) * DN_DV] = 1.0
    return jnp.asarray(eb, BF16), jnp.asarray(eg, BF16), jnp.asarray(tri, BF16), jnp.asarray(qmask, F32)


def _dn_call(p_main, p_small, mr, x2, conv_w, alog_row, dtb_row, dn_norm_g, w_dn_br, w_o, batch, t_len, tt):
    m = batch * t_len
    nt = t_len // tt
    eb, eg, tri, qmask = _dn_tables()
    nlevels = qmask.shape[0]
    row = lambda cb: pl.BlockSpec((tt, COLBLK), lambda b, t: (b * nt + t, cb))
    tok = lambda n: pl.BlockSpec((tt, n), lambda b, t: (b * nt + t, 0))
    const2 = lambda shape: pl.BlockSpec(shape, lambda b, t: (0, 0))
    return pl.pallas_call(
        functools.partial(_dn_kernel, tt // CHUNK),
        grid=(batch, nt),
        in_specs=[
            row(CB_DQ), row(CB_DK), row(CB_DV), row(CB_DZ), row(CB_GATE_D),
            tok(D_MODEL), tok(D_MODEL), tok(LANES),
            const2((SHORT_CONV, 3 * DN_QK)),
            const2((1, LANES)), const2((1, LANES)),
            const2((LANES, DN_V)), const2((LANES, DN_V)),
            const2((CHUNK, CHUNK)),
            pl.BlockSpec((nlevels, CHUNK, CHUNK), lambda b, t: (0, 0, 0)),
            const2((1, DN_DV)),
            const2((DN_V, D_MODEL)), const2((D_MODEL, D_MODEL)),
        ],
        out_specs=tok(D_MODEL),
        out_shape=jax.ShapeDtypeStruct((m, D_MODEL), F32),
        scratch_shapes=[
            pltpu.VMEM((DN_HEADS, DN_DK, DN_DV), F32),
            pltpu.VMEM((3, SUBLANES, DN_QK), F32),
            pltpu.VMEM((tt, DN_V), BF16),
        ],
        compiler_params=pltpu.CompilerParams(
            dimension_semantics=("arbitrary", "arbitrary"), vmem_limit_bytes=VMEM_LIMIT),
        name="deltanet",
    )(p_main, p_main, p_main, p_main, p_main, mr, x2, p_small, conv_w, alog_row, dtb_row, eb, eg, tri, qmask,
      dn_norm_g, w_dn_br, w_o)


FFN_COLS = 256
FFN_STEPS = D_FF // FFN_COLS


def _ffn_kernel(h_ref, g_ref, wup_ref, cw_ref, cb_ref, wdn_ref, gf_ref, o_ref, carry_ref, up_ref, act_ref):
    @pl.when(pl.program_id(1) == 0)
    def _():
        carry_ref[...] = jnp.zeros_like(carry_ref)

    tm = h_ref.shape[0]
    h = h_ref[...]
    u = ((h * lax.rsqrt(jnp.mean(h * h, axis=-1, keepdims=True) + EPS)) * g_ref[...]).astype(BF16)

    for j in range(FFN_STEPS):
        cols = slice(2 * j * FFN_COLS, 2 * (j + 1) * FFN_COLS)
        buf = up_ref.at[j % 2]
        up = _dot(u, wup_ref[:, cols])
        buf[0:SUBLANES, :] = carry_ref[:, cols]
        buf[SUBLANES:SUBLANES + tm, :] = up
        carry_ref[:, cols] = up[tm - SUBLANES:]
        w = cw_ref[:, cols]
        y = up * w[FFN_CONV - 1:FFN_CONV] + cb_ref[:, cols]
        for s in range(1, FFN_CONV):
            y = y + buf[SUBLANES - s:SUBLANES - s + tm, :] * w[FFN_CONV - 1 - s:FFN_CONV - s]
        act_ref[:, j * FFN_COLS:(j + 1) * FFN_COLS] = (_silu(y[:, :FFN_COLS]) * y[:, FFN_COLS:]).astype(BF16)

    acc = h + _dot(act_ref[...], wdn_ref[...])
    o_ref[...] = (acc * lax.rsqrt(jnp.mean(acc * acc, axis=-1, keepdims=True) + EPS)) * gf_ref[...]


def _ffn_step_major(a):
    lead = a.shape[:-1]
    a = a.reshape(lead + (2, FFN_STEPS, FFN_COLS))
    return jnp.swapaxes(a, -3, -2).reshape(lead + (2 * D_FF,))


def _ffn_call(h2, g_ffn, w_up, conv_w, conv_b, w_down, g_final, batch, t_len, tm):
    m = batch * t_len
    nt = t_len // tm
    tok = pl.BlockSpec((tm, D_MODEL), lambda b, t: (b * nt + t, 0))
    const2 = lambda shape: pl.BlockSpec(shape, lambda b, t: (0, 0))
    resident = lambda shape: pl.BlockSpec(shape, lambda b, t: (0, 0), pipeline_mode=pl.Buffered(1))
    return pl.pallas_call(
        _ffn_kernel,
        grid=(batch, nt),
        in_specs=[
            tok, const2((1, D_MODEL)),
            resident((D_MODEL, 2 * D_FF)),
            const2((FFN_CONV, 2 * D_FF)), const2((1, 2 * D_FF)),
            resident((D_FF, D_MODEL)),
            const2((1, D_MODEL)),
        ],
        out_specs=tok,
        out_shape=jax.ShapeDtypeStruct((m, D_MODEL), F32),
        scratch_shapes=[
            pltpu.VMEM((SUBLANES, 2 * D_FF), F32),
            pltpu.VMEM((2, SUBLANES + tm, 2 * FFN_COLS), F32),
            pltpu.VMEM((tm, D_FF), BF16),
        ],
        compiler_params=pltpu.CompilerParams(
            dimension_semantics=("arbitrary", "arbitrary"), vmem_limit_bytes=VMEM_LIMIT),
        name="convffn",
    )(h2, g_ffn, _ffn_step_major(w_up), _ffn_step_major(conv_w), _ffn_step_major(conv_b), w_down, g_final)


def _pick_tile(t_len, want):
    tile = min(want, t_len)
    assert t_len % tile == 0 and tile % CHUNK == 0
    return tile


def kernel(x, g_mix, w_in, ret_norm_g, dn_conv_w, dn_a_log, dn_dt_bias, dn_norm_g, w_ret_br, w_dn_br, w_o, g_ffn,
           w_up, ffn_conv_w, ffn_conv_b, w_down, g_final):
    batch, t_len, d_model = x.shape
    assert d_model == D_MODEL and g_mix.shape[0] == 1 and t_len % CHUNK == 0
    m = batch * t_len
    x2 = x.astype(F32).reshape(m, D_MODEL)

    small0 = 2 * RET_QK + 2 * RET_V + 2 * DN_QK + 2 * DN_V
    small1 = small0 + 2 * DN_HEADS
    w = w_in[0]
    w_main = jnp.concatenate([w[:, :small0], w[:, small1:]], axis=1).astype(BF16)
    w_small = jnp.pad(w[:, small0:small1], ((0, 0), (0, LANES - 2 * DN_HEADS))).astype(BF16)

    p_main, p_small = _proj_call(x2, g_mix, w_main, w_small, _pick_tile(m, 1024))

    cos_t, sin_t = _rope_tables(t_len)
    mr = _ret_call(p_main, cos_t, sin_t, ret_norm_g, w_ret_br[0].astype(BF16), batch, t_len, _pick_tile(t_len, 256))

    alog_row = jnp.zeros((1, LANES), F32).at[0, SMALL_DECAY0:SMALL_DECAY0 + DN_HEADS].set(dn_a_log[0])
    dtb_row = jnp.zeros((1, LANES), F32).at[0, SMALL_DECAY0:SMALL_DECAY0 + DN_HEADS].set(dn_dt_bias[0])
    h = _dn_call(p_main, p_small, mr, x2, dn_conv_w[0], alog_row, dtb_row, dn_norm_g, w_dn_br[0].astype(BF16),
                 w_o[0].astype(BF16), batch, t_len, _pick_tile(t_len, 256))

    out = _ffn_call(h, g_ffn, w_up[0].astype(BF16), ffn_conv_w[0], ffn_conv_b, w_down[0].astype(BF16),
                    g_final.reshape(1, D_MODEL), batch, t_len, _pick_tile(t_len, 512))
    return out.reshape(batch, t_len, D_MODEL).astype(x.dtype)
```

```python
import functools

import numpy as np
import jax
import jax.numpy as jnp
from jax import lax
from jax.experimental import pallas as pl
from jax.experimental.pallas import tpu as pltpu

D_MODEL = 1024
RET_HEADS = 8
RET_DK = 64
RET_DV = 128
DN_HEADS = 8
DN_DK = 128
DN_DV = 128
CHUNK = 128
SHORT_CONV = 4
FFN_CONV = 3
D_FF = 2816
ROPE_BASE = 10000.0
EPS = 1e-6
GN_EPS = 1e-5

RET_QK = RET_HEADS * RET_DK
RET_V = RET_HEADS * RET_DV
DN_QK = DN_HEADS * DN_DK
DN_V = DN_HEADS * DN_DV

LANES = 128
SUBLANES = 8
COLBLK = 1024
CB_RQK, CB_RV, CB_RGATE, CB_DQ, CB_DK, CB_DV, CB_DZ, CB_GATE_R, CB_GATE_D = range(9)
N_COLBLK = 9
SMALL_BETA0 = 0
SMALL_DECAY0 = 8

VMEM_LIMIT = 56 * 1024 * 1024

F32 = jnp.float32
BF16 = jnp.bfloat16


def _dot(a, b):
    return jnp.dot(a, b, preferred_element_type=F32)


def _dot_nt(a, b):
    return lax.dot_general(a, b, (((1,), (1,)), ((), ())), preferred_element_type=F32)


def _dot_tn(a, b):
    return lax.dot_general(a, b, (((0,), (0,)), ((), ())), preferred_element_type=F32)


def _split3(x):
    hi = x.astype(BF16)
    r1 = x - hi.astype(F32)
    mid = r1.astype(BF16)
    lo = (r1 - mid.astype(F32)).astype(BF16)
    return hi, mid, lo


def _dot_split_lhs(x, b):
    hi, mid, lo = _split3(x)
    return _dot(hi, b) + _dot(mid, b) + _dot(lo, b)


def _dot_split_rhs(a, x):
    hi, mid, lo = _split3(x)
    return _dot(a, hi) + _dot(a, mid) + _dot(a, lo)


def _sigmoid(x):
    return 1.0 / (1.0 + jnp.exp(-x))


def _silu(x):
    return x * _sigmoid(x)


def _softplus(x):
    return jnp.maximum(x, 0.0) + jnp.log(1.0 + jnp.exp(-jnp.abs(x)))


def _shift_rows(x, prev, s):
    r = pltpu.roll(x, s, 0)
    row = lax.broadcasted_iota(jnp.int32, (SUBLANES, x.shape[1]), 0)
    head = jnp.where(row < s, pltpu.roll(prev, s, 0), r[:SUBLANES])
    return jnp.concatenate([head, r[SUBLANES:]], axis=0)


N_HEAD_COLBLK = 7


def _proj_kernel(x_ref, g_ref, wh_ref, wt_ref, ws_ref, p_ref, ps_ref, u_ref):
    j = pl.program_id(1)

    @pl.when(j == 0)
    def _():
        x = x_ref[...]
        r = lax.rsqrt(jnp.mean(x * x, axis=-1, keepdims=True) + EPS)
        u = ((x * r) * g_ref[...]).astype(BF16)
        u_ref[...] = u
        ps_ref[...] = _dot(u, ws_ref[...])

    @pl.when(j < N_HEAD_COLBLK)
    def _():
        p_ref[...] = _dot(u_ref[...], wh_ref[...]).astype(BF16)

    @pl.when(j >= N_HEAD_COLBLK)
    def _():
        p_ref[...] = _dot(u_ref[...], wt_ref[...]).astype(BF16)


def _proj_call(x2, g_mix, w_head, w_tail, w_small, tm):
    m = x2.shape[0]
    assert w_head.shape[1] == N_HEAD_COLBLK * COLBLK and w_tail.shape[1] == (N_COLBLK - N_HEAD_COLBLK) * COLBLK
    return pl.pallas_call(
        _proj_kernel,
        grid=(m // tm, N_COLBLK),
        in_specs=[
            pl.BlockSpec((tm, D_MODEL), lambda i, j: (i, 0)),
            pl.BlockSpec((1, D_MODEL), lambda i, j: (0, 0)),
            pl.BlockSpec((D_MODEL, COLBLK), lambda i, j: (0, jnp.minimum(j, N_HEAD_COLBLK - 1))),
            pl.BlockSpec((D_MODEL, COLBLK), lambda i, j: (0, jnp.maximum(j - N_HEAD_COLBLK, 0))),
            pl.BlockSpec((D_MODEL, LANES), lambda i, j: (0, 0)),
        ],
        out_specs=[
            pl.BlockSpec((tm, COLBLK), lambda i, j: (i, j)),
            pl.BlockSpec((tm, LANES), lambda i, j: (i, 0)),
        ],
        out_shape=[
            jax.ShapeDtypeStruct((m, N_COLBLK * COLBLK), BF16),
            jax.ShapeDtypeStruct((m, LANES), F32),
        ],
        scratch_shapes=[pltpu.VMEM((tm, D_MODEL), BF16)],
        compiler_params=pltpu.CompilerParams(
            dimension_semantics=("arbitrary", "arbitrary"), vmem_limit_bytes=VMEM_LIMIT),
        name="proj",
    )(x2, g_mix, w_head, w_tail, w_small)


def _ret_kernel(decays, nchunk, qk_ref, v_ref, gate_ref, gr_ref, cos_ref, sin_ref, dmat_ref, xi_ref,
                zeta_ref, gn_ref, wbr_ref, mr_ref, s_ref, stage_ref):
    @pl.when(pl.program_id(1) == 0)
    def _():
        s_ref[...] = jnp.zeros_like(s_ref)

    lane = lax.broadcasted_iota(jnp.int32, (CHUNK, LANES), 1)
    first_half = (lane % RET_DK) < (RET_DK // 2)
    head_lo = lane < RET_DK

    def rot(x, cos_t, sin_t):
        partner = jnp.where(first_half, pltpu.roll(x, LANES - RET_DK // 2, 1), pltpu.roll(x, RET_DK // 2, 1))
        return x * cos_t + partner * sin_t

    qb, kb, qx, kz = {}, {}, {}, {}
    for c in range(nchunk):
        rows = slice(c * CHUNK, (c + 1) * CHUNK)
        cos_t = cos_ref[rows, :]
        sin_t = sin_ref[rows, :]
        for p in range(RET_HEADS // 2):
            q = rot(qk_ref[rows, p * LANES:(p + 1) * LANES].astype(F32), cos_t, sin_t)
            k = rot(qk_ref[rows, RET_QK + p * LANES:RET_QK + (p + 1) * LANES].astype(F32), cos_t, sin_t)
            k = k * (RET_DK ** -0.5)
            qb[c, p] = q.astype(BF16)
            kb[c, p] = k.astype(BF16)
            qx[c, p] = (q * xi_ref[p]).astype(BF16)
            kz[c, p] = (k * zeta_ref[p]).astype(BF16)

    items = [(c, h) for c in range(nchunk) for h in range(RET_HEADS)]

    def head_mask(h):
        return head_lo if h % 2 == 0 else jnp.logical_not(head_lo)

    def v_of(c, h):
        return v_ref[c * CHUNK:(c + 1) * CHUNK, h * RET_DV:(h + 1) * RET_DV]

    scores16 = {}
    for c, h in items:
        qm = jnp.where(head_mask(h), qb[c, h // 2], jnp.zeros((CHUNK, LANES), BF16))
        scores16[c, h] = (_dot_nt(qm, kb[c, h // 2]) * dmat_ref[h]).astype(BF16)
    inner = {(c, h): _dot(scores16[c, h], v_of(c, h)) for c, h in items}
    kv = {(c, h): _dot_tn(kz[c, h // 2], v_of(c, h)) for c, h in items}

    for c in range(nchunk):
        rows = slice(c * CHUNK, (c + 1) * CHUNK)
        s = [s_ref[h] for h in range(RET_HEADS)]
        ro = []
        for h in range(RET_HEADS):
            qxm = jnp.where(head_mask(h), qx[c, h // 2], jnp.zeros((CHUNK, LANES), BF16))
            ro.append(inner[c, h] + _dot(qxm, s[h].astype(BF16)))
        for h in range(RET_HEADS):
            s_ref[h] = s[h] * decays[h] + kv[c, h]
        for h in range(RET_HEADS):
            cols = slice(h * RET_DV, (h + 1) * RET_DV)
            mu = jnp.mean(ro[h], axis=-1, keepdims=True)
            d = ro[h] - mu
            var = jnp.mean(d * d, axis=-1, keepdims=True)
            y = (d * lax.rsqrt(var + GN_EPS)) * gn_ref[:, cols] * _silu(gate_ref[rows, cols].astype(F32))
            stage_ref[rows, cols] = y.astype(BF16)

    y_ret = _dot(stage_ref[...], wbr_ref[...])
    mr_ref[...] = (_sigmoid(gr_ref[...].astype(F32)) * y_ret).astype(BF16)


def _ret_tables():
    h = np.arange(RET_HEADS, dtype=np.float64)
    gamma = 1.0 - 2.0 ** (-5.0 - h)
    log_g = np.log(gamma)
    idx = np.arange(CHUNK, dtype=np.float64)
    diff = idx[:, None] - idx[None, :]
    dmat = np.where(diff >= 0, np.exp(log_g[:, None, None] * np.maximum(diff, 0.0)[None]), 0.0)
    zeta = np.exp(log_g[:, None] * (CHUNK - 1.0 - idx)[None, :])
    xi = np.exp(log_g[:, None] * (idx + 1.0)[None, :])
    decays = tuple(float(v) for v in np.exp(log_g * CHUNK))

    def pair_table(t):
        t = t.reshape(RET_HEADS // 2, 2, CHUNK)
        return np.repeat(np.transpose(t, (0, 2, 1)), RET_DK, axis=2)

    return (jnp.asarray(dmat, F32), jnp.asarray(pair_table(xi), F32), jnp.asarray(pair_table(zeta), F32), decays)


def _rope_tables(t_len):
    half = RET_DK // 2
    inv = ROPE_BASE ** (-np.arange(0, RET_DK, 2, dtype=np.float64) / RET_DK)
    ang = np.arange(t_len, dtype=np.float64)[:, None] * inv[None, :]
    cos, sin = np.cos(ang), np.sin(ang)
    cos_t = np.concatenate([cos, cos], axis=1)
    sin_t = np.concatenate([-sin, sin], axis=1)
    assert cos_t.shape[1] == 2 * half
    reps = LANES // RET_DK
    return jnp.asarray(np.tile(cos_t, (1, reps)), F32), jnp.asarray(np.tile(sin_t, (1, reps)), F32)


def _ret_call(p_main, cos_t, sin_t, ret_norm_g, w_ret_br, batch, t_len, tt):
    m = batch * t_len
    nt = t_len // tt
    dmat, xi, zeta, decays = _ret_tables()
    row = lambda cb: pl.BlockSpec((tt, COLBLK), lambda b, t: (b * nt + t, cb))
    const2 = lambda shape: pl.BlockSpec(shape, lambda b, t: (0, 0))
    const3 = lambda shape: pl.BlockSpec(shape, lambda b, t: (0, 0, 0))
    return pl.pallas_call(
        functools.partial(_ret_kernel, decays, tt // CHUNK),
        grid=(batch, nt),
        in_specs=[
            row(CB_RQK), row(CB_RV), row(CB_RGATE), row(CB_GATE_R),
            pl.BlockSpec((tt, LANES), lambda b, t: (t, 0)),
            pl.BlockSpec((tt, LANES), lambda b, t: (t, 0)),
            const3((RET_HEADS, CHUNK, CHUNK)),
            const3((RET_HEADS // 2, CHUNK, LANES)),
            const3((RET_HEADS // 2, CHUNK, LANES)),
            const2((1, RET_V)),
            const2((RET_V, D_MODEL)),
        ],
        out_specs=pl.BlockSpec((tt, D_MODEL), lambda b, t: (b * nt + t, 0)),
        out_shape=jax.ShapeDtypeStruct((m, D_MODEL), BF16),
        scratch_shapes=[
            pltpu.VMEM((RET_HEADS, LANES, RET_DV), F32),
            pltpu.VMEM((tt, RET_V), BF16),
        ],
        compiler_params=pltpu.CompilerParams(
            dimension_semantics=("arbitrary", "arbitrary"), vmem_limit_bytes=VMEM_LIMIT),
        name="retention",
    )(p_main, p_main, p_main, p_main, cos_t, sin_t, dmat, xi, zeta, ret_norm_g, w_ret_br)


def _inv_unit_lower_many(a_list, qmask_ref):
    row = lax.broadcasted_iota(jnp.int32, (CHUNK, CHUNK), 0)
    col = lax.broadcasted_iota(jnp.int32, (CHUNK, CHUNK), 1)
    eye = jnp.where(row == col, 1.0, 0.0).astype(F32)
    a16 = [a.astype(BF16) for a in a_list]
    d = [eye - a * qmask_ref[0] for a in a_list]
    level = 1
    b = 2
    while b < CHUNK:
        mask = qmask_ref[level]
        d16 = [x.astype(BF16) for x in d]
        t16 = [_dot(x, a).astype(BF16) for x, a in zip(d16, a16)]
        d = [x - _dot(t, y) * mask for x, t, y in zip(d, t16, d16)]
        level += 1
        b *= 2
    return d


def _dn_kernel(nchunk, q_ref, k_ref, v_ref, z_ref, gd_ref, mr_ref, x_ref, ps_ref, convw_ref, alog_ref, dtb_ref,
               eb_ref, eg_ref, tri_ref, qmask_ref, ng_ref, wbr_ref, wo_ref, h_ref, s_ref, carry_ref, stage_ref):
    @pl.when(pl.program_id(1) == 0)
    def _():
        s_ref[...] = jnp.zeros_like(s_ref)
        carry_ref[...] = jnp.zeros_like(carry_ref)

    tt = nchunk * CHUNK

    def conv_silu(ref, g):
        x = ref[...].astype(F32)
        prev = carry_ref[g]
        w = convw_ref[:, g * DN_QK:(g + 1) * DN_QK]
        y = x * w[SHORT_CONV - 1:SHORT_CONV]
        for s in range(1, SHORT_CONV):
            y = y + _shift_rows(x, prev, s) * w[SHORT_CONV - 1 - s:SHORT_CONV - s]
        carry_ref[g] = x[tt - SUBLANES:]
        return _silu(y)

    cq = conv_silu(q_ref, 0)
    ck = conv_silu(k_ref, 1)
    cv = conv_silu(v_ref, 2)

    ps = ps_ref[...]
    beta_all = _sigmoid(ps)
    g_all = -jnp.exp(alog_ref[...]) * _softplus(ps + dtb_ref[...])
    beta_b = _dot_split_lhs(beta_all, eb_ref[...])

    row = lax.broadcasted_iota(jnp.int32, (CHUNK, CHUNK), 0)
    col = lax.broadcasted_iota(jnp.int32, (CHUNK, CHUNK), 1)
    causal = row >= col
    strict = row > col
    tri = tri_ref[...]

    g_cum_b, g_cum_t = [], []
    for c in range(nchunk):
        g_cum = _dot_split_rhs(tri, g_all[c * CHUNK:(c + 1) * CHUNK])
        g_cum_b.append(_dot_split_lhs(g_cum, eg_ref[...]))
        g_cum_t.append(g_cum.T)

    items = [(c, h) for c in range(nchunk) for h in range(DN_HEADS)]

    a_list, attn16, rhs16, qdec16, kdec16, sdec = [], [], [], [], [], []
    for c, h in items:
        rows = slice(c * CHUNK, (c + 1) * CHUNK)
        cols = slice(h * DN_DV, (h + 1) * DN_DV)
        gc = g_cum_b[c][:, cols]
        gr = jnp.broadcast_to(g_cum_t[c][SMALL_DECAY0 + h:SMALL_DECAY0 + h + 1, :], (CHUNK, CHUNK))
        g_last = gc[CHUNK - 1:CHUNK, :]
        decay = jnp.where(causal, jnp.exp(gc - gr), 0.0)
        exp_g = jnp.exp(gc)
        q = cq[rows, cols]
        k = ck[rows, cols]
        q = q * (lax.rsqrt(jnp.sum(q * q, axis=-1, keepdims=True) + EPS) * (DN_DK ** -0.5))
        k = k * lax.rsqrt(jnp.sum(k * k, axis=-1, keepdims=True) + EPS)
        beta = beta_b[rows, cols]
        k_beta = k * beta
        k16 = k.astype(BF16)
        a_list.append(jnp.where(strict, _dot_nt(k_beta.astype(BF16), k16) * decay, 0.0))
        attn16.append(jnp.where(causal, _dot_nt(q.astype(BF16), k16) * decay, 0.0).astype(BF16))
        rhs16.append(jnp.concatenate([cv[rows, cols] * beta, k_beta * exp_g], axis=1).astype(BF16))
        qdec16.append((q * exp_g).astype(BF16))
        kdec16.append((k * jnp.exp(g_last - gc)).astype(BF16))
        sdec.append(jnp.exp(g_last))

    minv = _inv_unit_lower_many(a_list, qmask_ref)
    sol = [_dot(m.astype(BF16), r) for m, r in zip(minv, rhs16)]

    for c in range(nchunk):
        idx = [c * DN_HEADS + h for h in range(DN_HEADS)]
        s = [s_ref[h] for h in range(DN_HEADS)]
        s16 = [x.astype(BF16) for x in s]
        ws = [_dot(sol[i][:, DN_DV:].astype(BF16), s16[h]) for h, i in enumerate(idx)]
        qs = [_dot(qdec16[i], s16[h]) for h, i in enumerate(idx)]
        vn16 = [(sol[i][:, :DN_DV] - ws[h]).astype(BF16) for h, i in enumerate(idx)]
        o = [qs[h] + _dot(attn16[i], vn16[h]) for h, i in enumerate(idx)]
        for h, i in enumerate(idx):
            s_ref[h] = s[h] * sdec[i] + _dot_tn(kdec16[i], vn16[h])
        rows = slice(c * CHUNK, (c + 1) * CHUNK)
        for h in range(DN_HEADS):
            cols = slice(h * DN_DV, (h + 1) * DN_DV)
            on = o[h] * lax.rsqrt(jnp.mean(o[h] * o[h], axis=-1, keepdims=True) + EPS) * ng_ref[...]
            stage_ref[rows, cols] = (on * _silu(z_ref[rows, cols].astype(F32))).astype(BF16)

    y_dn = _dot(stage_ref[...], wbr_ref[...])
    merged = mr_ref[...].astype(F32) + _sigmoid(gd_ref[...].astype(F32)) * y_dn
    h_ref[...] = x_ref[...] + _dot(merged.astype(BF16), wo_ref[...])


def _dn_tables():
    idx = np.arange(CHUNK)
    r, c = idx[:, None], idx[None, :]
    qmasks = []
    b = 1
    while b < CHUNK:
        qmasks.append(((r // (2 * b)) == (c // (2 * b))) & ((r // b) % 2 == 1) & ((c // b) % 2 == 0))
        b *= 2
    qmask = np.stack(qmasks).astype(np.float32)
    tri = (r >= c).astype(np.float32)
    eb = np.zeros((LANES, DN_V), np.float32)
    eg = np.zeros((LANES, DN_V), np.float32)
    for h in range(DN_HEADS):
        eb[SMALL_BETA0 + h, h * DN_DV:(h + 1) * DN_DV] = 1.0
        eg[SMALL_DECAY0 + h, h * DN_DV:(h + 1) * DN_DV] = 1.0
    return jnp.asarray(eb, BF16), jnp.asarray(eg, BF16), jnp.asarray(tri, BF16), jnp.asarray(qmask, F32)


def _dn_call(p_main, p_small, mr, x2, conv_w, alog_row, dtb_row, dn_norm_g, w_dn_br, w_o, batch, t_len, tt):
    m = batch * t_len
    nt = t_len // tt
    eb, eg, tri, qmask = _dn_tables()
    nlevels = qmask.shape[0]
    row = lambda cb: pl.BlockSpec((tt, COLBLK), lambda b, t: (b * nt + t, cb))
    tok = lambda n: pl.BlockSpec((tt, n), lambda b, t: (b * nt + t, 0))
    const2 = lambda shape: pl.BlockSpec(shape, lambda b, t: (0, 0))
    return pl.pallas_call(
        functools.partial(_dn_kernel, tt // CHUNK),
        grid=(batch, nt),
        in_specs=[
            row(CB_DQ), row(CB_DK), row(CB_DV), row(CB_DZ), row(CB_GATE_D),
            tok(D_MODEL), tok(D_MODEL), tok(LANES),
            const2((SHORT_CONV, 3 * DN_QK)),
            const2((1, LANES)), const2((1, LANES)),
            const2((LANES, DN_V)), const2((LANES, DN_V)),
            const2((CHUNK, CHUNK)),
            pl.BlockSpec((nlevels, CHUNK, CHUNK), lambda b, t: (0, 0, 0)),
            const2((1, DN_DV)),
            const2((DN_V, D_MODEL)), const2((D_MODEL, D_MODEL)),
        ],
        out_specs=tok(D_MODEL),
        out_shape=jax.ShapeDtypeStruct((m, D_MODEL), F32),
        scratch_shapes=[
            pltpu.VMEM((DN_HEADS, DN_DK, DN_DV), F32),
            pltpu.VMEM((3, SUBLANES, DN_QK), F32),
            pltpu.VMEM((tt, DN_V), BF16),
        ],
        compiler_params=pltpu.CompilerParams(
            dimension_semantics=("arbitrary", "arbitrary"), vmem_limit_bytes=VMEM_LIMIT),
        name="deltanet",
    )(p_main, p_main, p_main, p_main, p_main, mr, x2, p_small, conv_w, alog_row, dtb_row, eb, eg, tri, qmask,
      dn_norm_g, w_dn_br, w_o)


FFN_COLS = 256
FFN_STEPS = D_FF // FFN_COLS


def _ffn_kernel(h_ref, g_ref, wup_ref, cw_ref, cb_ref, wdn_ref, gf_ref, o_ref, carry_ref, up_ref, act_ref):
    @pl.when(pl.program_id(1) == 0)
    def _():
        carry_ref[...] = jnp.zeros_like(carry_ref)

    tm = h_ref.shape[0]
    h = h_ref[...]
    u = ((h * lax.rsqrt(jnp.mean(h * h, axis=-1, keepdims=True) + EPS)) * g_ref[...]).astype(BF16)

    def conv_branch(c0, buf):
        cols = slice(c0, c0 + FFN_COLS)
        up = _dot(u, wup_ref[:, cols])
        buf[0:SUBLANES, :] = carry_ref[:, cols]
        buf[SUBLANES:SUBLANES + tm, :] = up
        carry_ref[:, cols] = up[tm - SUBLANES:]
        w = cw_ref[:, cols]
        y = up * w[FFN_CONV - 1:FFN_CONV] + cb_ref[:, cols]
        for s in range(1, FFN_CONV):
            y = y + buf[SUBLANES - s:SUBLANES - s + tm, :] * w[FFN_CONV - 1 - s:FFN_CONV - s]
        return y

    for j in range(FFN_STEPS):
        a = conv_branch(j * FFN_COLS, up_ref.at[j % 2, 0])
        b = conv_branch(D_FF + j * FFN_COLS, up_ref.at[j % 2, 1])
        act_ref[:, j * FFN_COLS:(j + 1) * FFN_COLS] = (_silu(a) * b).astype(BF16)

    acc = h + _dot(act_ref[...], wdn_ref[...])
    o_ref[...] = (acc * lax.rsqrt(jnp.mean(acc * acc, axis=-1, keepdims=True) + EPS)) * gf_ref[...]


def _ffn_call(h2, g_ffn, w_up, conv_w, conv_b, w_down, g_final, batch, t_len, tm):
    m = batch * t_len
    nt = t_len // tm
    tok = pl.BlockSpec((tm, D_MODEL), lambda b, t: (b * nt + t, 0))
    const2 = lambda shape: pl.BlockSpec(shape, lambda b, t: (0, 0))
    resident = lambda shape: pl.BlockSpec(shape, lambda b, t: (0, 0), pipeline_mode=pl.Buffered(1))
    return pl.pallas_call(
        _ffn_kernel,
        grid=(batch, nt),
        in_specs=[
            tok, const2((1, D_MODEL)),
            resident((D_MODEL, 2 * D_FF)),
            const2((FFN_CONV, 2 * D_FF)), const2((1, 2 * D_FF)),
            resident((D_FF, D_MODEL)),
            const2((1, D_MODEL)),
        ],
        out_specs=tok,
        out_shape=jax.ShapeDtypeStruct((m, D_MODEL), F32),
        scratch_shapes=[
            pltpu.VMEM((SUBLANES, 2 * D_FF), F32),
            pltpu.VMEM((2, 2, SUBLANES + tm, FFN_COLS), F32),
            pltpu.VMEM((tm, D_FF), BF16),
        ],
        compiler_params=pltpu.CompilerParams(
            dimension_semantics=("arbitrary", "arbitrary"), vmem_limit_bytes=VMEM_LIMIT),
        name="convffn",
    )(h2, g_ffn, w_up, conv_w, conv_b, w_down, g_final)


def _pick_tile(t_len, want):
    tile = min(want, t_len)
    assert t_len % tile == 0 and tile % CHUNK == 0
    return tile


def kernel(x, g_mix, w_in, ret_norm_g, dn_conv_w, dn_a_log, dn_dt_bias, dn_norm_g, w_ret_br, w_dn_br, w_o, g_ffn,
           w_up, ffn_conv_w, ffn_conv_b, w_down, g_final):
    batch, t_len, d_model = x.shape
    assert d_model == D_MODEL and g_mix.shape[0] == 1 and t_len % CHUNK == 0
    m = batch * t_len
    x2 = x.astype(F32).reshape(m, D_MODEL)

    small0 = 2 * RET_QK + 2 * RET_V + 2 * DN_QK + 2 * DN_V
    small1 = small0 + 2 * DN_HEADS
    w = w_in[0]
    w_head = w[:, :small0].astype(BF16)
    w_tail = w[:, small1:].astype(BF16)
    w_small = jnp.pad(w[:, small0:small1], ((0, 0), (0, LANES - 2 * DN_HEADS))).astype(BF16)

    p_main, p_small = _proj_call(x2, g_mix, w_head, w_tail, w_small, _pick_tile(m, 1024))

    cos_t, sin_t = _rope_tables(t_len)
    mr = _ret_call(p_main, cos_t, sin_t, ret_norm_g, w_ret_br[0].astype(BF16), batch, t_len, _pick_tile(t_len, 256))

    alog_row = jnp.zeros((1, LANES), F32).at[0, SMALL_DECAY0:SMALL_DECAY0 + DN_HEADS].set(dn_a_log[0])
    dtb_row = jnp.zeros((1, LANES), F32).at[0, SMALL_DECAY0:SMALL_DECAY0 + DN_HEADS].set(dn_dt_bias[0])
    h = _dn_call(p_main, p_small, mr, x2, dn_conv_w[0], alog_row, dtb_row, dn_norm_g, w_dn_br[0].astype(BF16),
                 w_o[0].astype(BF16), batch, t_len, _pick_tile(t_len, 256))

    out = _ffn_call(h, g_ffn, w_up[0].astype(BF16), ffn_conv_w[0], ffn_conv_b, w_down[0].astype(BF16),
                    g_final.reshape(1, D_MODEL), batch, t_len, _pick_tile(t_len, 512))
    return out.reshape(batch, t_len, D_MODEL).astype(x.dtype)
```

```python
import functools

import numpy as np
import jax
import jax.numpy as jnp
from jax import lax
from jax.experimental import pallas as pl
from jax.experimental.pallas import tpu as pltpu

D_MODEL = 1024
RET_HEADS = 8
RET_DK = 64
RET_DV = 128
DN_HEADS = 8
DN_DK = 128
DN_DV = 128
CHUNK = 128
SHORT_CONV = 4
FFN_CONV = 3
D_FF = 2816
ROPE_BASE = 10000.0
EPS = 1e-6
GN_EPS = 1e-5

RET_QK = RET_HEADS * RET_DK
RET_V = RET_HEADS * RET_DV
DN_QK = DN_HEADS * DN_DK
DN_V = DN_HEADS * DN_DV

LANES = 128
SUBLANES = 8
COLBLK = 1024
CB_RQK, CB_RV, CB_RGATE, CB_DQ, CB_DK, CB_DV, CB_DZ, CB_GATE_R, CB_GATE_D = range(9)
N_COLBLK = 9
SMALL_BETA0 = 0
SMALL_DECAY0 = 8

VMEM_LIMIT = 56 * 1024 * 1024

F32 = jnp.float32
BF16 = jnp.bfloat16


def _dot(a, b):
    return jnp.dot(a, b, preferred_element_type=F32)


def _dot_nt(a, b):
    return lax.dot_general(a, b, (((1,), (1,)), ((), ())), preferred_element_type=F32)


def _dot_tn(a, b):
    return lax.dot_general(a, b, (((0,), (0,)), ((), ())), preferred_element_type=F32)


def _split3(x):
    hi = x.astype(BF16)
    r1 = x - hi.astype(F32)
    mid = r1.astype(BF16)
    lo = (r1 - mid.astype(F32)).astype(BF16)
    return hi, mid, lo


def _dot_split_lhs(x, b):
    hi, mid, lo = _split3(x)
    return _dot(hi, b) + _dot(mid, b) + _dot(lo, b)


def _dot_split_rhs(a, x):
    hi, mid, lo = _split3(x)
    return _dot(a, hi) + _dot(a, mid) + _dot(a, lo)


def _sigmoid(x):
    return 1.0 / (1.0 + jnp.exp(-x))


def _silu(x):
    return x * _sigmoid(x)


def _softplus(x):
    return jnp.maximum(x, 0.0) + jnp.log(1.0 + jnp.exp(-jnp.abs(x)))


N_HEAD_COLBLK = 7


def _proj_kernel(tiles_per_seq, x_ref, g_ref, wh_ref, wt_ref, ws_ref, convw_ref, p_ref, ps_ref, u_ref,
                 carry_ref, buf_ref):
    i = pl.program_id(0)
    j = pl.program_id(1)
    tm = x_ref.shape[0]

    @pl.when(j == 0)
    def _():
        x = x_ref[...]
        r = lax.rsqrt(jnp.mean(x * x, axis=-1, keepdims=True) + EPS)
        u = ((x * r) * g_ref[...]).astype(BF16)
        u_ref[...] = u
        ps_ref[...] = _dot(u, ws_ref[...])

    @pl.when(jnp.logical_and(i == 0, j == 0))
    def _():
        carry_ref[...] = jnp.zeros_like(carry_ref)

    @pl.when(j <= CB_RV)
    def _():
        p_ref[...] = _dot(u_ref[...], wh_ref[...]).astype(BF16)

    @pl.when(jnp.logical_or(j == CB_RGATE, j == CB_DZ))
    def _():
        p_ref[...] = _silu(_dot(u_ref[...], wh_ref[...])).astype(BF16)

    @pl.when(j >= N_HEAD_COLBLK)
    def _():
        p_ref[...] = _sigmoid(_dot(u_ref[...], wt_ref[...])).astype(BF16)

    def conv_silu(g):
        acc = _dot(u_ref[...], wh_ref[...])
        prev = carry_ref[g]
        buf_ref[0:SUBLANES, :] = jnp.where(i % tiles_per_seq == 0, jnp.zeros_like(prev), prev)
        buf_ref[SUBLANES:SUBLANES + tm, :] = acc
        carry_ref[g] = acc[tm - SUBLANES:]
        w = convw_ref[g]
        y = acc * w[SHORT_CONV - 1:SHORT_CONV]
        for s in range(1, SHORT_CONV):
            y = y + buf_ref[SUBLANES - s:SUBLANES - s + tm, :] * w[SHORT_CONV - 1 - s:SHORT_CONV - s]
        return _silu(y)

    def l2norm_heads(y, scale):
        for h in range(DN_HEADS):
            cols = slice(h * DN_DK, (h + 1) * DN_DK)
            yh = y[:, cols]
            p_ref[:, cols] = (yh * (lax.rsqrt(jnp.sum(yh * yh, axis=-1, keepdims=True) + EPS) * scale)).astype(BF16)

    @pl.when(j == CB_DQ)
    def _():
        l2norm_heads(conv_silu(0), DN_DK ** -0.5)

    @pl.when(j == CB_DK)
    def _():
        l2norm_heads(conv_silu(1), 1.0)

    @pl.when(j == CB_DV)
    def _():
        p_ref[...] = conv_silu(2).astype(BF16)


def _proj_call(x2, g_mix, w_head, w_tail, w_small, conv_w, t_len, tm):
    m = x2.shape[0]
    assert w_head.shape[1] == N_HEAD_COLBLK * COLBLK and w_tail.shape[1] == (N_COLBLK - N_HEAD_COLBLK) * COLBLK
    assert t_len % tm == 0
    return pl.pallas_call(
        functools.partial(_proj_kernel, t_len // tm),
        grid=(m // tm, N_COLBLK),
        in_specs=[
            pl.BlockSpec((tm, D_MODEL), lambda i, j: (i, 0)),
            pl.BlockSpec((1, D_MODEL), lambda i, j: (0, 0)),
            pl.BlockSpec((D_MODEL, COLBLK), lambda i, j: (0, jnp.minimum(j, N_HEAD_COLBLK - 1))),
            pl.BlockSpec((D_MODEL, COLBLK), lambda i, j: (0, jnp.maximum(j - N_HEAD_COLBLK, 0))),
            pl.BlockSpec((D_MODEL, LANES), lambda i, j: (0, 0)),
            pl.BlockSpec((3, SHORT_CONV, COLBLK), lambda i, j: (0, 0, 0)),
        ],
        out_specs=[
            pl.BlockSpec((tm, COLBLK), lambda i, j: (i, j)),
            pl.BlockSpec((tm, LANES), lambda i, j: (i, 0)),
        ],
        out_shape=[
            jax.ShapeDtypeStruct((m, N_COLBLK * COLBLK), BF16),
            jax.ShapeDtypeStruct((m, LANES), F32),
        ],
        scratch_shapes=[
            pltpu.VMEM((tm, D_MODEL), BF16),
            pltpu.VMEM((3, SUBLANES, COLBLK), F32),
            pltpu.VMEM((SUBLANES + tm, COLBLK), F32),
        ],
        compiler_params=pltpu.CompilerParams(
            dimension_semantics=("arbitrary", "arbitrary"), vmem_limit_bytes=VMEM_LIMIT),
        name="proj",
    )(x2, g_mix, w_head, w_tail, w_small, conv_w)


def _ret_kernel(decays, nchunk, qk_ref, v_ref, gate_ref, gr_ref, cos_ref, sin_ref, dmat_ref, xi_ref,
                zeta_ref, gn_ref, wbr_ref, mr_ref, s_ref, stage_ref):
    @pl.when(pl.program_id(1) == 0)
    def _():
        s_ref[...] = jnp.zeros_like(s_ref)

    lane = lax.broadcasted_iota(jnp.int32, (CHUNK, LANES), 1)
    first_half = (lane % RET_DK) < (RET_DK // 2)
    head_lo = lane < RET_DK

    def rot(x, cos_t, sin_t):
        partner = jnp.where(first_half, pltpu.roll(x, LANES - RET_DK // 2, 1), pltpu.roll(x, RET_DK // 2, 1))
        return x * cos_t + partner * sin_t

    qb, kb, qx, kz = {}, {}, {}, {}
    for c in range(nchunk):
        rows = slice(c * CHUNK, (c + 1) * CHUNK)
        cos_t = cos_ref[rows, :]
        sin_t = sin_ref[rows, :]
        for p in range(RET_HEADS // 2):
            q = rot(qk_ref[rows, p * LANES:(p + 1) * LANES].astype(F32), cos_t, sin_t)
            k = rot(qk_ref[rows, RET_QK + p * LANES:RET_QK + (p + 1) * LANES].astype(F32), cos_t, sin_t)
            k = k * (RET_DK ** -0.5)
            qb[c, p] = q.astype(BF16)
            kb[c, p] = k.astype(BF16)
            qx[c, p] = (q * xi_ref[p]).astype(BF16)
            kz[c, p] = (k * zeta_ref[p]).astype(BF16)

    items = [(c, h) for c in range(nchunk) for h in range(RET_HEADS)]

    def head_mask(h):
        return head_lo if h % 2 == 0 else jnp.logical_not(head_lo)

    def v_of(c, h):
        return v_ref[c * CHUNK:(c + 1) * CHUNK, h * RET_DV:(h + 1) * RET_DV]

    scores16 = {}
    for c, h in items:
        qm = jnp.where(head_mask(h), qb[c, h // 2], jnp.zeros((CHUNK, LANES), BF16))
        scores16[c, h] = (_dot_nt(qm, kb[c, h // 2]) * dmat_ref[h]).astype(BF16)
    inner = {(c, h): _dot(scores16[c, h], v_of(c, h)) for c, h in items}
    kv = {(c, h): _dot_tn(kz[c, h // 2], v_of(c, h)) for c, h in items}

    for c in range(nchunk):
        rows = slice(c * CHUNK, (c + 1) * CHUNK)
        s = [s_ref[h] for h in range(RET_HEADS)]
        ro = []
        for h in range(RET_HEADS):
            qxm = jnp.where(head_mask(h), qx[c, h // 2], jnp.zeros((CHUNK, LANES), BF16))
            ro.append(inner[c, h] + _dot(qxm, s[h].astype(BF16)))
        for h in range(RET_HEADS):
            s_ref[h] = s[h] * decays[h] + kv[c, h]
        for h in range(RET_HEADS):
            cols = slice(h * RET_DV, (h + 1) * RET_DV)
            mu = jnp.mean(ro[h], axis=-1, keepdims=True)
            d = ro[h] - mu
            var = jnp.mean(d * d, axis=-1, keepdims=True)
            y = (d * lax.rsqrt(var + GN_EPS)) * gn_ref[:, cols] * gate_ref[rows, cols].astype(F32)
            stage_ref[rows, cols] = y.astype(BF16)

    y_ret = _dot(stage_ref[...], wbr_ref[...])
    mr_ref[...] = (gr_ref[...].astype(F32) * y_ret).astype(BF16)


def _ret_tables():
    h = np.arange(RET_HEADS, dtype=np.float64)
    gamma = 1.0 - 2.0 ** (-5.0 - h)
    log_g = np.log(gamma)
    idx = np.arange(CHUNK, dtype=np.float64)
    diff = idx[:, None] - idx[None, :]
    dmat = np.where(diff >= 0, np.exp(log_g[:, None, None] * np.maximum(diff, 0.0)[None]), 0.0)
    zeta = np.exp(log_g[:, None] * (CHUNK - 1.0 - idx)[None, :])
    xi = np.exp(log_g[:, None] * (idx + 1.0)[None, :])
    decays = tuple(float(v) for v in np.exp(log_g * CHUNK))

    def pair_table(t):
        t = t.reshape(RET_HEADS // 2, 2, CHUNK)
        return np.repeat(np.transpose(t, (0, 2, 1)), RET_DK, axis=2)

    return (jnp.asarray(dmat, F32), jnp.asarray(pair_table(xi), F32), jnp.asarray(pair_table(zeta), F32), decays)


def _rope_tables(t_len):
    half = RET_DK // 2
    inv = ROPE_BASE ** (-np.arange(0, RET_DK, 2, dtype=np.float64) / RET_DK)
    ang = np.arange(t_len, dtype=np.float64)[:, None] * inv[None, :]
    cos, sin = np.cos(ang), np.sin(ang)
    cos_t = np.concatenate([cos, cos], axis=1)
    sin_t = np.concatenate([-sin, sin], axis=1)
    assert cos_t.shape[1] == 2 * half
    reps = LANES // RET_DK
    return jnp.asarray(np.tile(cos_t, (1, reps)), F32), jnp.asarray(np.tile(sin_t, (1, reps)), F32)


def _ret_call(p_main, cos_t, sin_t, ret_norm_g, w_ret_br, batch, t_len, tt):
    m = batch * t_len
    nt = t_len // tt
    dmat, xi, zeta, decays = _ret_tables()
    row = lambda cb: pl.BlockSpec((tt, COLBLK), lambda b, t: (b * nt + t, cb))
    const2 = lambda shape: pl.BlockSpec(shape, lambda b, t: (0, 0))
    const3 = lambda shape: pl.BlockSpec(shape, lambda b, t: (0, 0, 0))
    return pl.pallas_call(
        functools.partial(_ret_kernel, decays, tt // CHUNK),
        grid=(batch, nt),
        in_specs=[
            row(CB_RQK), row(CB_RV), row(CB_RGATE), row(CB_GATE_R),
            pl.BlockSpec((tt, LANES), lambda b, t: (t, 0)),
            pl.BlockSpec((tt, LANES), lambda b, t: (t, 0)),
            const3((RET_HEADS, CHUNK, CHUNK)),
            const3((RET_HEADS // 2, CHUNK, LANES)),
            const3((RET_HEADS // 2, CHUNK, LANES)),
            const2((1, RET_V)),
            const2((RET_V, D_MODEL)),
        ],
        out_specs=pl.BlockSpec((tt, D_MODEL), lambda b, t: (b * nt + t, 0)),
        out_shape=jax.ShapeDtypeStruct((m, D_MODEL), BF16),
        scratch_shapes=[
            pltpu.VMEM((RET_HEADS, LANES, RET_DV), F32),
            pltpu.VMEM((tt, RET_V), BF16),
        ],
        compiler_params=pltpu.CompilerParams(
            dimension_semantics=("arbitrary", "arbitrary"), vmem_limit_bytes=VMEM_LIMIT),
        name="retention",
    )(p_main, p_main, p_main, p_main, cos_t, sin_t, dmat, xi, zeta, ret_norm_g, w_ret_br)


def _inv_unit_lower_many(a_list, qmask_ref):
    row = lax.broadcasted_iota(jnp.int32, (CHUNK, CHUNK), 0)
    col = lax.broadcasted_iota(jnp.int32, (CHUNK, CHUNK), 1)
    eye = jnp.where(row == col, 1.0, 0.0).astype(F32)
    a16 = [a.astype(BF16) for a in a_list]
    d = [eye - a * qmask_ref[0] for a in a_list]
    level = 1
    b = 2
    while b < CHUNK:
        mask = qmask_ref[level]
        d16 = [x.astype(BF16) for x in d]
        t16 = [_dot(x, a).astype(BF16) for x, a in zip(d16, a16)]
        d = [x - _dot(t, y) * mask for x, t, y in zip(d, t16, d16)]
        level += 1
        b *= 2
    return d


def _dn_kernel(nchunk, q_ref, k_ref, v_ref, z_ref, gd_ref, mr_ref, x_ref, ps_ref, alog_ref, dtb_ref,
               eb_ref, eg_ref, tri_ref, qmask_ref, ng_ref, wbr_ref, wo_ref, h_ref, s_ref, stage_ref):
    @pl.when(pl.program_id(1) == 0)
    def _():
        s_ref[...] = jnp.zeros_like(s_ref)

    ps = ps_ref[...]
    beta_all = _sigmoid(ps)
    g_all = -jnp.exp(alog_ref[...]) * _softplus(ps + dtb_ref[...])
    beta_b = _dot_split_lhs(beta_all, eb_ref[...])

    row = lax.broadcasted_iota(jnp.int32, (CHUNK, CHUNK), 0)
    col = lax.broadcasted_iota(jnp.int32, (CHUNK, CHUNK), 1)
    causal = row >= col
    strict = row > col
    tri = tri_ref[...]

    g_cum_b, g_cum_t = [], []
    for c in range(nchunk):
        g_cum = _dot_split_rhs(tri, g_all[c * CHUNK:(c + 1) * CHUNK])
        g_cum_b.append(_dot_split_lhs(g_cum, eg_ref[...]))
        g_cum_t.append(g_cum.T)

    items = [(c, h) for c in range(nchunk) for h in range(DN_HEADS)]

    a_list, attn16, rhs16, qdec16, kdec16, sdec = [], [], [], [], [], []
    for c, h in items:
        rows = slice(c * CHUNK, (c + 1) * CHUNK)
        cols = slice(h * DN_DV, (h + 1) * DN_DV)
        gc = g_cum_b[c][:, cols]
        gr = jnp.broadcast_to(g_cum_t[c][SMALL_DECAY0 + h:SMALL_DECAY0 + h + 1, :], (CHUNK, CHUNK))
        g_last = gc[CHUNK - 1:CHUNK, :]
        decay = jnp.where(causal, jnp.exp(gc - gr), 0.0)
        exp_g = jnp.exp(gc)
        q16 = q_ref[rows, cols]
        k16 = k_ref[rows, cols]
        q = q16.astype(F32)
        k = k16.astype(F32)
        beta = beta_b[rows, cols]
        k_beta = k * beta
        a_list.append(jnp.where(strict, _dot_nt(k_beta.astype(BF16), k16) * decay, 0.0))
        attn16.append(jnp.where(causal, _dot_nt(q16, k16) * decay, 0.0).astype(BF16))
        rhs16.append(jnp.concatenate([v_ref[rows, cols].astype(F32) * beta, k_beta * exp_g], axis=1).astype(BF16))
        qdec16.append((q * exp_g).astype(BF16))
        kdec16.append((k * jnp.exp(g_last - gc)).astype(BF16))
        sdec.append(jnp.exp(g_last))

    minv = _inv_unit_lower_many(a_list, qmask_ref)
    sol = [_dot(m.astype(BF16), r) for m, r in zip(minv, rhs16)]

    for c in range(nchunk):
        idx = [c * DN_HEADS + h for h in range(DN_HEADS)]
        s = [s_ref[h] for h in range(DN_HEADS)]
        s16 = [x.astype(BF16) for x in s]
        ws = [_dot(sol[i][:, DN_DV:].astype(BF16), s16[h]) for h, i in enumerate(idx)]
        qs = [_dot(qdec16[i], s16[h]) for h, i in enumerate(idx)]
        vn16 = [(sol[i][:, :DN_DV] - ws[h]).astype(BF16) for h, i in enumerate(idx)]
        o = [qs[h] + _dot(attn16[i], vn16[h]) for h, i in enumerate(idx)]
        for h, i in enumerate(idx):
            s_ref[h] = s[h] * sdec[i] + _dot_tn(kdec16[i], vn16[h])
        rows = slice(c * CHUNK, (c + 1) * CHUNK)
        for h in range(DN_HEADS):
            cols = slice(h * DN_DV, (h + 1) * DN_DV)
            on = o[h] * lax.rsqrt(jnp.mean(o[h] * o[h], axis=-1, keepdims=True) + EPS) * ng_ref[...]
            stage_ref[rows, cols] = (on * z_ref[rows, cols].astype(F32)).astype(BF16)

    y_dn = _dot(stage_ref[...], wbr_ref[...])
    merged = mr_ref[...].astype(F32) + gd_ref[...].astype(F32) * y_dn
    h_ref[...] = x_ref[...] + _dot(merged.astype(BF16), wo_ref[...])


def _dn_tables():
    idx = np.arange(CHUNK)
    r, c = idx[:, None], idx[None, :]
    qmasks = []
    b = 1
    while b < CHUNK:
        qmasks.append(((r // (2 * b)) == (c // (2 * b))) & ((r // b) % 2 == 1) & ((c // b) % 2 == 0))
        b *= 2
    qmask = np.stack(qmasks).astype(np.float32)
    tri = (r >= c).astype(np.float32)
    eb = np.zeros((LANES, DN_V), np.float32)
    eg = np.zeros((LANES, DN_V), np.float32)
    for h in range(DN_HEADS):
        eb[SMALL_BETA0 + h, h * DN_DV:(h + 1) * DN_DV] = 1.0
        eg[SMALL_DECAY0 + h, h * DN_DV:(h + 1) * DN_DV] = 1.0
    return jnp.asarray(eb, BF16), jnp.asarray(eg, BF16), jnp.asarray(tri, BF16), jnp.asarray(qmask, F32)


def _dn_call(p_main, p_small, mr, x2, alog_row, dtb_row, dn_norm_g, w_dn_br, w_o, batch, t_len, tt):
    m = batch * t_len
    nt = t_len // tt
    eb, eg, tri, qmask = _dn_tables()
    nlevels = qmask.shape[0]
    row = lambda cb: pl.BlockSpec((tt, COLBLK), lambda b, t: (b * nt + t, cb))
    tok = lambda n: pl.BlockSpec((tt, n), lambda b, t: (b * nt + t, 0))
    const2 = lambda shape: pl.BlockSpec(shape, lambda b, t: (0, 0))
    return pl.pallas_call(
        functools.partial(_dn_kernel, tt // CHUNK),
        grid=(batch, nt),
        in_specs=[
            row(CB_DQ), row(CB_DK), row(CB_DV), row(CB_DZ), row(CB_GATE_D),
            tok(D_MODEL), tok(D_MODEL), tok(LANES),
            const2((1, LANES)), const2((1, LANES)),
            const2((LANES, DN_V)), const2((LANES, DN_V)),
            const2((CHUNK, CHUNK)),
            pl.BlockSpec((nlevels, CHUNK, CHUNK), lambda b, t: (0, 0, 0)),
            const2((1, DN_DV)),
            const2((DN_V, D_MODEL)), const2((D_MODEL, D_MODEL)),
        ],
        out_specs=tok(D_MODEL),
        out_shape=jax.ShapeDtypeStruct((m, D_MODEL), F32),
        scratch_shapes=[
            pltpu.VMEM((DN_HEADS, DN_DK, DN_DV), F32),
            pltpu.VMEM((tt, DN_V), BF16),
        ],
        compiler_params=pltpu.CompilerParams(
            dimension_semantics=("arbitrary", "arbitrary"), vmem_limit_bytes=VMEM_LIMIT),
        name="deltanet",
    )(p_main, p_main, p_main, p_main, p_main, mr, x2, p_small, alog_row, dtb_row, eb, eg, tri, qmask,
      dn_norm_g, w_dn_br, w_o)


FFN_COLS = 256
FFN_STEPS = D_FF // FFN_COLS


def _ffn_kernel(h_ref, g_ref, wup_ref, cw_ref, cb_ref, wdn_ref, gf_ref, o_ref, carry_ref, up_ref, act_ref):
    @pl.when(pl.program_id(1) == 0)
    def _():
        carry_ref[...] = jnp.zeros_like(carry_ref)

    tm = h_ref.shape[0]
    h = h_ref[...]
    u = ((h * lax.rsqrt(jnp.mean(h * h, axis=-1, keepdims=True) + EPS)) * g_ref[...]).astype(BF16)

    def conv_branch(c0, buf):
        cols = slice(c0, c0 + FFN_COLS)
        up = _dot(u, wup_ref[:, cols])
        buf[0:SUBLANES, :] = carry_ref[:, cols]
        buf[SUBLANES:SUBLANES + tm, :] = up
        carry_ref[:, cols] = up[tm - SUBLANES:]
        w = cw_ref[:, cols]
        y = up * w[FFN_CONV - 1:FFN_CONV] + cb_ref[:, cols]
        for s in range(1, FFN_CONV):
            y = y + buf[SUBLANES - s:SUBLANES - s + tm, :] * w[FFN_CONV - 1 - s:FFN_CONV - s]
        return y

    for j in range(FFN_STEPS):
        a = conv_branch(j * FFN_COLS, up_ref.at[j % 2, 0])
        b = conv_branch(D_FF + j * FFN_COLS, up_ref.at[j % 2, 1])
        act_ref[:, j * FFN_COLS:(j + 1) * FFN_COLS] = (_silu(a) * b).astype(BF16)

    acc = h + _dot(act_ref[...], wdn_ref[...])
    o_ref[...] = (acc * lax.rsqrt(jnp.mean(acc * acc, axis=-1, keepdims=True) + EPS)) * gf_ref[...]


def _ffn_call(h2, g_ffn, w_up, conv_w, conv_b, w_down, g_final, batch, t_len, tm):
    m = batch * t_len
    nt = t_len // tm
    tok = pl.BlockSpec((tm, D_MODEL), lambda b, t: (b * nt + t, 0))
    const2 = lambda shape: pl.BlockSpec(shape, lambda b, t: (0, 0))
    resident = lambda shape: pl.BlockSpec(shape, lambda b, t: (0, 0), pipeline_mode=pl.Buffered(1))
    return pl.pallas_call(
        _ffn_kernel,
        grid=(batch, nt),
        in_specs=[
            tok, const2((1, D_MODEL)),
            resident((D_MODEL, 2 * D_FF)),
            const2((FFN_CONV, 2 * D_FF)), const2((1, 2 * D_FF)),
            resident((D_FF, D_MODEL)),
            const2((1, D_MODEL)),
        ],
        out_specs=tok,
        out_shape=jax.ShapeDtypeStruct((m, D_MODEL), F32),
        scratch_shapes=[
            pltpu.VMEM((SUBLANES, 2 * D_FF), F32),
            pltpu.VMEM((2, 2, SUBLANES + tm, FFN_COLS), F32),
            pltpu.VMEM((tm, D_FF), BF16),
        ],
        compiler_params=pltpu.CompilerParams(
            dimension_semantics=("arbitrary", "arbitrary"), vmem_limit_bytes=VMEM_LIMIT),
        name="convffn",
    )(h2, g_ffn, w_up, conv_w, conv_b, w_down, g_final)


def _pick_tile(t_len, want):
    tile = min(want, t_len)
    assert t_len % tile == 0 and tile % CHUNK == 0
    return tile


def kernel(x, g_mix, w_in, ret_norm_g, dn_conv_w, dn_a_log, dn_dt_bias, dn_norm_g, w_ret_br, w_dn_br, w_o, g_ffn,
           w_up, ffn_conv_w, ffn_conv_b, w_down, g_final):
    batch, t_len, d_model = x.shape
    assert d_model == D_MODEL and g_mix.shape[0] == 1 and t_len % CHUNK == 0
    m = batch * t_len
    x2 = x.astype(F32).reshape(m, D_MODEL)

    small0 = 2 * RET_QK + 2 * RET_V + 2 * DN_QK + 2 * DN_V
    small1 = small0 + 2 * DN_HEADS
    w = w_in[0]
    w_head = w[:, :small0].astype(BF16)
    w_tail = w[:, small1:].astype(BF16)
    w_small = jnp.pad(w[:, small0:small1], ((0, 0), (0, LANES - 2 * DN_HEADS))).astype(BF16)

    conv_w = jnp.transpose(dn_conv_w[0].reshape(SHORT_CONV, 3, COLBLK), (1, 0, 2))
    p_main, p_small = _proj_call(x2, g_mix, w_head, w_tail, w_small, conv_w, t_len, _pick_tile(t_len, 1024))

    cos_t, sin_t = _rope_tables(t_len)
    mr = _ret_call(p_main, cos_t, sin_t, ret_norm_g, w_ret_br[0].astype(BF16), batch, t_len, _pick_tile(t_len, 256))

    alog_row = jnp.zeros((1, LANES), F32).at[0, SMALL_DECAY0:SMALL_DECAY0 + DN_HEADS].set(dn_a_log[0])
    dtb_row = jnp.zeros((1, LANES), F32).at[0, SMALL_DECAY0:SMALL_DECAY0 + DN_HEADS].set(dn_dt_bias[0])
    h = _dn_call(p_main, p_small, mr, x2, alog_row, dtb_row, dn_norm_g, w_dn_br[0].astype(BF16),
                 w_o[0].astype(BF16), batch, t_len, _pick_tile(t_len, 256))

    out = _ffn_call(h, g_ffn, w_up[0].astype(BF16), ffn_conv_w[0], ffn_conv_b, w_down[0].astype(BF16),
                    g_final.reshape(1, D_MODEL), batch, t_len, _pick_tile(t_len, 512))
    return out.reshape(batch, t_len, D_MODEL).astype(x.dtype)
```

```python
import functools

import numpy as np
import jax
import jax.numpy as jnp
from jax import lax
from jax.experimental import pallas as pl
from jax.experimental.pallas import tpu as pltpu

D_MODEL = 1024
RET_HEADS = 8
RET_DK = 64
RET_DV = 128
DN_HEADS = 8
DN_DK = 128
DN_DV = 128
CHUNK = 128
SHORT_CONV = 4
FFN_CONV = 3
D_FF = 2816
ROPE_BASE = 10000.0
EPS = 1e-6
GN_EPS = 1e-5

RET_QK = RET_HEADS * RET_DK
RET_V = RET_HEADS * RET_DV
DN_QK = DN_HEADS * DN_DK
DN_V = DN_HEADS * DN_DV

LANES = 128
SUBLANES = 8
COLBLK = 1024
SMALL_BETA0 = 0
SMALL_DECAY0 = 8

VMEM_LIMIT = 56 * 1024 * 1024

F32 = jnp.float32
BF16 = jnp.bfloat16


def _dot(a, b):
    return jnp.dot(a, b, preferred_element_type=F32)


def _dot_nt(a, b):
    return lax.dot_general(a, b, (((1,), (1,)), ((), ())), preferred_element_type=F32)


def _dot_tn(a, b):
    return lax.dot_general(a, b, (((0,), (0,)), ((), ())), preferred_element_type=F32)


def _split3(x):
    hi = x.astype(BF16)
    r1 = x - hi.astype(F32)
    mid = r1.astype(BF16)
    lo = (r1 - mid.astype(F32)).astype(BF16)
    return hi, mid, lo


def _dot_split_lhs(x, b):
    hi, mid, lo = _split3(x)
    return _dot(hi, b) + _dot(mid, b) + _dot(lo, b)


def _dot_split_rhs(a, x):
    hi, mid, lo = _split3(x)
    return _dot(a, hi) + _dot(a, mid) + _dot(a, lo)


def _sigmoid(x):
    return 1.0 / (1.0 + jnp.exp(-x))


def _silu(x):
    return x * _sigmoid(x)


def _softplus(x):
    return jnp.maximum(x, 0.0) + jnp.log(1.0 + jnp.exp(-jnp.abs(x)))


PROJ_STEPS = 4
PLAIN_W = (2 * RET_QK + RET_V) // PROJ_STEPS
GATE_W = (RET_V + DN_V + 2 * D_MODEL) // PROJ_STEPS
CONV_W = (2 * DN_QK + DN_V) // PROJ_STEPS
GATE_RGATE, GATE_DZ, GATE_R, GATE_D = range(4)
CONV_GROUPS = CONV_W // DN_DK
PROJ_ROWS = 256


def _proj_kernel(tiles_per_seq, x_ref, g_ref, w1_ref, w2_ref, w3_ref, ws_ref, convw_ref, o1_ref, o2_ref, o3_ref,
                 ps_ref, u_ref, carry_ref, buf_ref):
    i = pl.program_id(0)
    s = pl.program_id(1)
    tm = x_ref.shape[0]

    @pl.when(s == 0)
    def _():
        x = x_ref[...]
        r = lax.rsqrt(jnp.mean(x * x, axis=-1, keepdims=True) + EPS)
        u = ((x * r) * g_ref[...]).astype(BF16)
        u_ref[...] = u
        ps_ref[...] = _dot(u, ws_ref[...])

    @pl.when(jnp.logical_and(i == 0, s == 0))
    def _():
        carry_ref[...] = jnp.zeros_like(carry_ref)

    prev = carry_ref[s]
    buf_ref[0:SUBLANES, :] = jnp.where(i % tiles_per_seq == 0, jnp.zeros_like(prev), prev)
    w = convw_ref[s]
    gate_is_silu = s <= GATE_DZ

    def products(rb):
        r0 = rb * PROJ_ROWS
        u = u_ref[r0:r0 + PROJ_ROWS, :]
        acc = _dot(u, w3_ref[...])
        buf_ref[SUBLANES + r0:SUBLANES + r0 + PROJ_ROWS, :] = acc
        return acc, _dot(u, w2_ref[...]), _dot(u, w1_ref[...])

    def elementwise(rb, acc, g, p):
        r0 = rb * PROJ_ROWS
        rows = slice(r0, r0 + PROJ_ROWS)
        y = acc * w[SHORT_CONV - 1:SHORT_CONV]
        for t in range(1, SHORT_CONV):
            y = y + buf_ref[SUBLANES + r0 - t:SUBLANES + r0 - t + PROJ_ROWS, :] * w[SHORT_CONV - 1 - t:SHORT_CONV - t]
        y = _silu(y)
        for l in range(CONV_GROUPS):
            head_group = s * CONV_GROUPS + l
            cols = slice(l * DN_DK, (l + 1) * DN_DK)
            yl = y[:, cols]
            r = lax.rsqrt(jnp.sum(yl * yl, axis=-1, keepdims=True) + EPS)
            r = r * jnp.where(head_group < DN_HEADS, DN_DK ** -0.5, 1.0)
            scale = jnp.where(head_group < 2 * DN_HEADS, r, 1.0)
            o3_ref[rows, cols] = (yl * scale).astype(BF16)
        o2_ref[rows, :] = (_sigmoid(g) * jnp.where(gate_is_silu, g, 1.0)).astype(BF16)
        o1_ref[rows, :] = p.astype(BF16)

    nrb = tm // PROJ_ROWS
    pending = products(0)
    for rb in range(1, nrb):
        nxt = products(rb)
        elementwise(rb - 1, *pending)
        pending = nxt
    elementwise(nrb - 1, *pending)
    carry_ref[s] = buf_ref[tm:tm + SUBLANES, :]


def _proj_call(x2, g_mix, w_plain, w_gate, w_conv, w_small, conv_w, t_len, tm):
    m = x2.shape[0]
    assert t_len % tm == 0
    wspec = lambda n: pl.BlockSpec((D_MODEL, n), lambda i, s: (0, s))
    ospec = lambda n: pl.BlockSpec((tm, n), lambda i, s: (i, s))
    return pl.pallas_call(
        functools.partial(_proj_kernel, t_len // tm),
        grid=(m // tm, PROJ_STEPS),
        in_specs=[
            pl.BlockSpec((tm, D_MODEL), lambda i, s: (i, 0)),
            pl.BlockSpec((1, D_MODEL), lambda i, s: (0, 0)),
            wspec(PLAIN_W), wspec(GATE_W), wspec(CONV_W),
            pl.BlockSpec((D_MODEL, LANES), lambda i, s: (0, 0)),
            pl.BlockSpec((PROJ_STEPS, SHORT_CONV, CONV_W), lambda i, s: (0, 0, 0)),
        ],
        out_specs=[
            ospec(PLAIN_W), ospec(GATE_W), ospec(CONV_W),
            pl.BlockSpec((tm, LANES), lambda i, s: (i, 0)),
        ],
        out_shape=[
            jax.ShapeDtypeStruct((m, PROJ_STEPS * PLAIN_W), BF16),
            jax.ShapeDtypeStruct((m, PROJ_STEPS * GATE_W), BF16),
            jax.ShapeDtypeStruct((m, PROJ_STEPS * CONV_W), BF16),
            jax.ShapeDtypeStruct((m, LANES), F32),
        ],
        scratch_shapes=[
            pltpu.VMEM((tm, D_MODEL), BF16),
            pltpu.VMEM((PROJ_STEPS, SUBLANES, CONV_W), F32),
            pltpu.VMEM((SUBLANES + tm, CONV_W), F32),
        ],
        compiler_params=pltpu.CompilerParams(
            dimension_semantics=("arbitrary", "arbitrary"), vmem_limit_bytes=VMEM_LIMIT),
        name="proj",
    )(x2, g_mix, w_plain, w_gate, w_conv, w_small, conv_w)


def _ret_kernel(decays, nchunk, qk_ref, v_ref, gate_ref, gr_ref, cos_ref, sin_ref, dmat_ref, xi_ref,
                zeta_ref, gn_ref, wbr_ref, mr_ref, s_ref, stage_ref):
    @pl.when(pl.program_id(1) == 0)
    def _():
        s_ref[...] = jnp.zeros_like(s_ref)

    lane = lax.broadcasted_iota(jnp.int32, (CHUNK, LANES), 1)
    first_half = (lane % RET_DK) < (RET_DK // 2)
    head_lo = lane < RET_DK

    def rot(x, cos_t, sin_t):
        partner = jnp.where(first_half, pltpu.roll(x, LANES - RET_DK // 2, 1), pltpu.roll(x, RET_DK // 2, 1))
        return x * cos_t + partner * sin_t

    qb, kb, qx, kz = {}, {}, {}, {}
    for c in range(nchunk):
        rows = slice(c * CHUNK, (c + 1) * CHUNK)
        cos_t = cos_ref[rows, :]
        sin_t = sin_ref[rows, :]
        for p in range(RET_HEADS // 2):
            q = rot(qk_ref[rows, p * LANES:(p + 1) * LANES].astype(F32), cos_t, sin_t)
            k = rot(qk_ref[rows, RET_QK + p * LANES:RET_QK + (p + 1) * LANES].astype(F32), cos_t, sin_t)
            k = k * (RET_DK ** -0.5)
            qb[c, p] = q.astype(BF16)
            kb[c, p] = k.astype(BF16)
            qx[c, p] = (q * xi_ref[p]).astype(BF16)
            kz[c, p] = (k * zeta_ref[p]).astype(BF16)

    items = [(c, h) for c in range(nchunk) for h in range(RET_HEADS)]

    def head_mask(h):
        return head_lo if h % 2 == 0 else jnp.logical_not(head_lo)

    def v_of(c, h):
        return v_ref[c * CHUNK:(c + 1) * CHUNK, h * RET_DV:(h + 1) * RET_DV]

    scores16 = {}
    for c, h in items:
        qm = jnp.where(head_mask(h), qb[c, h // 2], jnp.zeros((CHUNK, LANES), BF16))
        scores16[c, h] = (_dot_nt(qm, kb[c, h // 2]) * dmat_ref[h]).astype(BF16)
    inner = {(c, h): _dot(scores16[c, h], v_of(c, h)) for c, h in items}
    kv = {(c, h): _dot_tn(kz[c, h // 2], v_of(c, h)) for c, h in items}

    for c in range(nchunk):
        rows = slice(c * CHUNK, (c + 1) * CHUNK)
        s = [s_ref[h] for h in range(RET_HEADS)]
        ro = []
        for h in range(RET_HEADS):
            qxm = jnp.where(head_mask(h), qx[c, h // 2], jnp.zeros((CHUNK, LANES), BF16))
            ro.append(inner[c, h] + _dot(qxm, s[h].astype(BF16)))
        for h in range(RET_HEADS):
            s_ref[h] = s[h] * decays[h] + kv[c, h]
        for h in range(RET_HEADS):
            cols = slice(h * RET_DV, (h + 1) * RET_DV)
            mu = jnp.mean(ro[h], axis=-1, keepdims=True)
            d = ro[h] - mu
            var = jnp.mean(d * d, axis=-1, keepdims=True)
            y = (d * lax.rsqrt(var + GN_EPS)) * gn_ref[:, cols] * gate_ref[rows, cols].astype(F32)
            stage_ref[rows, cols] = y.astype(BF16)

    y_ret = _dot(stage_ref[...], wbr_ref[...])
    mr_ref[...] = (gr_ref[...].astype(F32) * y_ret).astype(BF16)


def _ret_tables():
    h = np.arange(RET_HEADS, dtype=np.float64)
    gamma = 1.0 - 2.0 ** (-5.0 - h)
    log_g = np.log(gamma)
    idx = np.arange(CHUNK, dtype=np.float64)
    diff = idx[:, None] - idx[None, :]
    dmat = np.where(diff >= 0, np.exp(log_g[:, None, None] * np.maximum(diff, 0.0)[None]), 0.0)
    zeta = np.exp(log_g[:, None] * (CHUNK - 1.0 - idx)[None, :])
    xi = np.exp(log_g[:, None] * (idx + 1.0)[None, :])
    decays = tuple(float(v) for v in np.exp(log_g * CHUNK))

    def pair_table(t):
        t = t.reshape(RET_HEADS // 2, 2, CHUNK)
        return np.repeat(np.transpose(t, (0, 2, 1)), RET_DK, axis=2)

    return (jnp.asarray(dmat, F32), jnp.asarray(pair_table(xi), F32), jnp.asarray(pair_table(zeta), F32), decays)


def _rope_tables(t_len):
    half = RET_DK // 2
    inv = ROPE_BASE ** (-np.arange(0, RET_DK, 2, dtype=np.float64) / RET_DK)
    ang = np.arange(t_len, dtype=np.float64)[:, None] * inv[None, :]
    cos, sin = np.cos(ang), np.sin(ang)
    cos_t = np.concatenate([cos, cos], axis=1)
    sin_t = np.concatenate([-sin, sin], axis=1)
    assert cos_t.shape[1] == 2 * half
    reps = LANES // RET_DK
    return jnp.asarray(np.tile(cos_t, (1, reps)), F32), jnp.asarray(np.tile(sin_t, (1, reps)), F32)


def _ret_call(p_plain, p_gate, cos_t, sin_t, ret_norm_g, w_ret_br, batch, t_len, tt):
    m = batch * t_len
    nt = t_len // tt
    dmat, xi, zeta, decays = _ret_tables()
    row = lambda cb: pl.BlockSpec((tt, COLBLK), lambda b, t: (b * nt + t, cb))
    const2 = lambda shape: pl.BlockSpec(shape, lambda b, t: (0, 0))
    const3 = lambda shape: pl.BlockSpec(shape, lambda b, t: (0, 0, 0))
    return pl.pallas_call(
        functools.partial(_ret_kernel, decays, tt // CHUNK),
        grid=(batch, nt),
        in_specs=[
            row(0), row(1), row(GATE_RGATE), row(GATE_R),
            pl.BlockSpec((tt, LANES), lambda b, t: (t, 0)),
            pl.BlockSpec((tt, LANES), lambda b, t: (t, 0)),
            const3((RET_HEADS, CHUNK, CHUNK)),
            const3((RET_HEADS // 2, CHUNK, LANES)),
            const3((RET_HEADS // 2, CHUNK, LANES)),
            const2((1, RET_V)),
            const2((RET_V, D_MODEL)),
        ],
        out_specs=pl.BlockSpec((tt, D_MODEL), lambda b, t: (b * nt + t, 0)),
        out_shape=jax.ShapeDtypeStruct((m, D_MODEL), BF16),
        scratch_shapes=[
            pltpu.VMEM((RET_HEADS, LANES, RET_DV), F32),
            pltpu.VMEM((tt, RET_V), BF16),
        ],
        compiler_params=pltpu.CompilerParams(
            dimension_semantics=("arbitrary", "arbitrary"), vmem_limit_bytes=VMEM_LIMIT),
        name="retention",
    )(p_plain, p_plain, p_gate, p_gate, cos_t, sin_t, dmat, xi, zeta, ret_norm_g, w_ret_br)


def _inv_unit_lower_many(a_list, qmask_ref):
    row = lax.broadcasted_iota(jnp.int32, (CHUNK, CHUNK), 0)
    col = lax.broadcasted_iota(jnp.int32, (CHUNK, CHUNK), 1)
    eye = jnp.where(row == col, 1.0, 0.0).astype(F32)
    a16 = [a.astype(BF16) for a in a_list]
    d = [eye - a * qmask_ref[0] for a in a_list]
    level = 1
    b = 2
    while b < CHUNK:
        mask = qmask_ref[level]
        d16 = [x.astype(BF16) for x in d]
        t16 = [_dot(x, a).astype(BF16) for x, a in zip(d16, a16)]
        d = [x - _dot(t, y) * mask for x, t, y in zip(d, t16, d16)]
        level += 1
        b *= 2
    return d


def _dn_kernel(nchunk, q_ref, k_ref, v_ref, z_ref, gd_ref, mr_ref, x_ref, ps_ref, alog_ref, dtb_ref,
               eb_ref, eg_ref, tri_ref, qmask_ref, ng_ref, wbr_ref, wo_ref, h_ref, s_ref, stage_ref):
    @pl.when(pl.program_id(1) == 0)
    def _():
        s_ref[...] = jnp.zeros_like(s_ref)

    ps = ps_ref[...]
    beta_all = _sigmoid(ps)
    g_all = -jnp.exp(alog_ref[...]) * _softplus(ps + dtb_ref[...])
    beta_b = _dot_split_lhs(beta_all, eb_ref[...])

    row = lax.broadcasted_iota(jnp.int32, (CHUNK, CHUNK), 0)
    col = lax.broadcasted_iota(jnp.int32, (CHUNK, CHUNK), 1)
    causal = row >= col
    strict = row > col
    tri = tri_ref[...]

    g_cum_b, g_cum_t = [], []
    for c in range(nchunk):
        g_cum = _dot_split_rhs(tri, g_all[c * CHUNK:(c + 1) * CHUNK])
        g_cum_b.append(_dot_split_lhs(g_cum, eg_ref[...]))
        g_cum_t.append(g_cum.T)

    items = [(c, h) for c in range(nchunk) for h in range(DN_HEADS)]

    a_list, attn16, rhs16, qdec16, kdec16, sdec = [], [], [], [], [], []
    for c, h in items:
        rows = slice(c * CHUNK, (c + 1) * CHUNK)
        cols = slice(h * DN_DV, (h + 1) * DN_DV)
        gc = g_cum_b[c][:, cols]
        gr = jnp.broadcast_to(g_cum_t[c][SMALL_DECAY0 + h:SMALL_DECAY0 + h + 1, :], (CHUNK, CHUNK))
        g_last = gc[CHUNK - 1:CHUNK, :]
        decay = jnp.where(causal, jnp.exp(gc - gr), 0.0)
        exp_g = jnp.exp(gc)
        q16 = q_ref[rows, cols]
        k16 = k_ref[rows, cols]
        q = q16.astype(F32)
        k = k16.astype(F32)
        beta = beta_b[rows, cols]
        k_beta = k * beta
        a_list.append(jnp.where(strict, _dot_nt(k_beta.astype(BF16), k16) * decay, 0.0))
        attn16.append(jnp.where(causal, _dot_nt(q16, k16) * decay, 0.0).astype(BF16))
        rhs16.append(jnp.concatenate([v_ref[rows, cols].astype(F32) * beta, k_beta * exp_g], axis=1).astype(BF16))
        qdec16.append((q * exp_g).astype(BF16))
        kdec16.append((k * jnp.exp(g_last - gc)).astype(BF16))
        sdec.append(jnp.exp(g_last))

    minv = _inv_unit_lower_many(a_list, qmask_ref)
    sol = [_dot(m.astype(BF16), r) for m, r in zip(minv, rhs16)]

    for c in range(nchunk):
        idx = [c * DN_HEADS + h for h in range(DN_HEADS)]
        s = [s_ref[h] for h in range(DN_HEADS)]
        s16 = [x.astype(BF16) for x in s]
        ws = [_dot(sol[i][:, DN_DV:].astype(BF16), s16[h]) for h, i in enumerate(idx)]
        qs = [_dot(qdec16[i], s16[h]) for h, i in enumerate(idx)]
        vn16 = [(sol[i][:, :DN_DV] - ws[h]).astype(BF16) for h, i in enumerate(idx)]
        o = [qs[h] + _dot(attn16[i], vn16[h]) for h, i in enumerate(idx)]
        for h, i in enumerate(idx):
            s_ref[h] = s[h] * sdec[i] + _dot_tn(kdec16[i], vn16[h])
        rows = slice(c * CHUNK, (c + 1) * CHUNK)
        for h in range(DN_HEADS):
            cols = slice(h * DN_DV, (h + 1) * DN_DV)
            on = o[h] * lax.rsqrt(jnp.mean(o[h] * o[h], axis=-1, keepdims=True) + EPS) * ng_ref[...]
            stage_ref[rows, cols] = (on * z_ref[rows, cols].astype(F32)).astype(BF16)

    y_dn = _dot(stage_ref[...], wbr_ref[...])
    merged = mr_ref[...].astype(F32) + gd_ref[...].astype(F32) * y_dn
    h_ref[...] = x_ref[...] + _dot(merged.astype(BF16), wo_ref[...])


def _dn_tables():
    idx = np.arange(CHUNK)
    r, c = idx[:, None], idx[None, :]
    qmasks = []
    b = 1
    while b < CHUNK:
        qmasks.append(((r // (2 * b)) == (c // (2 * b))) & ((r // b) % 2 == 1) & ((c // b) % 2 == 0))
        b *= 2
    qmask = np.stack(qmasks).astype(np.float32)
    tri = (r >= c).astype(np.float32)
    eb = np.zeros((LANES, DN_V), np.float32)
    eg = np.zeros((LANES, DN_V), np.float32)
    for h in range(DN_HEADS):
        eb[SMALL_BETA0 + h, h * DN_DV:(h + 1) * DN_DV] = 1.0
        eg[SMALL_DECAY0 + h, h * DN_DV:(h + 1) * DN_DV] = 1.0
    return jnp.asarray(eb, BF16), jnp.asarray(eg, BF16), jnp.asarray(tri, BF16), jnp.asarray(qmask, F32)


def _dn_call(p_conv, p_gate, p_small, mr, x2, alog_row, dtb_row, dn_norm_g, w_dn_br, w_o, batch, t_len, tt):
    m = batch * t_len
    nt = t_len // tt
    eb, eg, tri, qmask = _dn_tables()
    nlevels = qmask.shape[0]
    row = lambda cb: pl.BlockSpec((tt, COLBLK), lambda b, t: (b * nt + t, cb))
    tok = lambda n: pl.BlockSpec((tt, n), lambda b, t: (b * nt + t, 0))
    const2 = lambda shape: pl.BlockSpec(shape, lambda b, t: (0, 0))
    return pl.pallas_call(
        functools.partial(_dn_kernel, tt // CHUNK),
        grid=(batch, nt),
        in_specs=[
            row(0), row(1), row(2), row(GATE_DZ), row(GATE_D),
            tok(D_MODEL), tok(D_MODEL), tok(LANES),
            const2((1, LANES)), const2((1, LANES)),
            const2((LANES, DN_V)), const2((LANES, DN_V)),
            const2((CHUNK, CHUNK)),
            pl.BlockSpec((nlevels, CHUNK, CHUNK), lambda b, t: (0, 0, 0)),
            const2((1, DN_DV)),
            const2((DN_V, D_MODEL)), const2((D_MODEL, D_MODEL)),
        ],
        out_specs=tok(D_MODEL),
        out_shape=jax.ShapeDtypeStruct((m, D_MODEL), F32),
        scratch_shapes=[
            pltpu.VMEM((DN_HEADS, DN_DK, DN_DV), F32),
            pltpu.VMEM((tt, DN_V), BF16),
        ],
        compiler_params=pltpu.CompilerParams(
            dimension_semantics=("arbitrary", "arbitrary"), vmem_limit_bytes=VMEM_LIMIT),
        name="deltanet",
    )(p_conv, p_conv, p_conv, p_gate, p_gate, mr, x2, p_small, alog_row, dtb_row, eb, eg, tri, qmask,
      dn_norm_g, w_dn_br, w_o)


FFN_COLS = 256
FFN_STEPS = D_FF // FFN_COLS


def _ffn_kernel(h_ref, g_ref, wup_ref, cw_ref, cb_ref, wdn_ref, gf_ref, o_ref, carry_ref, up_ref, act_ref):
    @pl.when(pl.program_id(1) == 0)
    def _():
        carry_ref[...] = jnp.zeros_like(carry_ref)

    tm = h_ref.shape[0]
    h = h_ref[...]
    u = ((h * lax.rsqrt(jnp.mean(h * h, axis=-1, keepdims=True) + EPS)) * g_ref[...]).astype(BF16)

    def conv_branch(c0, buf):
        cols = slice(c0, c0 + FFN_COLS)
        up = _dot(u, wup_ref[:, cols])
        buf[0:SUBLANES, :] = carry_ref[:, cols]
        buf[SUBLANES:SUBLANES + tm, :] = up
        carry_ref[:, cols] = up[tm - SUBLANES:]
        w = cw_ref[:, cols]
        y = up * w[FFN_CONV - 1:FFN_CONV] + cb_ref[:, cols]
        for s in range(1, FFN_CONV):
            y = y + buf[SUBLANES - s:SUBLANES - s + tm, :] * w[FFN_CONV - 1 - s:FFN_CONV - s]
        return y

    for j in range(FFN_STEPS):
        a = conv_branch(j * FFN_COLS, up_ref.at[j % 2, 0])
        b = conv_branch(D_FF + j * FFN_COLS, up_ref.at[j % 2, 1])
        act_ref[:, j * FFN_COLS:(j + 1) * FFN_COLS] = (_silu(a) * b).astype(BF16)

    acc = h + _dot(act_ref[...], wdn_ref[...])
    o_ref[...] = (acc * lax.rsqrt(jnp.mean(acc * acc, axis=-1, keepdims=True) + EPS)) * gf_ref[...]


def _ffn_call(h2, g_ffn, w_up, conv_w, conv_b, w_down, g_final, batch, t_len, tm):
    m = batch * t_len
    nt = t_len // tm
    tok = pl.BlockSpec((tm, D_MODEL), lambda b, t: (b * nt + t, 0))
    const2 = lambda shape: pl.BlockSpec(shape, lambda b, t: (0, 0))
    resident = lambda shape: pl.BlockSpec(shape, lambda b, t: (0, 0), pipeline_mode=pl.Buffered(1))
    return pl.pallas_call(
        _ffn_kernel,
        grid=(batch, nt),
        in_specs=[
            tok, const2((1, D_MODEL)),
            resident((D_MODEL, 2 * D_FF)),
            const2((FFN_CONV, 2 * D_FF)), const2((1, 2 * D_FF)),
            resident((D_FF, D_MODEL)),
            const2((1, D_MODEL)),
        ],
        out_specs=tok,
        out_shape=jax.ShapeDtypeStruct((m, D_MODEL), F32),
        scratch_shapes=[
            pltpu.VMEM((SUBLANES, 2 * D_FF), F32),
            pltpu.VMEM((2, 2, SUBLANES + tm, FFN_COLS), F32),
            pltpu.VMEM((tm, D_FF), BF16),
        ],
        compiler_params=pltpu.CompilerParams(
            dimension_semantics=("arbitrary", "arbitrary"), vmem_limit_bytes=VMEM_LIMIT),
        name="convffn",
    )(h2, g_ffn, w_up, conv_w, conv_b, w_down, g_final)


def _pick_tile(t_len, want):
    tile = min(want, t_len)
    assert t_len % tile == 0 and tile % CHUNK == 0
    return tile


def kernel(x, g_mix, w_in, ret_norm_g, dn_conv_w, dn_a_log, dn_dt_bias, dn_norm_g, w_ret_br, w_dn_br, w_o, g_ffn,
           w_up, ffn_conv_w, ffn_conv_b, w_down, g_final):
    batch, t_len, d_model = x.shape
    assert d_model == D_MODEL and g_mix.shape[0] == 1 and t_len % CHUNK == 0
    m = batch * t_len
    x2 = x.astype(F32).reshape(m, D_MODEL)

    c_plain = 2 * RET_QK + RET_V
    c_rgate = c_plain + RET_V
    c_conv = c_rgate + 2 * DN_QK + DN_V
    c_dz = c_conv + DN_V
    c_small = c_dz + 2 * DN_HEADS
    w = w_in[0]
    w_plain = w[:, :c_plain].astype(BF16)
    w_gate = jnp.concatenate([w[:, c_plain:c_rgate], w[:, c_conv:c_dz], w[:, c_small:]], axis=1).astype(BF16)
    w_conv = w[:, c_rgate:c_conv].astype(BF16)
    w_small = jnp.pad(w[:, c_dz:c_small], ((0, 0), (0, LANES - 2 * DN_HEADS))).astype(BF16)

    conv_w = jnp.transpose(dn_conv_w[0].reshape(SHORT_CONV, PROJ_STEPS, CONV_W), (1, 0, 2))
    p_plain, p_gate, p_conv, p_small = _proj_call(x2, g_mix, w_plain, w_gate, w_conv, w_small, conv_w, t_len,
                                                  _pick_tile(t_len, 1024))

    cos_t, sin_t = _rope_tables(t_len)
    mr = _ret_call(p_plain, p_gate, cos_t, sin_t, ret_norm_g, w_ret_br[0].astype(BF16), batch, t_len, _pick_tile(t_len, 256))

    alog_row = jnp.zeros((1, LANES), F32).at[0, SMALL_DECAY0:SMALL_DECAY0 + DN_HEADS].set(dn_a_log[0])
    dtb_row = jnp.zeros((1, LANES), F32).at[0, SMALL_DECAY0:SMALL_DECAY0 + DN_HEADS].set(dn_dt_bias[0])
    h = _dn_call(p_conv, p_gate, p_small, mr, x2, alog_row, dtb_row, dn_norm_g, w_dn_br[0].astype(BF16),
                 w_o[0].astype(BF16), batch, t_len, _pick_tile(t_len, 256))

    out = _ffn_call(h, g_ffn, w_up[0].astype(BF16), ffn_conv_w[0], ffn_conv_b, w_down[0].astype(BF16),
                    g_final.reshape(1, D_MODEL), batch, t_len, _pick_tile(t_len, 512))
    return out.reshape(batch, t_len, D_MODEL).astype(x.dtype)
```

```python
import functools

import numpy as np
import jax
import jax.numpy as jnp
from jax import lax
from jax.experimental import pallas as pl
from jax.experimental.pallas import tpu as pltpu

D_MODEL = 1024
RET_HEADS = 8
RET_DK = 64
RET_DV = 128
DN_HEADS = 8
DN_DK = 128
DN_DV = 128
CHUNK = 128
SHORT_CONV = 4
FFN_CONV = 3
D_FF = 2816
ROPE_BASE = 10000.0
EPS = 1e-6
GN_EPS = 1e-5

RET_QK = RET_HEADS * RET_DK
RET_V = RET_HEADS * RET_DV
DN_QK = DN_HEADS * DN_DK
DN_V = DN_HEADS * DN_DV

LANES = 128
SUBLANES = 8
COLBLK = 1024
SMALL_BETA0 = 0
SMALL_DECAY0 = 8

VMEM_LIMIT = 56 * 1024 * 1024

F32 = jnp.float32
BF16 = jnp.bfloat16


def _dot(a, b):
    return jnp.dot(a, b, preferred_element_type=F32)


def _dot_nt(a, b):
    return lax.dot_general(a, b, (((1,), (1,)), ((), ())), preferred_element_type=F32)


def _dot_tn(a, b):
    return lax.dot_general(a, b, (((0,), (0,)), ((), ())), preferred_element_type=F32)


def _split3(x):
    hi = x.astype(BF16)
    r1 = x - hi.astype(F32)
    mid = r1.astype(BF16)
    lo = (r1 - mid.astype(F32)).astype(BF16)
    return hi, mid, lo


def _dot_split_rhs(a, x):
    hi, mid, lo = _split3(x)
    return _dot(a, hi) + _dot(a, mid) + _dot(a, lo)


def _sigmoid(x):
    return 1.0 / (1.0 + jnp.exp(-x))


def _silu(x):
    return x * _sigmoid(x)


def _softplus(x):
    return jnp.maximum(x, 0.0) + jnp.log(1.0 + jnp.exp(-jnp.abs(x)))


PROJ_STEPS = 4
PLAIN_W = (2 * RET_QK + RET_V) // PROJ_STEPS
GATE_W = (RET_V + DN_V + 2 * D_MODEL) // PROJ_STEPS
CONV_W = (2 * DN_QK + DN_V) // PROJ_STEPS
GATE_RGATE, GATE_DZ, GATE_R, GATE_D = range(4)
CONV_GROUPS = CONV_W // DN_DK
PROJ_ROWS = 256


def _proj_kernel(tiles_per_seq, x_ref, g_ref, w1_ref, w2_ref, w3_ref, ws_ref, convw_ref, o1_ref, o2_ref, o3_ref,
                 ps_ref, u_ref, carry_ref, buf_ref):
    i = pl.program_id(0)
    s = pl.program_id(1)
    tm = x_ref.shape[0]

    @pl.when(s == 0)
    def _():
        x = x_ref[...]
        r = lax.rsqrt(jnp.mean(x * x, axis=-1, keepdims=True) + EPS)
        u = ((x * r) * g_ref[...]).astype(BF16)
        u_ref[...] = u
        ps_ref[...] = _dot(u, ws_ref[...])

    @pl.when(jnp.logical_and(i == 0, s == 0))
    def _():
        carry_ref[...] = jnp.zeros_like(carry_ref)

    prev = carry_ref[s]
    buf_ref[0:SUBLANES, :] = jnp.where(i % tiles_per_seq == 0, jnp.zeros_like(prev), prev)
    w = convw_ref[s]
    gate_is_silu = s <= GATE_DZ

    def products(rb):
        r0 = rb * PROJ_ROWS
        u = u_ref[r0:r0 + PROJ_ROWS, :]
        acc = _dot(u, w3_ref[...])
        buf_ref[SUBLANES + r0:SUBLANES + r0 + PROJ_ROWS, :] = acc
        return acc, _dot(u, w2_ref[...]), _dot(u, w1_ref[...])

    def elementwise(rb, acc, g, p):
        r0 = rb * PROJ_ROWS
        rows = slice(r0, r0 + PROJ_ROWS)
        y = acc * w[SHORT_CONV - 1:SHORT_CONV]
        for t in range(1, SHORT_CONV):
            y = y + buf_ref[SUBLANES + r0 - t:SUBLANES + r0 - t + PROJ_ROWS, :] * w[SHORT_CONV - 1 - t:SHORT_CONV - t]
        y = _silu(y)
        for l in range(CONV_GROUPS):
            head_group = s * CONV_GROUPS + l
            cols = slice(l * DN_DK, (l + 1) * DN_DK)
            yl = y[:, cols]
            r = lax.rsqrt(jnp.sum(yl * yl, axis=-1, keepdims=True) + EPS)
            r = r * jnp.where(head_group < DN_HEADS, DN_DK ** -0.5, 1.0)
            scale = jnp.where(head_group < 2 * DN_HEADS, r, 1.0)
            o3_ref[rows, cols] = (yl * scale).astype(BF16)
        o2_ref[rows, :] = (_sigmoid(g) * jnp.where(gate_is_silu, g, 1.0)).astype(BF16)
        o1_ref[rows, :] = p.astype(BF16)

    nrb = tm // PROJ_ROWS
    pending = products(0)
    for rb in range(1, nrb):
        nxt = products(rb)
        elementwise(rb - 1, *pending)
        pending = nxt
    elementwise(nrb - 1, *pending)
    carry_ref[s] = buf_ref[tm:tm + SUBLANES, :]


def _proj_call(x2, g_mix, w_plain, w_gate, w_conv, w_small, conv_w, t_len, tm):
    m = x2.shape[0]
    assert t_len % tm == 0
    wspec = lambda n: pl.BlockSpec((D_MODEL, n), lambda i, s: (0, s))
    ospec = lambda n: pl.BlockSpec((tm, n), lambda i, s: (i, s))
    return pl.pallas_call(
        functools.partial(_proj_kernel, t_len // tm),
        grid=(m // tm, PROJ_STEPS),
        in_specs=[
            pl.BlockSpec((tm, D_MODEL), lambda i, s: (i, 0)),
            pl.BlockSpec((1, D_MODEL), lambda i, s: (0, 0)),
            wspec(PLAIN_W), wspec(GATE_W), wspec(CONV_W),
            pl.BlockSpec((D_MODEL, LANES), lambda i, s: (0, 0)),
            pl.BlockSpec((PROJ_STEPS, SHORT_CONV, CONV_W), lambda i, s: (0, 0, 0)),
        ],
        out_specs=[
            ospec(PLAIN_W), ospec(GATE_W), ospec(CONV_W),
            pl.BlockSpec((tm, LANES), lambda i, s: (i, 0)),
        ],
        out_shape=[
            jax.ShapeDtypeStruct((m, PROJ_STEPS * PLAIN_W), BF16),
            jax.ShapeDtypeStruct((m, PROJ_STEPS * GATE_W), BF16),
            jax.ShapeDtypeStruct((m, PROJ_STEPS * CONV_W), BF16),
            jax.ShapeDtypeStruct((m, LANES), F32),
        ],
        scratch_shapes=[
            pltpu.VMEM((tm, D_MODEL), BF16),
            pltpu.VMEM((PROJ_STEPS, SUBLANES, CONV_W), F32),
            pltpu.VMEM((SUBLANES + tm, CONV_W), F32),
        ],
        compiler_params=pltpu.CompilerParams(
            dimension_semantics=("arbitrary", "arbitrary"), vmem_limit_bytes=VMEM_LIMIT),
        name="proj",
    )(x2, g_mix, w_plain, w_gate, w_conv, w_small, conv_w)


def _ret_kernel(decays, nchunk, qk_ref, v_ref, gate_ref, gr_ref, cos_ref, sin_ref, dmat_ref, xi_ref,
                zeta_ref, gn_ref, wbr_ref, mr_ref, s_ref, stage_ref):
    @pl.when(pl.program_id(1) == 0)
    def _():
        s_ref[...] = jnp.zeros_like(s_ref)

    lane = lax.broadcasted_iota(jnp.int32, (CHUNK, LANES), 1)
    first_half = (lane % RET_DK) < (RET_DK // 2)
    head_lo = lane < RET_DK

    def rot(x, cos_t, sin_t):
        partner = jnp.where(first_half, pltpu.roll(x, LANES - RET_DK // 2, 1), pltpu.roll(x, RET_DK // 2, 1))
        return x * cos_t + partner * sin_t

    qb, kb, qx, kz = {}, {}, {}, {}
    for c in range(nchunk):
        rows = slice(c * CHUNK, (c + 1) * CHUNK)
        cos_t = cos_ref[rows, :]
        sin_t = sin_ref[rows, :]
        for p in range(RET_HEADS // 2):
            q = rot(qk_ref[rows, p * LANES:(p + 1) * LANES].astype(F32), cos_t, sin_t)
            k = rot(qk_ref[rows, RET_QK + p * LANES:RET_QK + (p + 1) * LANES].astype(F32), cos_t, sin_t)
            k = k * (RET_DK ** -0.5)
            qb[c, p] = q.astype(BF16)
            kb[c, p] = k.astype(BF16)
            qx[c, p] = (q * xi_ref[p]).astype(BF16)
            kz[c, p] = (k * zeta_ref[p]).astype(BF16)

    items = [(c, h) for c in range(nchunk) for h in range(RET_HEADS)]

    def head_mask(h):
        return head_lo if h % 2 == 0 else jnp.logical_not(head_lo)

    def v_of(c, h):
        return v_ref[c * CHUNK:(c + 1) * CHUNK, h * RET_DV:(h + 1) * RET_DV]

    scores16 = {}
    for c, h in items:
        qm = jnp.where(head_mask(h), qb[c, h // 2], jnp.zeros((CHUNK, LANES), BF16))
        scores16[c, h] = (_dot_nt(qm, kb[c, h // 2]) * dmat_ref[h]).astype(BF16)
    kv = {(c, h): _dot_tn(kz[c, h // 2], v_of(c, h)) for c, h in items}

    for c in range(nchunk):
        rows = slice(c * CHUNK, (c + 1) * CHUNK)
        s = [s_ref[h] for h in range(RET_HEADS)]
        ro = []
        for h in range(RET_HEADS):
            qxm = jnp.where(head_mask(h), qx[c, h // 2], jnp.zeros((CHUNK, LANES), BF16))
            ro.append(_dot(jnp.concatenate([scores16[c, h], qxm], axis=1),
                           jnp.concatenate([v_of(c, h), s[h].astype(BF16)], axis=0)))
        for h in range(RET_HEADS):
            s_ref[h] = s[h] * decays[h] + kv[c, h]
        for h in range(RET_HEADS):
            cols = slice(h * RET_DV, (h + 1) * RET_DV)
            mu = jnp.mean(ro[h], axis=-1, keepdims=True)
            d = ro[h] - mu
            var = jnp.mean(d * d, axis=-1, keepdims=True)
            y = (d * lax.rsqrt(var + GN_EPS)) * gn_ref[:, cols] * gate_ref[rows, cols].astype(F32)
            stage_ref[rows, cols] = y.astype(BF16)

    y_ret = _dot(stage_ref[...], wbr_ref[...])
    mr_ref[...] = (gr_ref[...].astype(F32) * y_ret).astype(BF16)


def _ret_tables():
    h = np.arange(RET_HEADS, dtype=np.float64)
    gamma = 1.0 - 2.0 ** (-5.0 - h)
    log_g = np.log(gamma)
    idx = np.arange(CHUNK, dtype=np.float64)
    diff = idx[:, None] - idx[None, :]
    dmat = np.where(diff >= 0, np.exp(log_g[:, None, None] * np.maximum(diff, 0.0)[None]), 0.0)
    zeta = np.exp(log_g[:, None] * (CHUNK - 1.0 - idx)[None, :])
    xi = np.exp(log_g[:, None] * (idx + 1.0)[None, :])
    decays = tuple(float(v) for v in np.exp(log_g * CHUNK))

    def pair_table(t):
        t = t.reshape(RET_HEADS // 2, 2, CHUNK)
        return np.repeat(np.transpose(t, (0, 2, 1)), RET_DK, axis=2)

    return (jnp.asarray(dmat, F32), jnp.asarray(pair_table(xi), F32), jnp.asarray(pair_table(zeta), F32), decays)


def _rope_tables(t_len):
    half = RET_DK // 2
    inv = ROPE_BASE ** (-np.arange(0, RET_DK, 2, dtype=np.float64) / RET_DK)
    ang = np.arange(t_len, dtype=np.float64)[:, None] * inv[None, :]
    cos, sin = np.cos(ang), np.sin(ang)
    cos_t = np.concatenate([cos, cos], axis=1)
    sin_t = np.concatenate([-sin, sin], axis=1)
    assert cos_t.shape[1] == 2 * half
    reps = LANES // RET_DK
    return jnp.asarray(np.tile(cos_t, (1, reps)), F32), jnp.asarray(np.tile(sin_t, (1, reps)), F32)


def _ret_call(p_plain, p_gate, cos_t, sin_t, ret_norm_g, w_ret_br, batch, t_len, tt):
    m = batch * t_len
    nt = t_len // tt
    dmat, xi, zeta, decays = _ret_tables()
    row = lambda cb: pl.BlockSpec((tt, COLBLK), lambda b, t: (b * nt + t, cb))
    const2 = lambda shape: pl.BlockSpec(shape, lambda b, t: (0, 0))
    const3 = lambda shape: pl.BlockSpec(shape, lambda b, t: (0, 0, 0))
    return pl.pallas_call(
        functools.partial(_ret_kernel, decays, tt // CHUNK),
        grid=(batch, nt),
        in_specs=[
            row(0), row(1), row(GATE_RGATE), row(GATE_R),
            pl.BlockSpec((tt, LANES), lambda b, t: (t, 0)),
            pl.BlockSpec((tt, LANES), lambda b, t: (t, 0)),
            const3((RET_HEADS, CHUNK, CHUNK)),
            const3((RET_HEADS // 2, CHUNK, LANES)),
            const3((RET_HEADS // 2, CHUNK, LANES)),
            const2((1, RET_V)),
            const2((RET_V, D_MODEL)),
        ],
        out_specs=pl.BlockSpec((tt, D_MODEL), lambda b, t: (b * nt + t, 0)),
        out_shape=jax.ShapeDtypeStruct((m, D_MODEL), BF16),
        scratch_shapes=[
            pltpu.VMEM((RET_HEADS, LANES, RET_DV), F32),
            pltpu.VMEM((tt, RET_V), BF16),
        ],
        compiler_params=pltpu.CompilerParams(
            dimension_semantics=("arbitrary", "arbitrary"), vmem_limit_bytes=VMEM_LIMIT),
        name="retention",
    )(p_plain, p_plain, p_gate, p_gate, cos_t, sin_t, dmat, xi, zeta, ret_norm_g, w_ret_br)


INV_FULL_LEVELS = 3


def _odd_block_rows(x, b):
    return jnp.concatenate([x[(2 * i + 1) * b:(2 * i + 2) * b] for i in range(CHUNK // (2 * b))], axis=0)


def _with_odd_block_rows(x, odd, b):
    parts = []
    for i in range(CHUNK // (2 * b)):
        parts += [x[2 * i * b:(2 * i + 1) * b], odd[i * b:(i + 1) * b]]
    return jnp.concatenate(parts, axis=0)


def _inv_unit_lower_many(a_list, qmask_ref, qmask_odd_ref):
    row = lax.broadcasted_iota(jnp.int32, (CHUNK, CHUNK), 0)
    col = lax.broadcasted_iota(jnp.int32, (CHUNK, CHUNK), 1)
    eye = jnp.where(row == col, 1.0, 0.0).astype(F32)
    a16 = [a.astype(BF16) for a in a_list]
    d = [eye - a * qmask_ref[0] for a in a_list]
    level = 1
    b = 2
    while b < CHUNK:
        d16 = [x.astype(BF16) for x in d]
        if level < INV_FULL_LEVELS:
            mask = qmask_ref[level]
            t16 = [_dot(x, a).astype(BF16) for x, a in zip(d16, a16)]
            d = [x - _dot(t, y) * mask for x, t, y in zip(d, t16, d16)]
        else:
            mask = qmask_odd_ref[level - INV_FULL_LEVELS]
            t16 = [_dot(_odd_block_rows(x, b), a).astype(BF16) for x, a in zip(d16, a16)]
            d = [_with_odd_block_rows(x, _odd_block_rows(x, b) - _dot(t, y) * mask, b)
                 for x, t, y in zip(d, t16, d16)]
        level += 1
        b *= 2
    return d


def _dn_kernel(nchunk, q_ref, k_ref, v_ref, z_ref, gd_ref, mr_ref, x_ref, ps_ref, alog_ref, dtb_ref,
               tri_ref, qmask_ref, qmask_odd_ref, ng_ref, wbr_ref, wo_ref, h_ref, s_ref, stage_ref):
    @pl.when(pl.program_id(1) == 0)
    def _():
        s_ref[...] = jnp.zeros_like(s_ref)

    ps = ps_ref[...]
    beta_all = _sigmoid(ps)
    g_all = -jnp.exp(alog_ref[...]) * _softplus(ps + dtb_ref[...])

    row = lax.broadcasted_iota(jnp.int32, (CHUNK, CHUNK), 0)
    col = lax.broadcasted_iota(jnp.int32, (CHUNK, CHUNK), 1)
    causal = row >= col
    strict = row > col
    tri = tri_ref[...]

    def lanes(x, j):
        return jnp.broadcast_to(x[:, j:j + 1], (CHUNK, LANES))

    g_cum, g_cum_t, exp_g_all, exp_rest_all = [], [], [], []
    for c in range(nchunk):
        g = _dot_split_rhs(tri, g_all[c * CHUNK:(c + 1) * CHUNK])
        g_cum.append(g)
        g_cum_t.append(g.T)
        exp_g_all.append(jnp.exp(g))
        exp_rest_all.append(jnp.exp(g[CHUNK - 1:CHUNK, :] - g))

    def decay_of(c, h, mask):
        gc = lanes(g_cum[c], SMALL_DECAY0 + h)
        gr = jnp.broadcast_to(g_cum_t[c][SMALL_DECAY0 + h:SMALL_DECAY0 + h + 1, :], (CHUNK, CHUNK))
        return jnp.where(mask, jnp.exp(gc - gr), 0.0)

    def block(ref, c, h):
        return ref[c * CHUNK:(c + 1) * CHUNK, h * DN_DV:(h + 1) * DN_DV]

    items = [(c, h) for c in range(nchunk) for h in range(DN_HEADS)]

    a_list, rhs16 = [], []
    for c, h in items:
        k16 = block(k_ref, c, h)
        beta = lanes(beta_all[c * CHUNK:(c + 1) * CHUNK], SMALL_BETA0 + h)
        k_beta = k16.astype(F32) * beta
        a_list.append(_dot_nt(k_beta.astype(BF16), k16) * decay_of(c, h, strict))
        exp_g = lanes(exp_g_all[c], SMALL_DECAY0 + h)
        rhs16.append(jnp.concatenate([block(v_ref, c, h).astype(F32) * beta, k_beta * exp_g], axis=1).astype(BF16))

    minv = _inv_unit_lower_many(a_list, qmask_ref, qmask_odd_ref)
    sol = [_dot(m.astype(BF16), r) for m, r in zip(minv, rhs16)]

    for c in range(nchunk):
        qa16, kdec16, sdec = [], [], []
        for h in range(DN_HEADS):
            q16 = block(q_ref, c, h)
            k16 = block(k_ref, c, h)
            exp_g = lanes(exp_g_all[c], SMALL_DECAY0 + h)
            attn = _dot_nt(q16, k16) * decay_of(c, h, causal)
            qa16.append(jnp.concatenate([(q16.astype(F32) * exp_g).astype(BF16), attn.astype(BF16)], axis=1))
            kdec16.append((k16.astype(F32) * lanes(exp_rest_all[c], SMALL_DECAY0 + h)).astype(BF16))
            sdec.append(exp_g[CHUNK - 1:CHUNK, :])
        idx = [c * DN_HEADS + h for h in range(DN_HEADS)]
        s = [s_ref[h] for h in range(DN_HEADS)]
        s16 = [x.astype(BF16) for x in s]
        ws = [_dot(sol[i][:, DN_DV:].astype(BF16), s16[h]) for h, i in enumerate(idx)]
        vn16 = [(sol[i][:, :DN_DV] - ws[h]).astype(BF16) for h, i in enumerate(idx)]
        o = [_dot(qa16[h], jnp.concatenate([s16[h], vn16[h]], axis=0)) for h in range(DN_HEADS)]
        for h in range(DN_HEADS):
            s_ref[h] = s[h] * sdec[h] + _dot_tn(kdec16[h], vn16[h])
        for h in range(DN_HEADS):
            on = o[h] * lax.rsqrt(jnp.mean(o[h] * o[h], axis=-1, keepdims=True) + EPS) * ng_ref[...]
            stage_ref[c * CHUNK:(c + 1) * CHUNK, h * DN_DV:(h + 1) * DN_DV] = (
                on * block(z_ref, c, h).astype(F32)).astype(BF16)

    y_dn = _dot(stage_ref[...], wbr_ref[...])
    merged = mr_ref[...].astype(F32) + gd_ref[...].astype(F32) * y_dn
    h_ref[...] = x_ref[...] + _dot(merged.astype(BF16), wo_ref[...])


def _dn_tables():
    idx = np.arange(CHUNK)
    r, c = idx[:, None], idx[None, :]
    qmasks, qmasks_odd = [], []
    b = 1
    while b < CHUNK:
        m = ((r // (2 * b)) == (c // (2 * b))) & ((r // b) % 2 == 1) & ((c // b) % 2 == 0)
        if len(qmasks) < INV_FULL_LEVELS:
            qmasks.append(m)
        else:
            qmasks_odd.append(m[(idx // b) % 2 == 1])
        b *= 2
    tri = (r >= c).astype(np.float32)
    return (jnp.asarray(tri, BF16), jnp.asarray(np.stack(qmasks), F32), jnp.asarray(np.stack(qmasks_odd), F32))


def _dn_call(p_conv, p_gate, p_small, mr, x2, alog_row, dtb_row, dn_norm_g, w_dn_br, w_o, batch, t_len, tt):
    m = batch * t_len
    nt = t_len // tt
    tri, qmask, qmask_odd = _dn_tables()
    row = lambda cb: pl.BlockSpec((tt, COLBLK), lambda b, t: (b * nt + t, cb))
    tok = lambda n: pl.BlockSpec((tt, n), lambda b, t: (b * nt + t, 0))
    const2 = lambda shape: pl.BlockSpec(shape, lambda b, t: (0, 0))
    return pl.pallas_call(
        functools.partial(_dn_kernel, tt // CHUNK),
        grid=(batch, nt),
        in_specs=[
            row(0), row(1), row(2), row(GATE_DZ), row(GATE_D),
            tok(D_MODEL), tok(D_MODEL), tok(LANES),
            const2((1, LANES)), const2((1, LANES)),
            const2((CHUNK, CHUNK)),
            pl.BlockSpec(qmask.shape, lambda b, t: (0, 0, 0)),
            pl.BlockSpec(qmask_odd.shape, lambda b, t: (0, 0, 0)),
            const2((1, DN_DV)),
            const2((DN_V, D_MODEL)), const2((D_MODEL, D_MODEL)),
        ],
        out_specs=tok(D_MODEL),
        out_shape=jax.ShapeDtypeStruct((m, D_MODEL), F32),
        scratch_shapes=[
            pltpu.VMEM((DN_HEADS, DN_DK, DN_DV), F32),
            pltpu.VMEM((tt, DN_V), BF16),
        ],
        compiler_params=pltpu.CompilerParams(
            dimension_semantics=("arbitrary", "arbitrary"), vmem_limit_bytes=VMEM_LIMIT),
        name="deltanet",
    )(p_conv, p_conv, p_conv, p_gate, p_gate, mr, x2, p_small, alog_row, dtb_row, tri, qmask, qmask_odd,
      dn_norm_g, w_dn_br, w_o)


FFN_COLS = 256
FFN_STEPS = D_FF // FFN_COLS


def _ffn_kernel(h_ref, g_ref, wup_ref, cw_ref, cb_ref, wdn_ref, gf_ref, o_ref, carry_ref, up_ref, act_ref):
    @pl.when(pl.program_id(1) == 0)
    def _():
        carry_ref[...] = jnp.zeros_like(carry_ref)

    tm = h_ref.shape[0]
    h = h_ref[...]
    u = ((h * lax.rsqrt(jnp.mean(h * h, axis=-1, keepdims=True) + EPS)) * g_ref[...]).astype(BF16)

    def conv_branch(c0, buf):
        cols = slice(c0, c0 + FFN_COLS)
        up = _dot(u, wup_ref[:, cols])
        buf[0:SUBLANES, :] = carry_ref[:, cols]
        buf[SUBLANES:SUBLANES + tm, :] = up
        carry_ref[:, cols] = up[tm - SUBLANES:]
        w = cw_ref[:, cols]
        y = up * w[FFN_CONV - 1:FFN_CONV] + cb_ref[:, cols]
        for s in range(1, FFN_CONV):
            y = y + buf[SUBLANES - s:SUBLANES - s + tm, :] * w[FFN_CONV - 1 - s:FFN_CONV - s]
        return y

    for j in range(FFN_STEPS):
        a = conv_branch(j * FFN_COLS, up_ref.at[j % 2, 0])
        b = conv_branch(D_FF + j * FFN_COLS, up_ref.at[j % 2, 1])
        act_ref[:, j * FFN_COLS:(j + 1) * FFN_COLS] = (_silu(a) * b).astype(BF16)

    acc = h + _dot(act_ref[...], wdn_ref[...])
    o_ref[...] = (acc * lax.rsqrt(jnp.mean(acc * acc, axis=-1, keepdims=True) + EPS)) * gf_ref[...]


def _ffn_call(h2, g_ffn, w_up, conv_w, conv_b, w_down, g_final, batch, t_len, tm):
    m = batch * t_len
    nt = t_len // tm
    tok = pl.BlockSpec((tm, D_MODEL), lambda b, t: (b * nt + t, 0))
    const2 = lambda shape: pl.BlockSpec(shape, lambda b, t: (0, 0))
    resident = lambda shape: pl.BlockSpec(shape, lambda b, t: (0, 0), pipeline_mode=pl.Buffered(1))
    return pl.pallas_call(
        _ffn_kernel,
        grid=(batch, nt),
        in_specs=[
            tok, const2((1, D_MODEL)),
            resident((D_MODEL, 2 * D_FF)),
            const2((FFN_CONV, 2 * D_FF)), const2((1, 2 * D_FF)),
            resident((D_FF, D_MODEL)),
            const2((1, D_MODEL)),
        ],
        out_specs=tok,
        out_shape=jax.ShapeDtypeStruct((m, D_MODEL), F32),
        scratch_shapes=[
            pltpu.VMEM((SUBLANES, 2 * D_FF), F32),
            pltpu.VMEM((2, 2, SUBLANES + tm, FFN_COLS), F32),
            pltpu.VMEM((tm, D_FF), BF16),
        ],
        compiler_params=pltpu.CompilerParams(
            dimension_semantics=("arbitrary", "arbitrary"), vmem_limit_bytes=VMEM_LIMIT),
        name="convffn",
    )(h2, g_ffn, w_up, conv_w, conv_b, w_down, g_final)


def _pick_tile(t_len, want):
    tile = min(want, t_len)
    assert t_len % tile == 0 and tile % CHUNK == 0
    return tile


PROJ_TILE = 1024
RET_TILE = 256
DN_TILE = 512
FFN_TILE = 512


def kernel(x, g_mix, w_in, ret_norm_g, dn_conv_w, dn_a_log, dn_dt_bias, dn_norm_g, w_ret_br, w_dn_br, w_o, g_ffn,
           w_up, ffn_conv_w, ffn_conv_b, w_down, g_final):
    batch, t_len, d_model = x.shape
    assert d_model == D_MODEL and g_mix.shape[0] == 1 and t_len % CHUNK == 0
    m = batch * t_len
    x2 = x.astype(F32).reshape(m, D_MODEL)

    c_plain = 2 * RET_QK + RET_V
    c_rgate = c_plain + RET_V
    c_conv = c_rgate + 2 * DN_QK + DN_V
    c_dz = c_conv + DN_V
    c_small = c_dz + 2 * DN_HEADS
    w = w_in[0]
    w_plain = w[:, :c_plain].astype(BF16)
    w_gate = jnp.concatenate([w[:, c_plain:c_rgate], w[:, c_conv:c_dz], w[:, c_small:]], axis=1).astype(BF16)
    w_conv = w[:, c_rgate:c_conv].astype(BF16)
    w_small = jnp.pad(w[:, c_dz:c_small], ((0, 0), (0, LANES - 2 * DN_HEADS))).astype(BF16)

    conv_w = jnp.transpose(dn_conv_w[0].reshape(SHORT_CONV, PROJ_STEPS, CONV_W), (1, 0, 2))
    p_plain, p_gate, p_conv, p_small = _proj_call(x2, g_mix, w_plain, w_gate, w_conv, w_small, conv_w, t_len,
                                                  _pick_tile(t_len, PROJ_TILE))

    cos_t, sin_t = _rope_tables(t_len)
    mr = _ret_call(p_plain, p_gate, cos_t, sin_t, ret_norm_g, w_ret_br[0].astype(BF16), batch, t_len,
                   _pick_tile(t_len, RET_TILE))

    alog_row = jnp.zeros((1, LANES), F32).at[0, SMALL_DECAY0:SMALL_DECAY0 + DN_HEADS].set(dn_a_log[0])
    dtb_row = jnp.zeros((1, LANES), F32).at[0, SMALL_DECAY0:SMALL_DECAY0 + DN_HEADS].set(dn_dt_bias[0])
    h = _dn_call(p_conv, p_gate, p_small, mr, x2, alog_row, dtb_row, dn_norm_g, w_dn_br[0].astype(BF16),
                 w_o[0].astype(BF16), batch, t_len, _pick_tile(t_len, DN_TILE))

    out = _ffn_call(h, g_ffn, w_up[0].astype(BF16), ffn_conv_w[0], ffn_conv_b, w_down[0].astype(BF16),
                    g_final.reshape(1, D_MODEL), batch, t_len, _pick_tile(t_len, FFN_TILE))
    return out.reshape(batch, t_len, D_MODEL).astype(x.dtype)
```

```python
import functools

import numpy as np
import jax
import jax.numpy as jnp
from jax import lax
from jax.experimental import pallas as pl
from jax.experimental.pallas import tpu as pltpu

D_MODEL = 1024
RET_HEADS = 8
RET_DK = 64
RET_DV = 128
DN_HEADS = 8
DN_DK = 128
DN_DV = 128
CHUNK = 128
SHORT_CONV = 4
FFN_CONV = 3
D_FF = 2816
ROPE_BASE = 10000.0
EPS = 1e-6
GN_EPS = 1e-5

RET_QK = RET_HEADS * RET_DK
RET_V = RET_HEADS * RET_DV
DN_QK = DN_HEADS * DN_DK
DN_V = DN_HEADS * DN_DV

LANES = 128
SUBLANES = 8
COLBLK = 1024
SMALL_BETA0 = 0
SMALL_DECAY0 = 8

VMEM_LIMIT = 56 * 1024 * 1024

F32 = jnp.float32
BF16 = jnp.bfloat16


def _dot(a, b):
    return jnp.dot(a, b, preferred_element_type=F32)


def _dot_nt(a, b):
    return lax.dot_general(a, b, (((1,), (1,)), ((), ())), preferred_element_type=F32)


def _dot_tn(a, b):
    return lax.dot_general(a, b, (((0,), (0,)), ((), ())), preferred_element_type=F32)


def _split3(x):
    hi = x.astype(BF16)
    r1 = x - hi.astype(F32)
    mid = r1.astype(BF16)
    lo = (r1 - mid.astype(F32)).astype(BF16)
    return hi, mid, lo


def _dot_split_rhs(a, x):
    hi, mid, lo = _split3(x)
    return _dot(a, hi) + _dot(a, mid) + _dot(a, lo)


def _sigmoid(x):
    return 0.5 + 0.5 * jnp.tanh(0.5 * x)


def _silu(x):
    h = 0.5 * x
    return h + h * jnp.tanh(h)


def _softplus(x):
    return jnp.maximum(x, 0.0) + jnp.log(1.0 + jnp.exp(-jnp.abs(x)))


PROJ_STEPS = 4
PLAIN_W = (2 * RET_QK + RET_V) // PROJ_STEPS
SILU_W = (RET_V + DN_V) // PROJ_STEPS
SIGM_W = 2 * D_MODEL // PROJ_STEPS
CONV_W = (2 * DN_QK + DN_V) // PROJ_STEPS
COL_RGATE = 2 * RET_QK + RET_V
COL_CONV = COL_RGATE + RET_V
COL_DZ = COL_CONV + 2 * DN_QK + DN_V
COL_SMALL = COL_DZ + DN_V
COL_SIGM = COL_SMALL + 2 * DN_HEADS
CONV_GROUPS = CONV_W // DN_DK
PROJ_ROWS = 256


def _proj_kernel(tiles_per_seq, x_ref, g_ref, w1_ref, w2a_ref, w2b_ref, w3_ref, ws_ref, convw_ref,
                 o1_ref, o2a_ref, o2b_ref, o3_ref, ps_ref, u_ref, carry_ref, buf_ref):
    i = pl.program_id(0)
    s = pl.program_id(1)
    tm = x_ref.shape[0]

    @pl.when(s == 0)
    def _():
        x = x_ref[...]
        r = lax.rsqrt(jnp.mean(x * x, axis=-1, keepdims=True) + EPS)
        u = ((x * r) * g_ref[...]).astype(BF16)
        u_ref[...] = u
        ps_ref[...] = _dot(u, ws_ref[...])

    @pl.when(jnp.logical_and(i == 0, s == 0))
    def _():
        carry_ref[...] = jnp.zeros_like(carry_ref)

    prev = carry_ref[s]
    buf_ref[0:SUBLANES, :] = jnp.where(i % tiles_per_seq == 0, jnp.zeros_like(prev), prev)
    w = convw_ref[s]

    def products(rb):
        r0 = rb * PROJ_ROWS
        u = u_ref[r0:r0 + PROJ_ROWS, :]
        acc = _dot(u, w3_ref[...])
        buf_ref[SUBLANES + r0:SUBLANES + r0 + PROJ_ROWS, :] = acc
        return acc, _dot(u, w2a_ref[...]), _dot(u, w2b_ref[...]), _dot(u, w1_ref[...])

    def elementwise(rb, acc, ga, gb, p):
        r0 = rb * PROJ_ROWS
        rows = slice(r0, r0 + PROJ_ROWS)
        y = acc * w[SHORT_CONV - 1:SHORT_CONV]
        for t in range(1, SHORT_CONV):
            y = y + buf_ref[SUBLANES + r0 - t:SUBLANES + r0 - t + PROJ_ROWS, :] * w[SHORT_CONV - 1 - t:SHORT_CONV - t]
        y = _silu(y)
        for l in range(CONV_GROUPS):
            head_group = s * CONV_GROUPS + l
            cols = slice(l * DN_DK, (l + 1) * DN_DK)
            yl = y[:, cols]
            r = lax.rsqrt(jnp.sum(yl * yl, axis=-1, keepdims=True) + EPS)
            r = r * jnp.where(head_group < DN_HEADS, DN_DK ** -0.5, 1.0)
            scale = jnp.where(head_group < 2 * DN_HEADS, r, 1.0)
            o3_ref[rows, cols] = (yl * scale).astype(BF16)
        o2a_ref[rows, :] = _silu(ga).astype(BF16)
        o2b_ref[rows, :] = _sigmoid(gb).astype(BF16)
        o1_ref[rows, :] = p.astype(BF16)

    nrb = tm // PROJ_ROWS
    pending = products(0)
    for rb in range(1, nrb):
        nxt = products(rb)
        elementwise(rb - 1, *pending)
        pending = nxt
    elementwise(nrb - 1, *pending)
    carry_ref[s] = buf_ref[tm:tm + SUBLANES, :]


def _cast_kernel(w_ref, o_ref):
    o_ref[...] = w_ref[...].astype(BF16)


def _cast_call(w_in, ncols):
    assert ncols % COLBLK == 0
    return pl.pallas_call(
        _cast_kernel,
        grid=(ncols // COLBLK,),
        in_specs=[pl.BlockSpec((None, D_MODEL, COLBLK), lambda j: (0, 0, j))],
        out_specs=pl.BlockSpec((D_MODEL, COLBLK), lambda j: (0, j)),
        out_shape=jax.ShapeDtypeStruct((D_MODEL, ncols), BF16),
        compiler_params=pltpu.CompilerParams(dimension_semantics=("arbitrary",), vmem_limit_bytes=VMEM_LIMIT),
        name="cast_w_in",
    )(w_in)


def _proj_call(x2, g_mix, w_head, w_sigm, w_small, conv_w, t_len, tm):
    m = x2.shape[0]
    assert t_len % tm == 0 and w_head.shape[1] == COL_SMALL
    half = PROJ_STEPS // 2
    ospec = lambda n: pl.BlockSpec((tm, n), lambda i, s: (i, s))
    return pl.pallas_call(
        functools.partial(_proj_kernel, t_len // tm),
        grid=(m // tm, PROJ_STEPS),
        in_specs=[
            pl.BlockSpec((tm, D_MODEL), lambda i, s: (i, 0)),
            pl.BlockSpec((1, D_MODEL), lambda i, s: (0, 0)),
            pl.BlockSpec((D_MODEL, PLAIN_W), lambda i, s: (0, s)),
            pl.BlockSpec((D_MODEL, SILU_W),
                         lambda i, s: (0, jnp.where(s < half, COL_RGATE // SILU_W + s, COL_DZ // SILU_W + s - half))),
            pl.BlockSpec((D_MODEL, SIGM_W), lambda i, s: (0, s)),
            pl.BlockSpec((D_MODEL, CONV_W), lambda i, s: (0, COL_CONV // CONV_W + s)),
            pl.BlockSpec((D_MODEL, LANES), lambda i, s: (0, 0)),
            pl.BlockSpec((PROJ_STEPS, SHORT_CONV, CONV_W), lambda i, s: (0, 0, 0)),
        ],
        out_specs=[
            ospec(PLAIN_W), ospec(SILU_W), ospec(SIGM_W), ospec(CONV_W),
            pl.BlockSpec((tm, LANES), lambda i, s: (i, 0)),
        ],
        out_shape=[
            jax.ShapeDtypeStruct((m, PROJ_STEPS * PLAIN_W), BF16),
            jax.ShapeDtypeStruct((m, PROJ_STEPS * SILU_W), BF16),
            jax.ShapeDtypeStruct((m, PROJ_STEPS * SIGM_W), BF16),
            jax.ShapeDtypeStruct((m, PROJ_STEPS * CONV_W), BF16),
            jax.ShapeDtypeStruct((m, LANES), F32),
        ],
        scratch_shapes=[
            pltpu.VMEM((tm, D_MODEL), BF16),
            pltpu.VMEM((PROJ_STEPS, SUBLANES, CONV_W), F32),
            pltpu.VMEM((SUBLANES + tm, CONV_W), F32),
        ],
        compiler_params=pltpu.CompilerParams(
            dimension_semantics=("arbitrary", "arbitrary"), vmem_limit_bytes=VMEM_LIMIT),
        name="proj",
    )(x2, g_mix, w_head, w_head, w_sigm, w_head, w_small, conv_w)


def _ret_kernel(decays, nchunk, qk_ref, v_ref, gate_ref, gr_ref, cos_ref, sin_ref, dmat_ref, xi_ref,
                zeta_ref, gn_ref, wbr_ref, mr_ref, s_ref, stage_ref):
    @pl.when(pl.program_id(1) == 0)
    def _():
        s_ref[...] = jnp.zeros_like(s_ref)

    lane = lax.broadcasted_iota(jnp.int32, (CHUNK, LANES), 1)
    first_half = (lane % RET_DK) < (RET_DK // 2)
    head_lo = lane < RET_DK

    def rot(x, cos_t, sin_t):
        partner = jnp.where(first_half, pltpu.roll(x, LANES - RET_DK // 2, 1), pltpu.roll(x, RET_DK // 2, 1))
        return x * cos_t + partner * sin_t

    qb, kb, qx, kz = {}, {}, {}, {}
    for c in range(nchunk):
        rows = slice(c * CHUNK, (c + 1) * CHUNK)
        cos_t = cos_ref[rows, :]
        sin_t = sin_ref[rows, :]
        for p in range(RET_HEADS // 2):
            q = rot(qk_ref[rows, p * LANES:(p + 1) * LANES].astype(F32), cos_t, sin_t)
            k = rot(qk_ref[rows, RET_QK + p * LANES:RET_QK + (p + 1) * LANES].astype(F32), cos_t, sin_t)
            k = k * (RET_DK ** -0.5)
            qb[c, p] = q.astype(BF16)
            kb[c, p] = k.astype(BF16)
            qx[c, p] = (q * xi_ref[p]).astype(BF16)
            kz[c, p] = (k * zeta_ref[p]).astype(BF16)

    items = [(c, h) for c in range(nchunk) for h in range(RET_HEADS)]

    def head_mask(h):
        return head_lo if h % 2 == 0 else jnp.logical_not(head_lo)

    def v_of(c, h):
        return v_ref[c * CHUNK:(c + 1) * CHUNK, h * RET_DV:(h + 1) * RET_DV]

    scores16 = {}
    for c, h in items:
        qm = jnp.where(head_mask(h), qb[c, h // 2], jnp.zeros((CHUNK, LANES), BF16))
        scores16[c, h] = (_dot_nt(qm, kb[c, h // 2]) * dmat_ref[h]).astype(BF16)
    kv = {(c, h): _dot_tn(kz[c, h // 2], v_of(c, h)) for c, h in items}

    for c in range(nchunk):
        rows = slice(c * CHUNK, (c + 1) * CHUNK)
        s = [s_ref[h] for h in range(RET_HEADS)]
        ro = []
        for h in range(RET_HEADS):
            qxm = jnp.where(head_mask(h), qx[c, h // 2], jnp.zeros((CHUNK, LANES), BF16))
            ro.append(_dot(jnp.concatenate([scores16[c, h], qxm], axis=1),
                           jnp.concatenate([v_of(c, h), s[h].astype(BF16)], axis=0)))
        for h in range(RET_HEADS):
            s_ref[h] = s[h] * decays[h] + kv[c, h]
        for h in range(RET_HEADS):
            cols = slice(h * RET_DV, (h + 1) * RET_DV)
            mu = jnp.mean(ro[h], axis=-1, keepdims=True)
            d = ro[h] - mu
            var = jnp.mean(d * d, axis=-1, keepdims=True)
            y = (d * lax.rsqrt(var + GN_EPS)) * gn_ref[:, cols] * gate_ref[rows, cols].astype(F32)
            stage_ref[rows, cols] = y.astype(BF16)

    y_ret = _dot(stage_ref[...], wbr_ref[...])
    mr_ref[...] = (gr_ref[...].astype(F32) * y_ret).astype(BF16)


def _ret_tables():
    h = np.arange(RET_HEADS, dtype=np.float64)
    gamma = 1.0 - 2.0 ** (-5.0 - h)
    log_g = np.log(gamma)
    idx = np.arange(CHUNK, dtype=np.float64)
    diff = idx[:, None] - idx[None, :]
    dmat = np.where(diff >= 0, np.exp(log_g[:, None, None] * np.maximum(diff, 0.0)[None]), 0.0)
    zeta = np.exp(log_g[:, None] * (CHUNK - 1.0 - idx)[None, :])
    xi = np.exp(log_g[:, None] * (idx + 1.0)[None, :])
    decays = tuple(float(v) for v in np.exp(log_g * CHUNK))

    def pair_table(t):
        t = t.reshape(RET_HEADS // 2, 2, CHUNK)
        return np.repeat(np.transpose(t, (0, 2, 1)), RET_DK, axis=2)

    return (jnp.asarray(dmat, F32), jnp.asarray(pair_table(xi), F32), jnp.asarray(pair_table(zeta), F32), decays)


def _rope_tables(t_len):
    half = RET_DK // 2
    inv = ROPE_BASE ** (-np.arange(0, RET_DK, 2, dtype=np.float64) / RET_DK)
    ang = np.arange(t_len, dtype=np.float64)[:, None] * inv[None, :]
    cos, sin = np.cos(ang), np.sin(ang)
    cos_t = np.concatenate([cos, cos], axis=1)
    sin_t = np.concatenate([-sin, sin], axis=1)
    assert cos_t.shape[1] == 2 * half
    reps = LANES // RET_DK
    return jnp.asarray(np.tile(cos_t, (1, reps)), F32), jnp.asarray(np.tile(sin_t, (1, reps)), F32)


def _ret_call(p_plain, p_silu, p_sigm, cos_t, sin_t, ret_norm_g, w_ret_br, batch, t_len, tt):
    m = batch * t_len
    nt = t_len // tt
    dmat, xi, zeta, decays = _ret_tables()
    row = lambda cb: pl.BlockSpec((tt, COLBLK), lambda b, t: (b * nt + t, cb))
    const2 = lambda shape: pl.BlockSpec(shape, lambda b, t: (0, 0))
    const3 = lambda shape: pl.BlockSpec(shape, lambda b, t: (0, 0, 0))
    return pl.pallas_call(
        functools.partial(_ret_kernel, decays, tt // CHUNK),
        grid=(batch, nt),
        in_specs=[
            row(0), row(1), row(0), row(0),
            pl.BlockSpec((tt, LANES), lambda b, t: (t, 0)),
            pl.BlockSpec((tt, LANES), lambda b, t: (t, 0)),
            const3((RET_HEADS, CHUNK, CHUNK)),
            const3((RET_HEADS // 2, CHUNK, LANES)),
            const3((RET_HEADS // 2, CHUNK, LANES)),
            const2((1, RET_V)),
            const2((RET_V, D_MODEL)),
        ],
        out_specs=pl.BlockSpec((tt, D_MODEL), lambda b, t: (b * nt + t, 0)),
        out_shape=jax.ShapeDtypeStruct((m, D_MODEL), BF16),
        scratch_shapes=[
            pltpu.VMEM((RET_HEADS, LANES, RET_DV), F32),
            pltpu.VMEM((tt, RET_V), BF16),
        ],
        compiler_params=pltpu.CompilerParams(
            dimension_semantics=("arbitrary", "arbitrary"), vmem_limit_bytes=VMEM_LIMIT),
        name="retention",
    )(p_plain, p_plain, p_silu, p_sigm, cos_t, sin_t, dmat, xi, zeta, ret_norm_g, w_ret_br)


INV_FULL_LEVELS = 3


def _odd_block_rows(x, b):
    return jnp.concatenate([x[(2 * i + 1) * b:(2 * i + 2) * b] for i in range(CHUNK // (2 * b))], axis=0)


def _with_odd_block_rows(x, odd, b):
    parts = []
    for i in range(CHUNK // (2 * b)):
        parts += [x[2 * i * b:(2 * i + 1) * b], odd[i * b:(i + 1) * b]]
    return jnp.concatenate(parts, axis=0)


def _inv_unit_lower_many(a_list, qmask_ref, qmask_odd_ref):
    row = lax.broadcasted_iota(jnp.int32, (CHUNK, CHUNK), 0)
    col = lax.broadcasted_iota(jnp.int32, (CHUNK, CHUNK), 1)
    eye = jnp.where(row == col, 1.0, 0.0).astype(F32)
    a16 = [a.astype(BF16) for a in a_list]
    d = [eye - a * qmask_ref[0] for a in a_list]
    level = 1
    b = 2
    while b < CHUNK:
        d16 = [x.astype(BF16) for x in d]
        if level < INV_FULL_LEVELS:
            mask = qmask_ref[level]
            t16 = [_dot(x, a).astype(BF16) for x, a in zip(d16, a16)]
            d = [x - _dot(t, y) * mask for x, t, y in zip(d, t16, d16)]
        else:
            mask = qmask_odd_ref[level - INV_FULL_LEVELS]
            t16 = [_dot(_odd_block_rows(x, b), a).astype(BF16) for x, a in zip(d16, a16)]
            d = [_with_odd_block_rows(x, _odd_block_rows(x, b) - _dot(t, y) * mask, b)
                 for x, t, y in zip(d, t16, d16)]
        level += 1
        b *= 2
    return d


def _dn_kernel(nchunk, q_ref, k_ref, v_ref, z_ref, gd_ref, mr_ref, x_ref, ps_ref, alog_ref, dtb_ref,
               tri_ref, qmask_ref, qmask_odd_ref, ng_ref, wbr_ref, wo_ref, h_ref, s_ref, stage_ref):
    @pl.when(pl.program_id(1) == 0)
    def _():
        s_ref[...] = jnp.zeros_like(s_ref)

    ps = ps_ref[...]
    beta_all = _sigmoid(ps)
    g_all = -jnp.exp(alog_ref[...]) * _softplus(ps + dtb_ref[...])

    row = lax.broadcasted_iota(jnp.int32, (CHUNK, CHUNK), 0)
    col = lax.broadcasted_iota(jnp.int32, (CHUNK, CHUNK), 1)
    causal = row >= col
    strict = row > col
    tri = tri_ref[...]

    def lanes(x, j):
        return jnp.broadcast_to(x[:, j:j + 1], (CHUNK, LANES))

    g_cum, g_cum_t, exp_g_all, exp_rest_all = [], [], [], []
    for c in range(nchunk):
        g = _dot_split_rhs(tri, g_all[c * CHUNK:(c + 1) * CHUNK])
        g_cum.append(g)
        g_cum_t.append(g.T)
        exp_g_all.append(jnp.exp(g))
        exp_rest_all.append(jnp.exp(g[CHUNK - 1:CHUNK, :] - g))

    def decay_of(c, h, mask):
        gc = lanes(g_cum[c], SMALL_DECAY0 + h)
        gr = jnp.broadcast_to(g_cum_t[c][SMALL_DECAY0 + h:SMALL_DECAY0 + h + 1, :], (CHUNK, CHUNK))
        return jnp.where(mask, jnp.exp(gc - gr), 0.0)

    def block(ref, c, h):
        return ref[c * CHUNK:(c + 1) * CHUNK, h * DN_DV:(h + 1) * DN_DV]

    items = [(c, h) for c in range(nchunk) for h in range(DN_HEADS)]

    a_list, rhs16 = [], []
    for c, h in items:
        k16 = block(k_ref, c, h)
        beta = lanes(beta_all[c * CHUNK:(c + 1) * CHUNK], SMALL_BETA0 + h)
        k_beta = k16.astype(F32) * beta
        a_list.append(_dot_nt(k_beta.astype(BF16), k16) * decay_of(c, h, strict))
        exp_g = lanes(exp_g_all[c], SMALL_DECAY0 + h)
        rhs16.append(jnp.concatenate([block(v_ref, c, h).astype(F32) * beta, k_beta * exp_g], axis=1).astype(BF16))

    minv = _inv_unit_lower_many(a_list, qmask_ref, qmask_odd_ref)
    sol = [_dot(m.astype(BF16), r) for m, r in zip(minv, rhs16)]

    for c in range(nchunk):
        qa16, kdec16, sdec = [], [], []
        for h in range(DN_HEADS):
            q16 = block(q_ref, c, h)
            k16 = block(k_ref, c, h)
            exp_g = lanes(exp_g_all[c], SMALL_DECAY0 + h)
            attn = _dot_nt(q16, k16) * decay_of(c, h, causal)
            qa16.append(jnp.concatenate([(q16.astype(F32) * exp_g).astype(BF16), attn.astype(BF16)], axis=1))
            kdec16.append((k16.astype(F32) * lanes(exp_rest_all[c], SMALL_DECAY0 + h)).astype(BF16))
            sdec.append(exp_g[CHUNK - 1:CHUNK, :])
        idx = [c * DN_HEADS + h for h in range(DN_HEADS)]
        s = [s_ref[h] for h in range(DN_HEADS)]
        s16 = [x.astype(BF16) for x in s]
        ws = [_dot(sol[i][:, DN_DV:].astype(BF16), s16[h]) for h, i in enumerate(idx)]
        vn16 = [(sol[i][:, :DN_DV] - ws[h]).astype(BF16) for h, i in enumerate(idx)]
        o = [_dot(qa16[h], jnp.concatenate([s16[h], vn16[h]], axis=0)) for h in range(DN_HEADS)]
        for h in range(DN_HEADS):
            s_ref[h] = s[h] * sdec[h] + _dot_tn(kdec16[h], vn16[h])
        for h in range(DN_HEADS):
            on = o[h] * lax.rsqrt(jnp.mean(o[h] * o[h], axis=-1, keepdims=True) + EPS) * ng_ref[...]
            stage_ref[c * CHUNK:(c + 1) * CHUNK, h * DN_DV:(h + 1) * DN_DV] = (
                on * block(z_ref, c, h).astype(F32)).astype(BF16)

    y_dn = _dot(stage_ref[...], wbr_ref[...])
    merged = mr_ref[...].astype(F32) + gd_ref[...].astype(F32) * y_dn
    h_ref[...] = x_ref[...] + _dot(merged.astype(BF16), wo_ref[...])


def _dn_tables():
    idx = np.arange(CHUNK)
    r, c = idx[:, None], idx[None, :]
    qmasks, qmasks_odd = [], []
    b = 1
    while b < CHUNK:
        m = ((r // (2 * b)) == (c // (2 * b))) & ((r // b) % 2 == 1) & ((c // b) % 2 == 0)
        if len(qmasks) < INV_FULL_LEVELS:
            qmasks.append(m)
        else:
            qmasks_odd.append(m[(idx // b) % 2 == 1])
        b *= 2
    tri = (r >= c).astype(np.float32)
    return (jnp.asarray(tri, BF16), jnp.asarray(np.stack(qmasks), F32), jnp.asarray(np.stack(qmasks_odd), F32))


def _dn_call(p_conv, p_silu, p_sigm, p_small, mr, x2, alog_row, dtb_row, dn_norm_g, w_dn_br, w_o, batch, t_len, tt):
    m = batch * t_len
    nt = t_len // tt
    tri, qmask, qmask_odd = _dn_tables()
    row = lambda cb: pl.BlockSpec((tt, COLBLK), lambda b, t: (b * nt + t, cb))
    tok = lambda n: pl.BlockSpec((tt, n), lambda b, t: (b * nt + t, 0))
    const2 = lambda shape: pl.BlockSpec(shape, lambda b, t: (0, 0))
    return pl.pallas_call(
        functools.partial(_dn_kernel, tt // CHUNK),
        grid=(batch, nt),
        in_specs=[
            row(0), row(1), row(2), row(1), row(1),
            tok(D_MODEL), tok(D_MODEL), tok(LANES),
            const2((1, LANES)), const2((1, LANES)),
            const2((CHUNK, CHUNK)),
            pl.BlockSpec(qmask.shape, lambda b, t: (0, 0, 0)),
            pl.BlockSpec(qmask_odd.shape, lambda b, t: (0, 0, 0)),
            const2((1, DN_DV)),
            const2((DN_V, D_MODEL)), const2((D_MODEL, D_MODEL)),
        ],
        out_specs=tok(D_MODEL),
        out_shape=jax.ShapeDtypeStruct((m, D_MODEL), F32),
        scratch_shapes=[
            pltpu.VMEM((DN_HEADS, DN_DK, DN_DV), F32),
            pltpu.VMEM((tt, DN_V), BF16),
        ],
        compiler_params=pltpu.CompilerParams(
            dimension_semantics=("arbitrary", "arbitrary"), vmem_limit_bytes=VMEM_LIMIT),
        name="deltanet",
    )(p_conv, p_conv, p_conv, p_silu, p_sigm, mr, x2, p_small, alog_row, dtb_row, tri, qmask, qmask_odd,
      dn_norm_g, w_dn_br, w_o)


FFN_COLS = 256
FFN_STEPS = D_FF // FFN_COLS


def _ffn_kernel(h_ref, g_ref, wup_ref, cw_ref, cb_ref, wdn_ref, gf_ref, o_ref, carry_ref, up_ref, act_ref):
    @pl.when(pl.program_id(1) == 0)
    def _():
        carry_ref[...] = jnp.zeros_like(carry_ref)

    tm = h_ref.shape[0]
    h = h_ref[...]
    u = ((h * lax.rsqrt(jnp.mean(h * h, axis=-1, keepdims=True) + EPS)) * g_ref[...]).astype(BF16)

    def conv_branch(c0, buf):
        cols = slice(c0, c0 + FFN_COLS)
        up = _dot(u, wup_ref[:, cols])
        buf[0:SUBLANES, :] = carry_ref[:, cols]
        buf[SUBLANES:SUBLANES + tm, :] = up
        carry_ref[:, cols] = up[tm - SUBLANES:]
        w = cw_ref[:, cols]
        y = up * w[FFN_CONV - 1:FFN_CONV] + cb_ref[:, cols]
        for s in range(1, FFN_CONV):
            y = y + buf[SUBLANES - s:SUBLANES - s + tm, :] * w[FFN_CONV - 1 - s:FFN_CONV - s]
        return y

    for j in range(FFN_STEPS):
        a = conv_branch(j * FFN_COLS, up_ref.at[j % 2, 0])
        b = conv_branch(D_FF + j * FFN_COLS, up_ref.at[j % 2, 1])
        act_ref[:, j * FFN_COLS:(j + 1) * FFN_COLS] = (_silu(a) * b).astype(BF16)

    acc = h + _dot(act_ref[...], wdn_ref[...])
    o_ref[...] = (acc * lax.rsqrt(jnp.mean(acc * acc, axis=-1, keepdims=True) + EPS)) * gf_ref[...]


def _ffn_call(h2, g_ffn, w_up, conv_w, conv_b, w_down, g_final, batch, t_len, tm):
    m = batch * t_len
    nt = t_len // tm
    tok = pl.BlockSpec((tm, D_MODEL), lambda b, t: (b * nt + t, 0))
    const2 = lambda shape: pl.BlockSpec(shape, lambda b, t: (0, 0))
    resident = lambda shape: pl.BlockSpec(shape, lambda b, t: (0, 0), pipeline_mode=pl.Buffered(1))
    return pl.pallas_call(
        _ffn_kernel,
        grid=(batch, nt),
        in_specs=[
            tok, const2((1, D_MODEL)),
            resident((D_MODEL, 2 * D_FF)),
            const2((FFN_CONV, 2 * D_FF)), const2((1, 2 * D_FF)),
            resident((D_FF, D_MODEL)),
            const2((1, D_MODEL)),
        ],
        out_specs=tok,
        out_shape=jax.ShapeDtypeStruct((m, D_MODEL), F32),
        scratch_shapes=[
            pltpu.VMEM((SUBLANES, 2 * D_FF), F32),
            pltpu.VMEM((2, 2, SUBLANES + tm, FFN_COLS), F32),
            pltpu.VMEM((tm, D_FF), BF16),
        ],
        compiler_params=pltpu.CompilerParams(
            dimension_semantics=("arbitrary", "arbitrary"), vmem_limit_bytes=VMEM_LIMIT),
        name="convffn",
    )(h2, g_ffn, w_up, conv_w, conv_b, w_down, g_final)


def _pick_tile(t_len, want):
    tile = min(want, t_len)
    assert t_len % tile == 0 and tile % CHUNK == 0
    return tile


PROJ_TILE = 1024
RET_TILE = 512
DN_TILE = 512
FFN_TILE = 512


def kernel(x, g_mix, w_in, ret_norm_g, dn_conv_w, dn_a_log, dn_dt_bias, dn_norm_g, w_ret_br, w_dn_br, w_o, g_ffn,
           w_up, ffn_conv_w, ffn_conv_b, w_down, g_final):
    batch, t_len, d_model = x.shape
    assert d_model == D_MODEL and g_mix.shape[0] == 1 and t_len % CHUNK == 0
    m = batch * t_len
    x2 = x.astype(F32).reshape(m, D_MODEL)

    w_head = _cast_call(w_in, COL_SMALL)
    w_sigm = w_in[0, :, COL_SIGM:].astype(BF16)
    w_small = jnp.pad(w_in[0, :, COL_SMALL:COL_SIGM], ((0, 0), (0, LANES - 2 * DN_HEADS))).astype(BF16)

    conv_w = jnp.transpose(dn_conv_w[0].reshape(SHORT_CONV, PROJ_STEPS, CONV_W), (1, 0, 2))
    p_plain, p_silu, p_sigm, p_conv, p_small = _proj_call(x2, g_mix, w_head, w_sigm, w_small, conv_w, t_len,
                                                          _pick_tile(t_len, PROJ_TILE))

    cos_t, sin_t = _rope_tables(t_len)
    mr = _ret_call(p_plain, p_silu, p_sigm, cos_t, sin_t, ret_norm_g, w_ret_br[0].astype(BF16), batch, t_len,
                   _pick_tile(t_len, RET_TILE))

    alog_row = jnp.zeros((1, LANES), F32).at[0, SMALL_DECAY0:SMALL_DECAY0 + DN_HEADS].set(dn_a_log[0])
    dtb_row = jnp.zeros((1, LANES), F32).at[0, SMALL_DECAY0:SMALL_DECAY0 + DN_HEADS].set(dn_dt_bias[0])
    h = _dn_call(p_conv, p_silu, p_sigm, p_small, mr, x2, alog_row, dtb_row, dn_norm_g, w_dn_br[0].astype(BF16),
                 w_o[0].astype(BF16), batch, t_len, _pick_tile(t_len, DN_TILE))

    out = _ffn_call(h, g_ffn, w_up[0].astype(BF16), ffn_conv_w[0], ffn_conv_b, w_down[0].astype(BF16),
                    g_final.reshape(1, D_MODEL), batch, t_len, _pick_tile(t_len, FFN_TILE))
    return out.reshape(batch, t_len, D_MODEL).astype(x.dtype)
```

```python
import functools

import numpy as np
import jax
import jax.numpy as jnp
from jax import lax
from jax.experimental import pallas as pl
from jax.experimental.pallas import tpu as pltpu

D_MODEL = 1024
RET_HEADS = 8
RET_DK = 64
RET_DV = 128
DN_HEADS = 8
DN_DK = 128
DN_DV = 128
CHUNK = 128
SHORT_CONV = 4
FFN_CONV = 3
D_FF = 2816
ROPE_BASE = 10000.0
EPS = 1e-6
GN_EPS = 1e-5

RET_QK = RET_HEADS * RET_DK
RET_V = RET_HEADS * RET_DV
DN_QK = DN_HEADS * DN_DK
DN_V = DN_HEADS * DN_DV

LANES = 128
SUBLANES = 8
COLBLK = 1024
SMALL_BETA0 = 0
SMALL_DECAY0 = 8

VMEM_LIMIT = 56 * 1024 * 1024

F32 = jnp.float32
BF16 = jnp.bfloat16


def _dot(a, b):
    return jnp.dot(a, b, preferred_element_type=F32)


def _dot_nt(a, b):
    return lax.dot_general(a, b, (((1,), (1,)), ((), ())), preferred_element_type=F32)


def _dot_tn(a, b):
    return lax.dot_general(a, b, (((0,), (0,)), ((), ())), preferred_element_type=F32)


def _split3(x):
    hi = x.astype(BF16)
    r1 = x - hi.astype(F32)
    mid = r1.astype(BF16)
    lo = (r1 - mid.astype(F32)).astype(BF16)
    return hi, mid, lo


def _dot_split_rhs(a, x):
    hi, mid, lo = _split3(x)
    return _dot(a, hi) + _dot(a, mid) + _dot(a, lo)


def _sigmoid(x):
    return 0.5 + 0.5 * jnp.tanh(0.5 * x)


def _silu(x):
    h = 0.5 * x
    return h + h * jnp.tanh(h)


def _softplus(x):
    return jnp.maximum(x, 0.0) + jnp.log(1.0 + jnp.exp(-jnp.abs(x)))


PROJ_STEPS = 4
PLAIN_W = (2 * RET_QK + RET_V) // PROJ_STEPS
SILU_W = (RET_V + DN_V) // PROJ_STEPS
SIGM_W = 2 * D_MODEL // PROJ_STEPS
CONV_W = (2 * DN_QK + DN_V) // PROJ_STEPS
COL_RGATE = 2 * RET_QK + RET_V
COL_CONV = COL_RGATE + RET_V
COL_DZ = COL_CONV + 2 * DN_QK + DN_V
COL_SMALL = COL_DZ + DN_V
COL_SIGM = COL_SMALL + 2 * DN_HEADS
CONV_GROUPS = CONV_W // DN_DK
PROJ_ROWS = 256


def _proj_kernel(tiles_per_seq, x_ref, g_ref, w1_ref, w2a_ref, w2b_ref, w3_ref, ws_ref, convw_ref,
                 o1_ref, o2a_ref, o2b_ref, o3_ref, ps_ref, u_ref, carry_ref, buf_ref):
    i = pl.program_id(0)
    s = pl.program_id(1)
    tm = x_ref.shape[0]

    @pl.when(s == 0)
    def _():
        x = x_ref[...]
        r = lax.rsqrt(jnp.mean(x * x, axis=-1, keepdims=True) + EPS)
        u = ((x * r) * g_ref[...]).astype(BF16)
        u_ref[...] = u
        ps_ref[...] = _dot(u, ws_ref[...])

    @pl.when(jnp.logical_and(i == 0, s == 0))
    def _():
        carry_ref[...] = jnp.zeros_like(carry_ref)

    prev = carry_ref[s]
    buf_ref[0:SUBLANES, :] = jnp.where(i % tiles_per_seq == 0, jnp.zeros_like(prev), prev)
    w = convw_ref[s]

    def products(rb):
        r0 = rb * PROJ_ROWS
        u = u_ref[r0:r0 + PROJ_ROWS, :]
        acc = _dot(u, w3_ref[...])
        buf_ref[SUBLANES + r0:SUBLANES + r0 + PROJ_ROWS, :] = acc
        return acc, _dot(u, w2a_ref[...]), _dot(u, w2b_ref[...]), _dot(u, w1_ref[...])

    def elementwise(rb, acc, ga, gb, p):
        r0 = rb * PROJ_ROWS
        rows = slice(r0, r0 + PROJ_ROWS)
        y = acc * w[SHORT_CONV - 1:SHORT_CONV]
        for t in range(1, SHORT_CONV):
            y = y + buf_ref[SUBLANES + r0 - t:SUBLANES + r0 - t + PROJ_ROWS, :] * w[SHORT_CONV - 1 - t:SHORT_CONV - t]
        y = _silu(y)
        for l in range(CONV_GROUPS):
            head_group = s * CONV_GROUPS + l
            cols = slice(l * DN_DK, (l + 1) * DN_DK)
            yl = y[:, cols]
            r = lax.rsqrt(jnp.sum(yl * yl, axis=-1, keepdims=True) + EPS)
            r = r * jnp.where(head_group < DN_HEADS, DN_DK ** -0.5, 1.0)
            scale = jnp.where(head_group < 2 * DN_HEADS, r, 1.0)
            o3_ref[rows, cols] = (yl * scale).astype(BF16)
        o2a_ref[rows, :] = _silu(ga).astype(BF16)
        o2b_ref[rows, :] = _sigmoid(gb).astype(BF16)
        o1_ref[rows, :] = p.astype(BF16)

    nrb = tm // PROJ_ROWS
    pending = products(0)
    for rb in range(1, nrb):
        nxt = products(rb)
        elementwise(rb - 1, *pending)
        pending = nxt
    elementwise(nrb - 1, *pending)
    carry_ref[s] = buf_ref[tm:tm + SUBLANES, :]


def _cast_kernel(w_ref, o_ref):
    o_ref[...] = w_ref[...].T.astype(BF16)


def _cast_call(w_t, ncols):
    assert ncols % COLBLK == 0
    return pl.pallas_call(
        _cast_kernel,
        grid=(ncols // COLBLK,),
        in_specs=[pl.BlockSpec((COLBLK, D_MODEL), lambda j: (j, 0))],
        out_specs=pl.BlockSpec((D_MODEL, COLBLK), lambda j: (0, j)),
        out_shape=jax.ShapeDtypeStruct((D_MODEL, ncols), BF16),
        compiler_params=pltpu.CompilerParams(dimension_semantics=("arbitrary",), vmem_limit_bytes=VMEM_LIMIT),
        name="cast_w_in",
    )(w_t)


def _cast_tail_kernel(ws_ref, wg_ref, os_ref, og_ref):
    rows = lax.broadcasted_iota(jnp.int32, ws_ref.shape, 0)
    small = jnp.where(rows < COL_SIGM - COL_SMALL, ws_ref[...], 0.0)
    os_ref[...] = small.T.astype(BF16)
    og_ref[...] = wg_ref[...].T.astype(BF16)


def _cast_tail_call(w_t):
    nblk = (w_t.shape[0] - COL_SIGM) // COLBLK
    return pl.pallas_call(
        _cast_tail_kernel,
        grid=(nblk,),
        in_specs=[
            pl.BlockSpec((pl.Element(LANES), pl.Element(D_MODEL)), lambda j: (COL_SMALL, 0)),
            pl.BlockSpec((pl.Element(COLBLK), pl.Element(D_MODEL)), lambda j: (pl.multiple_of(COL_SIGM + j * COLBLK, SUBLANES), 0)),
        ],
        out_specs=[
            pl.BlockSpec((D_MODEL, LANES), lambda j: (0, 0)),
            pl.BlockSpec((D_MODEL, COLBLK), lambda j: (0, j)),
        ],
        out_shape=[
            jax.ShapeDtypeStruct((D_MODEL, LANES), BF16),
            jax.ShapeDtypeStruct((D_MODEL, nblk * COLBLK), BF16),
        ],
        compiler_params=pltpu.CompilerParams(dimension_semantics=("arbitrary",), vmem_limit_bytes=VMEM_LIMIT),
        name="cast_w_tail",
    )(w_t, w_t)


def _proj_call(x2, g_mix, w_head, w_sigm, w_small, conv_w, t_len, tm):
    m = x2.shape[0]
    assert t_len % tm == 0 and w_head.shape[1] == COL_SMALL
    half = PROJ_STEPS // 2
    ospec = lambda n: pl.BlockSpec((tm, n), lambda i, s: (i, s))
    return pl.pallas_call(
        functools.partial(_proj_kernel, t_len // tm),
        grid=(m // tm, PROJ_STEPS),
        in_specs=[
            pl.BlockSpec((tm, D_MODEL), lambda i, s: (i, 0)),
            pl.BlockSpec((1, D_MODEL), lambda i, s: (0, 0)),
            pl.BlockSpec((D_MODEL, PLAIN_W), lambda i, s: (0, s)),
            pl.BlockSpec((D_MODEL, SILU_W),
                         lambda i, s: (0, jnp.where(s < half, COL_RGATE // SILU_W + s, COL_DZ // SILU_W + s - half))),
            pl.BlockSpec((D_MODEL, SIGM_W), lambda i, s: (0, s)),
            pl.BlockSpec((D_MODEL, CONV_W), lambda i, s: (0, COL_CONV // CONV_W + s)),
            pl.BlockSpec((D_MODEL, LANES), lambda i, s: (0, 0)),
            pl.BlockSpec((PROJ_STEPS, SHORT_CONV, CONV_W), lambda i, s: (0, 0, 0)),
        ],
        out_specs=[
            ospec(PLAIN_W), ospec(SILU_W), ospec(SIGM_W), ospec(CONV_W),
            pl.BlockSpec((tm, LANES), lambda i, s: (i, 0)),
        ],
        out_shape=[
            jax.ShapeDtypeStruct((m, PROJ_STEPS * PLAIN_W), BF16),
            jax.ShapeDtypeStruct((m, PROJ_STEPS * SILU_W), BF16),
            jax.ShapeDtypeStruct((m, PROJ_STEPS * SIGM_W), BF16),
            jax.ShapeDtypeStruct((m, PROJ_STEPS * CONV_W), BF16),
            jax.ShapeDtypeStruct((m, LANES), F32),
        ],
        scratch_shapes=[
            pltpu.VMEM((tm, D_MODEL), BF16),
            pltpu.VMEM((PROJ_STEPS, SUBLANES, CONV_W), F32),
            pltpu.VMEM((SUBLANES + tm, CONV_W), F32),
        ],
        compiler_params=pltpu.CompilerParams(
            dimension_semantics=("arbitrary", "arbitrary"), vmem_limit_bytes=VMEM_LIMIT),
        name="proj",
    )(x2, g_mix, w_head, w_head, w_sigm, w_head, w_small, conv_w)


def _ret_kernel(decays, nchunk, qk_ref, v_ref, gate_ref, gr_ref, cos_ref, sin_ref, dmat_ref, xi_ref,
                zeta_ref, gn_ref, wbr_ref, mr_ref, s_ref, stage_ref):
    @pl.when(pl.program_id(1) == 0)
    def _():
        s_ref[...] = jnp.zeros_like(s_ref)

    lane = lax.broadcasted_iota(jnp.int32, (CHUNK, LANES), 1)
    first_half = (lane % RET_DK) < (RET_DK // 2)
    head_lo = lane < RET_DK

    def rot(x, cos_t, sin_t):
        partner = jnp.where(first_half, pltpu.roll(x, LANES - RET_DK // 2, 1), pltpu.roll(x, RET_DK // 2, 1))
        return x * cos_t + partner * sin_t

    qb, kb, qx, kz = {}, {}, {}, {}
    for c in range(nchunk):
        rows = slice(c * CHUNK, (c + 1) * CHUNK)
        cos_t = cos_ref[rows, :]
        sin_t = sin_ref[rows, :]
        for p in range(RET_HEADS // 2):
            q = rot(qk_ref[rows, p * LANES:(p + 1) * LANES].astype(F32), cos_t, sin_t)
            k = rot(qk_ref[rows, RET_QK + p * LANES:RET_QK + (p + 1) * LANES].astype(F32), cos_t, sin_t)
            k = k * (RET_DK ** -0.5)
            qb[c, p] = q.astype(BF16)
            kb[c, p] = k.astype(BF16)
            qx[c, p] = (q * xi_ref[p]).astype(BF16)
            kz[c, p] = (k * zeta_ref[p]).astype(BF16)

    items = [(c, h) for c in range(nchunk) for h in range(RET_HEADS)]

    def head_mask(h):
        return head_lo if h % 2 == 0 else jnp.logical_not(head_lo)

    def v_of(c, h):
        return v_ref[c * CHUNK:(c + 1) * CHUNK, h * RET_DV:(h + 1) * RET_DV]

    scores16 = {}
    for c, h in items:
        qm = jnp.where(head_mask(h), qb[c, h // 2], jnp.zeros((CHUNK, LANES), BF16))
        scores16[c, h] = (_dot_nt(qm, kb[c, h // 2]) * dmat_ref[h]).astype(BF16)
    kv = {(c, h): _dot_tn(kz[c, h // 2], v_of(c, h)) for c, h in items}

    for c in range(nchunk):
        rows = slice(c * CHUNK, (c + 1) * CHUNK)
        s = [s_ref[h] for h in range(RET_HEADS)]
        ro = []
        for h in range(RET_HEADS):
            qxm = jnp.where(head_mask(h), qx[c, h // 2], jnp.zeros((CHUNK, LANES), BF16))
            ro.append(_dot(jnp.concatenate([scores16[c, h], qxm], axis=1),
                           jnp.concatenate([v_of(c, h), s[h].astype(BF16)], axis=0)))
        for h in range(RET_HEADS):
            s_ref[h] = s[h] * decays[h] + kv[c, h]
        for h in range(RET_HEADS):
            cols = slice(h * RET_DV, (h + 1) * RET_DV)
            mu = jnp.mean(ro[h], axis=-1, keepdims=True)
            d = ro[h] - mu
            var = jnp.mean(d * d, axis=-1, keepdims=True)
            y = (d * lax.rsqrt(var + GN_EPS)) * gn_ref[:, cols] * gate_ref[rows, cols].astype(F32)
            stage_ref[rows, cols] = y.astype(BF16)

    y_ret = _dot(stage_ref[...], wbr_ref[...])
    mr_ref[...] = (gr_ref[...].astype(F32) * y_ret).astype(BF16)


def _ret_tables():
    h = np.arange(RET_HEADS, dtype=np.float64)
    gamma = 1.0 - 2.0 ** (-5.0 - h)
    log_g = np.log(gamma)
    idx = np.arange(CHUNK, dtype=np.float64)
    diff = idx[:, None] - idx[None, :]
    dmat = np.where(diff >= 0, np.exp(log_g[:, None, None] * np.maximum(diff, 0.0)[None]), 0.0)
    zeta = np.exp(log_g[:, None] * (CHUNK - 1.0 - idx)[None, :])
    xi = np.exp(log_g[:, None] * (idx + 1.0)[None, :])
    decays = tuple(float(v) for v in np.exp(log_g * CHUNK))

    def pair_table(t):
        t = t.reshape(RET_HEADS // 2, 2, CHUNK)
        return np.repeat(np.transpose(t, (0, 2, 1)), RET_DK, axis=2)

    return (jnp.asarray(dmat, F32), jnp.asarray(pair_table(xi), F32), jnp.asarray(pair_table(zeta), F32), decays)


def _rope_tables(t_len):
    half = RET_DK // 2
    inv = ROPE_BASE ** (-np.arange(0, RET_DK, 2, dtype=np.float64) / RET_DK)
    ang = np.arange(t_len, dtype=np.float64)[:, None] * inv[None, :]
    cos, sin = np.cos(ang), np.sin(ang)
    cos_t = np.concatenate([cos, cos], axis=1)
    sin_t = np.concatenate([-sin, sin], axis=1)
    assert cos_t.shape[1] == 2 * half
    reps = LANES // RET_DK
    return jnp.asarray(np.tile(cos_t, (1, reps)), F32), jnp.asarray(np.tile(sin_t, (1, reps)), F32)


def _ret_call(p_plain, p_silu, p_sigm, cos_t, sin_t, ret_norm_g, w_ret_br, batch, t_len, tt):
    m = batch * t_len
    nt = t_len // tt
    dmat, xi, zeta, decays = _ret_tables()
    row = lambda cb: pl.BlockSpec((tt, COLBLK), lambda b, t: (b * nt + t, cb))
    const2 = lambda shape: pl.BlockSpec(shape, lambda b, t: (0, 0))
    const3 = lambda shape: pl.BlockSpec(shape, lambda b, t: (0, 0, 0))
    return pl.pallas_call(
        functools.partial(_ret_kernel, decays, tt // CHUNK),
        grid=(batch, nt),
        in_specs=[
            row(0), row(1), row(0), row(0),
            pl.BlockSpec((tt, LANES), lambda b, t: (t, 0)),
            pl.BlockSpec((tt, LANES), lambda b, t: (t, 0)),
            const3((RET_HEADS, CHUNK, CHUNK)),
            const3((RET_HEADS // 2, CHUNK, LANES)),
            const3((RET_HEADS // 2, CHUNK, LANES)),
            const2((1, RET_V)),
            const2((RET_V, D_MODEL)),
        ],
        out_specs=pl.BlockSpec((tt, D_MODEL), lambda b, t: (b * nt + t, 0)),
        out_shape=jax.ShapeDtypeStruct((m, D_MODEL), BF16),
        scratch_shapes=[
            pltpu.VMEM((RET_HEADS, LANES, RET_DV), F32),
            pltpu.VMEM((tt, RET_V), BF16),
        ],
        compiler_params=pltpu.CompilerParams(
            dimension_semantics=("arbitrary", "arbitrary"), vmem_limit_bytes=VMEM_LIMIT),
        name="retention",
    )(p_plain, p_plain, p_silu, p_sigm, cos_t, sin_t, dmat, xi, zeta, ret_norm_g, w_ret_br)


INV_FULL_LEVELS = 3


def _odd_block_rows(x, b):
    return jnp.concatenate([x[(2 * i + 1) * b:(2 * i + 2) * b] for i in range(CHUNK // (2 * b))], axis=0)


def _with_odd_block_rows(x, odd, b):
    parts = []
    for i in range(CHUNK // (2 * b)):
        parts += [x[2 * i * b:(2 * i + 1) * b], odd[i * b:(i + 1) * b]]
    return jnp.concatenate(parts, axis=0)


def _inv_unit_lower_many(a_list, qmask_ref, qmask_odd_ref):
    row = lax.broadcasted_iota(jnp.int32, (CHUNK, CHUNK), 0)
    col = lax.broadcasted_iota(jnp.int32, (CHUNK, CHUNK), 1)
    eye = jnp.where(row == col, 1.0, 0.0).astype(F32)
    a16 = [a.astype(BF16) for a in a_list]
    d = [eye - a * qmask_ref[0] for a in a_list]
    level = 1
    b = 2
    while b < CHUNK:
        d16 = [x.astype(BF16) for x in d]
        if level < INV_FULL_LEVELS:
            mask = qmask_ref[level]
            t16 = [_dot(x, a).astype(BF16) for x, a in zip(d16, a16)]
            d = [x - _dot(t, y) * mask for x, t, y in zip(d, t16, d16)]
        else:
            mask = qmask_odd_ref[level - INV_FULL_LEVELS]
            t16 = [_dot(_odd_block_rows(x, b), a).astype(BF16) for x, a in zip(d16, a16)]
            d = [_with_odd_block_rows(x, _odd_block_rows(x, b) - _dot(t, y) * mask, b)
                 for x, t, y in zip(d, t16, d16)]
        level += 1
        b *= 2
    return d


def _dn_kernel(nchunk, q_ref, k_ref, v_ref, z_ref, gd_ref, mr_ref, x_ref, ps_ref, alog_ref, dtb_ref,
               tri_ref, qmask_ref, qmask_odd_ref, ng_ref, wbr_ref, wo_ref, h_ref, s_ref, stage_ref):
    @pl.when(pl.program_id(1) == 0)
    def _():
        s_ref[...] = jnp.zeros_like(s_ref)

    ps = ps_ref[...]
    beta_all = _sigmoid(ps)
    g_all = -jnp.exp(alog_ref[...]) * _softplus(ps + dtb_ref[...])

    row = lax.broadcasted_iota(jnp.int32, (CHUNK, CHUNK), 0)
    col = lax.broadcasted_iota(jnp.int32, (CHUNK, CHUNK), 1)
    causal = row >= col
    strict = row > col
    tri = tri_ref[...]

    def lanes(x, j):
        return jnp.broadcast_to(x[:, j:j + 1], (CHUNK, LANES))

    g_cum, g_cum_t, exp_g_all, exp_rest_all = [], [], [], []
    for c in range(nchunk):
        g = _dot_split_rhs(tri, g_all[c * CHUNK:(c + 1) * CHUNK])
        g_cum.append(g)
        g_cum_t.append(g.T)
        exp_g_all.append(jnp.exp(g))
        exp_rest_all.append(jnp.exp(g[CHUNK - 1:CHUNK, :] - g))

    def decay_of(c, h, mask):
        gc = lanes(g_cum[c], SMALL_DECAY0 + h)
        gr = jnp.broadcast_to(g_cum_t[c][SMALL_DECAY0 + h:SMALL_DECAY0 + h + 1, :], (CHUNK, CHUNK))
        return jnp.where(mask, jnp.exp(gc - gr), 0.0)

    def block(ref, c, h):
        return ref[c * CHUNK:(c + 1) * CHUNK, h * DN_DV:(h + 1) * DN_DV]

    items = [(c, h) for c in range(nchunk) for h in range(DN_HEADS)]

    a_list, rhs16 = [], []
    for c, h in items:
        k16 = block(k_ref, c, h)
        beta = lanes(beta_all[c * CHUNK:(c + 1) * CHUNK], SMALL_BETA0 + h)
        k_beta = k16.astype(F32) * beta
        a_list.append(_dot_nt(k_beta.astype(BF16), k16) * decay_of(c, h, strict))
        exp_g = lanes(exp_g_all[c], SMALL_DECAY0 + h)
        rhs16.append(jnp.concatenate([block(v_ref, c, h).astype(F32) * beta, k_beta * exp_g], axis=1).astype(BF16))

    minv = _inv_unit_lower_many(a_list, qmask_ref, qmask_odd_ref)
    sol = [_dot(m.astype(BF16), r) for m, r in zip(minv, rhs16)]

    for c in range(nchunk):
        qa16, kdec16, sdec = [], [], []
        for h in range(DN_HEADS):
            q16 = block(q_ref, c, h)
            k16 = block(k_ref, c, h)
            exp_g = lanes(exp_g_all[c], SMALL_DECAY0 + h)
            attn = _dot_nt(q16, k16) * decay_of(c, h, causal)
            qa16.append(jnp.concatenate([(q16.astype(F32) * exp_g).astype(BF16), attn.astype(BF16)], axis=1))
            kdec16.append((k16.astype(F32) * lanes(exp_rest_all[c], SMALL_DECAY0 + h)).astype(BF16))
            sdec.append(exp_g[CHUNK - 1:CHUNK, :])
        idx = [c * DN_HEADS + h for h in range(DN_HEADS)]
        s = [s_ref[h] for h in range(DN_HEADS)]
        s16 = [x.astype(BF16) for x in s]
        ws = [_dot(sol[i][:, DN_DV:].astype(BF16), s16[h]) for h, i in enumerate(idx)]
        vn16 = [(sol[i][:, :DN_DV] - ws[h]).astype(BF16) for h, i in enumerate(idx)]
        o = [_dot(qa16[h], jnp.concatenate([s16[h], vn16[h]], axis=0)) for h in range(DN_HEADS)]
        for h in range(DN_HEADS):
            s_ref[h] = s[h] * sdec[h] + _dot_tn(kdec16[h], vn16[h])
        for h in range(DN_HEADS):
            on = o[h] * lax.rsqrt(jnp.mean(o[h] * o[h], axis=-1, keepdims=True) + EPS) * ng_ref[...]
            stage_ref[c * CHUNK:(c + 1) * CHUNK, h * DN_DV:(h + 1) * DN_DV] = (
                on * block(z_ref, c, h).astype(F32)).astype(BF16)

    y_dn = _dot(stage_ref[...], wbr_ref[...])
    merged = mr_ref[...].astype(F32) + gd_ref[...].astype(F32) * y_dn
    h_ref[...] = x_ref[...] + _dot(merged.astype(BF16), wo_ref[...])


def _dn_tables():
    idx = np.arange(CHUNK)
    r, c = idx[:, None], idx[None, :]
    qmasks, qmasks_odd = [], []
    b = 1
    while b < CHUNK:
        m = ((r // (2 * b)) == (c // (2 * b))) & ((r // b) % 2 == 1) & ((c // b) % 2 == 0)
        if len(qmasks) < INV_FULL_LEVELS:
            qmasks.append(m)
        else:
            qmasks_odd.append(m[(idx // b) % 2 == 1])
        b *= 2
    tri = (r >= c).astype(np.float32)
    return (jnp.asarray(tri, BF16), jnp.asarray(np.stack(qmasks), F32), jnp.asarray(np.stack(qmasks_odd), F32))


def _dn_call(p_conv, p_silu, p_sigm, p_small, mr, x2, alog_row, dtb_row, dn_norm_g, w_dn_br, w_o, batch, t_len, tt):
    m = batch * t_len
    nt = t_len // tt
    tri, qmask, qmask_odd = _dn_tables()
    row = lambda cb: pl.BlockSpec((tt, COLBLK), lambda b, t: (b * nt + t, cb))
    tok = lambda n: pl.BlockSpec((tt, n), lambda b, t: (b * nt + t, 0))
    const2 = lambda shape: pl.BlockSpec(shape, lambda b, t: (0, 0))
    return pl.pallas_call(
        functools.partial(_dn_kernel, tt // CHUNK),
        grid=(batch, nt),
        in_specs=[
            row(0), row(1), row(2), row(1), row(1),
            tok(D_MODEL), tok(D_MODEL), tok(LANES),
            const2((1, LANES)), const2((1, LANES)),
            const2((CHUNK, CHUNK)),
            pl.BlockSpec(qmask.shape, lambda b, t: (0, 0, 0)),
            pl.BlockSpec(qmask_odd.shape, lambda b, t: (0, 0, 0)),
            const2((1, DN_DV)),
            const2((DN_V, D_MODEL)), const2((D_MODEL, D_MODEL)),
        ],
        out_specs=tok(D_MODEL),
        out_shape=jax.ShapeDtypeStruct((m, D_MODEL), F32),
        scratch_shapes=[
            pltpu.VMEM((DN_HEADS, DN_DK, DN_DV), F32),
            pltpu.VMEM((tt, DN_V), BF16),
        ],
        compiler_params=pltpu.CompilerParams(
            dimension_semantics=("arbitrary", "arbitrary"), vmem_limit_bytes=VMEM_LIMIT),
        name="deltanet",
    )(p_conv, p_conv, p_conv, p_silu, p_sigm, mr, x2, p_small, alog_row, dtb_row, tri, qmask, qmask_odd,
      dn_norm_g, w_dn_br, w_o)


FFN_COLS = 256
FFN_STEPS = D_FF // FFN_COLS


def _ffn_kernel(h_ref, g_ref, wup_ref, cw_ref, cb_ref, wdn_ref, gf_ref, o_ref, carry_ref, up_ref, act_ref):
    @pl.when(pl.program_id(1) == 0)
    def _():
        carry_ref[...] = jnp.zeros_like(carry_ref)

    tm = h_ref.shape[0]
    h = h_ref[...]
    u = ((h * lax.rsqrt(jnp.mean(h * h, axis=-1, keepdims=True) + EPS)) * g_ref[...]).astype(BF16)

    def conv_branch(c0, buf):
        cols = slice(c0, c0 + FFN_COLS)
        up = _dot(u, wup_ref[:, cols])
        buf[0:SUBLANES, :] = carry_ref[:, cols]
        buf[SUBLANES:SUBLANES + tm, :] = up
        carry_ref[:, cols] = up[tm - SUBLANES:]
        w = cw_ref[:, cols]
        y = up * w[FFN_CONV - 1:FFN_CONV] + cb_ref[:, cols]
        for s in range(1, FFN_CONV):
            y = y + buf[SUBLANES - s:SUBLANES - s + tm, :] * w[FFN_CONV - 1 - s:FFN_CONV - s]
        return y

    for j in range(FFN_STEPS):
        a = conv_branch(j * FFN_COLS, up_ref.at[j % 2, 0])
        b = conv_branch(D_FF + j * FFN_COLS, up_ref.at[j % 2, 1])
        act_ref[:, j * FFN_COLS:(j + 1) * FFN_COLS] = (_silu(a) * b).astype(BF16)

    acc = h + _dot(act_ref[...], wdn_ref[...])
    o_ref[...] = (acc * lax.rsqrt(jnp.mean(acc * acc, axis=-1, keepdims=True) + EPS)) * gf_ref[...]


def _ffn_call(h2, g_ffn, w_up, conv_w, conv_b, w_down, g_final, batch, t_len, tm):
    m = batch * t_len
    nt = t_len // tm
    tok = pl.BlockSpec((tm, D_MODEL), lambda b, t: (b * nt + t, 0))
    const2 = lambda shape: pl.BlockSpec(shape, lambda b, t: (0, 0))
    resident = lambda shape: pl.BlockSpec(shape, lambda b, t: (0, 0), pipeline_mode=pl.Buffered(1))
    return pl.pallas_call(
        _ffn_kernel,
        grid=(batch, nt),
        in_specs=[
            tok, const2((1, D_MODEL)),
            resident((D_MODEL, 2 * D_FF)),
            const2((FFN_CONV, 2 * D_FF)), const2((1, 2 * D_FF)),
            resident((D_FF, D_MODEL)),
            const2((1, D_MODEL)),
        ],
        out_specs=tok,
        out_shape=jax.ShapeDtypeStruct((m, D_MODEL), F32),
        scratch_shapes=[
            pltpu.VMEM((SUBLANES, 2 * D_FF), F32),
            pltpu.VMEM((2, 2, SUBLANES + tm, FFN_COLS), F32),
            pltpu.VMEM((tm, D_FF), BF16),
        ],
        compiler_params=pltpu.CompilerParams(
            dimension_semantics=("arbitrary", "arbitrary"), vmem_limit_bytes=VMEM_LIMIT),
        name="convffn",
    )(h2, g_ffn, w_up, conv_w, conv_b, w_down, g_final)


def _pick_tile(t_len, want):
    tile = min(want, t_len)
    assert t_len % tile == 0 and tile % CHUNK == 0
    return tile


PROJ_TILE = 1024
RET_TILE = 512
DN_TILE = 512
FFN_TILE = 512


def kernel(x, g_mix, w_in, ret_norm_g, dn_conv_w, dn_a_log, dn_dt_bias, dn_norm_g, w_ret_br, w_dn_br, w_o, g_ffn,
           w_up, ffn_conv_w, ffn_conv_b, w_down, g_final):
    batch, t_len, d_model = x.shape
    assert d_model == D_MODEL and g_mix.shape[0] == 1 and t_len % CHUNK == 0
    m = batch * t_len
    x2 = x.astype(F32).reshape(m, D_MODEL)

    w_t = jnp.swapaxes(w_in[0], 0, 1)
    w_head = _cast_call(w_t, COL_SMALL)
    w_small, w_sigm = _cast_tail_call(w_t)

    conv_w = jnp.transpose(dn_conv_w[0].reshape(SHORT_CONV, PROJ_STEPS, CONV_W), (1, 0, 2))
    p_plain, p_silu, p_sigm, p_conv, p_small = _proj_call(x2, g_mix, w_head, w_sigm, w_small, conv_w, t_len,
                                                          _pick_tile(t_len, PROJ_TILE))

    cos_t, sin_t = _rope_tables(t_len)
    mr = _ret_call(p_plain, p_silu, p_sigm, cos_t, sin_t, ret_norm_g, w_ret_br[0].astype(BF16), batch, t_len,
                   _pick_tile(t_len, RET_TILE))

    alog_row = jnp.zeros((1, LANES), F32).at[0, SMALL_DECAY0:SMALL_DECAY0 + DN_HEADS].set(dn_a_log[0])
    dtb_row = jnp.zeros((1, LANES), F32).at[0, SMALL_DECAY0:SMALL_DECAY0 + DN_HEADS].set(dn_dt_bias[0])
    h = _dn_call(p_conv, p_silu, p_sigm, p_small, mr, x2, alog_row, dtb_row, dn_norm_g, w_dn_br[0].astype(BF16),
                 w_o[0].astype(BF16), batch, t_len, _pick_tile(t_len, DN_TILE))

    out = _ffn_call(h, g_ffn, w_up[0].astype(BF16), ffn_conv_w[0], ffn_conv_b, w_down[0].astype(BF16),
                    g_final.reshape(1, D_MODEL), batch, t_len, _pick_tile(t_len, FFN_TILE))
    return out.reshape(batch, t_len, D_MODEL).astype(x.dtype)
```

```python
import functools

import numpy as np
import jax
import jax.numpy as jnp
from jax import lax
from jax.experimental import pallas as pl
from jax.experimental.pallas import tpu as pltpu

D_MODEL = 1024
RET_HEADS = 8
RET_DK = 64
RET_DV = 128
DN_HEADS = 8
DN_DK = 128
DN_DV = 128
CHUNK = 128
SHORT_CONV = 4
FFN_CONV = 3
D_FF = 2816
ROPE_BASE = 10000.0
EPS = 1e-6
GN_EPS = 1e-5

RET_QK = RET_HEADS * RET_DK
RET_V = RET_HEADS * RET_DV
DN_QK = DN_HEADS * DN_DK
DN_V = DN_HEADS * DN_DV

LANES = 128
SUBLANES = 8
COLBLK = 1024
SMALL_BETA0 = 0
SMALL_DECAY0 = 8

VMEM_LIMIT = 56 * 1024 * 1024

F32 = jnp.float32
BF16 = jnp.bfloat16


def _dot(a, b):
    return jnp.dot(a, b, preferred_element_type=F32)


def _dot_nt(a, b):
    return lax.dot_general(a, b, (((1,), (1,)), ((), ())), preferred_element_type=F32)


def _dot_tn(a, b):
    return lax.dot_general(a, b, (((0,), (0,)), ((), ())), preferred_element_type=F32)


def _split3(x):
    hi = x.astype(BF16)
    r1 = x - hi.astype(F32)
    mid = r1.astype(BF16)
    lo = (r1 - mid.astype(F32)).astype(BF16)
    return hi, mid, lo


def _dot_split_rhs(a, x):
    hi, mid, lo = _split3(x)
    return _dot(a, hi) + _dot(a, mid) + _dot(a, lo)


def _sigmoid(x):
    return 0.5 + 0.5 * jnp.tanh(0.5 * x)


def _silu(x):
    h = 0.5 * x
    return h + h * jnp.tanh(h)


def _softplus(x):
    return jnp.maximum(x, 0.0) + jnp.log(1.0 + jnp.exp(-jnp.abs(x)))


PROJ_STEPS = 4
PLAIN_W = (2 * RET_QK + RET_V) // PROJ_STEPS
SILU_W = (RET_V + DN_V) // PROJ_STEPS
SIGM_W = 2 * D_MODEL // PROJ_STEPS
CONV_W = (2 * DN_QK + DN_V) // PROJ_STEPS
COL_RGATE = 2 * RET_QK + RET_V
COL_CONV = COL_RGATE + RET_V
COL_DZ = COL_CONV + 2 * DN_QK + DN_V
COL_SMALL = COL_DZ + DN_V
COL_SIGM = COL_SMALL + 2 * DN_HEADS
CONV_GROUPS = CONV_W // DN_DK
PROJ_ROWS = 256


def _proj_kernel(tiles_per_seq, x_ref, g_ref, w1_ref, w2a_ref, w2b_ref, w3_ref, ws_ref, convw_ref,
                 o1_ref, o2a_ref, o2b_ref, o3_ref, ps_ref, u_ref, carry_ref, buf_ref):
    i = pl.program_id(0)
    s = pl.program_id(1)
    tm = x_ref.shape[0]

    @pl.when(s == 0)
    def _():
        x = x_ref[...]
        r = lax.rsqrt(jnp.mean(x * x, axis=-1, keepdims=True) + EPS)
        u = ((x * r) * g_ref[...]).astype(BF16)
        u_ref[...] = u
        ps_ref[...] = _dot(u, ws_ref[...])

    @pl.when(jnp.logical_and(i == 0, s == 0))
    def _():
        carry_ref[...] = jnp.zeros_like(carry_ref)

    prev = carry_ref[s]
    buf_ref[0:SUBLANES, :] = jnp.where(i % tiles_per_seq == 0, jnp.zeros_like(prev), prev)
    w = convw_ref[s]

    def products(rb):
        r0 = rb * PROJ_ROWS
        u = u_ref[r0:r0 + PROJ_ROWS, :]
        acc = _dot(u, w3_ref[...])
        buf_ref[SUBLANES + r0:SUBLANES + r0 + PROJ_ROWS, :] = acc
        return acc, _dot(u, w2a_ref[...]), _dot(u, w2b_ref[...]), _dot(u, w1_ref[...])

    def elementwise(rb, acc, ga, gb, p):
        r0 = rb * PROJ_ROWS
        rows = slice(r0, r0 + PROJ_ROWS)
        y = acc * w[SHORT_CONV - 1:SHORT_CONV]
        for t in range(1, SHORT_CONV):
            y = y + buf_ref[SUBLANES + r0 - t:SUBLANES + r0 - t + PROJ_ROWS, :] * w[SHORT_CONV - 1 - t:SHORT_CONV - t]
        y = _silu(y)
        for l in range(CONV_GROUPS):
            head_group = s * CONV_GROUPS + l
            cols = slice(l * DN_DK, (l + 1) * DN_DK)
            yl = y[:, cols]
            r = lax.rsqrt(jnp.sum(yl * yl, axis=-1, keepdims=True) + EPS)
            r = r * jnp.where(head_group < DN_HEADS, DN_DK ** -0.5, 1.0)
            scale = jnp.where(head_group < 2 * DN_HEADS, r, 1.0)
            o3_ref[rows, cols] = (yl * scale).astype(BF16)
        o2a_ref[rows, :] = _silu(ga).astype(BF16)
        o2b_ref[rows, :] = _sigmoid(gb).astype(BF16)
        o1_ref[rows, :] = p.astype(BF16)

    nrb = tm // PROJ_ROWS
    pending = products(0)
    for rb in range(1, nrb):
        nxt = products(rb)
        elementwise(rb - 1, *pending)
        pending = nxt
    elementwise(nrb - 1, *pending)
    carry_ref[s] = buf_ref[tm:tm + SUBLANES, :]


def _cast_kernel(w_ref, o_ref):
    o_ref[...] = w_ref[...].T.astype(BF16)


def _cast_call(w_t, ncols):
    assert ncols % COLBLK == 0
    return pl.pallas_call(
        _cast_kernel,
        grid=(ncols // COLBLK,),
        in_specs=[pl.BlockSpec((COLBLK, D_MODEL), lambda j: (j, 0))],
        out_specs=pl.BlockSpec((D_MODEL, COLBLK), lambda j: (0, j)),
        out_shape=jax.ShapeDtypeStruct((D_MODEL, ncols), BF16),
        compiler_params=pltpu.CompilerParams(dimension_semantics=("arbitrary",), vmem_limit_bytes=VMEM_LIMIT),
        name="cast_w_in",
    )(w_t)


def _cast_tail_kernel(ws_ref, wg_ref, os_ref, og_ref):
    rows = lax.broadcasted_iota(jnp.int32, ws_ref.shape, 0)
    small = jnp.where(rows < COL_SIGM - COL_SMALL, ws_ref[...], 0.0)
    os_ref[...] = small.T.astype(BF16)
    og_ref[...] = wg_ref[...].T.astype(BF16)


def _cast_tail_call(w_t):
    nblk = (w_t.shape[0] - COL_SIGM) // COLBLK
    return pl.pallas_call(
        _cast_tail_kernel,
        grid=(nblk,),
        in_specs=[
            pl.BlockSpec((pl.Element(LANES), pl.Element(D_MODEL)), lambda j: (COL_SMALL, 0)),
            pl.BlockSpec((pl.Element(COLBLK), pl.Element(D_MODEL)), lambda j: (pl.multiple_of(COL_SIGM + j * COLBLK, SUBLANES), 0)),
        ],
        out_specs=[
            pl.BlockSpec((D_MODEL, LANES), lambda j: (0, 0)),
            pl.BlockSpec((D_MODEL, COLBLK), lambda j: (0, j)),
        ],
        out_shape=[
            jax.ShapeDtypeStruct((D_MODEL, LANES), BF16),
            jax.ShapeDtypeStruct((D_MODEL, nblk * COLBLK), BF16),
        ],
        compiler_params=pltpu.CompilerParams(dimension_semantics=("arbitrary",), vmem_limit_bytes=VMEM_LIMIT),
        name="cast_w_tail",
    )(w_t, w_t)


def _proj_call(x2, g_mix, w_head, w_sigm, w_small, conv_w, t_len, tm):
    m = x2.shape[0]
    assert t_len % tm == 0 and w_head.shape[1] == COL_SMALL
    half = PROJ_STEPS // 2
    ospec = lambda n: pl.BlockSpec((tm, n), lambda i, s: (i, s))
    return pl.pallas_call(
        functools.partial(_proj_kernel, t_len // tm),
        grid=(m // tm, PROJ_STEPS),
        in_specs=[
            pl.BlockSpec((tm, D_MODEL), lambda i, s: (i, 0)),
            pl.BlockSpec((1, D_MODEL), lambda i, s: (0, 0)),
            pl.BlockSpec((D_MODEL, PLAIN_W), lambda i, s: (0, s)),
            pl.BlockSpec((D_MODEL, SILU_W),
                         lambda i, s: (0, jnp.where(s < half, COL_RGATE // SILU_W + s, COL_DZ // SILU_W + s - half))),
            pl.BlockSpec((D_MODEL, SIGM_W), lambda i, s: (0, s)),
            pl.BlockSpec((D_MODEL, CONV_W), lambda i, s: (0, COL_CONV // CONV_W + s)),
            pl.BlockSpec((D_MODEL, LANES), lambda i, s: (0, 0)),
            pl.BlockSpec((PROJ_STEPS, SHORT_CONV, CONV_W), lambda i, s: (0, 0, 0)),
        ],
        out_specs=[
            ospec(PLAIN_W), ospec(SILU_W), ospec(SIGM_W), ospec(CONV_W),
            pl.BlockSpec((tm, LANES), lambda i, s: (i, 0)),
        ],
        out_shape=[
            jax.ShapeDtypeStruct((m, PROJ_STEPS * PLAIN_W), BF16),
            jax.ShapeDtypeStruct((m, PROJ_STEPS * SILU_W), BF16),
            jax.ShapeDtypeStruct((m, PROJ_STEPS * SIGM_W), BF16),
            jax.ShapeDtypeStruct((m, PROJ_STEPS * CONV_W), BF16),
            jax.ShapeDtypeStruct((m, LANES), F32),
        ],
        scratch_shapes=[
            pltpu.VMEM((tm, D_MODEL), BF16),
            pltpu.VMEM((PROJ_STEPS, SUBLANES, CONV_W), F32),
            pltpu.VMEM((SUBLANES + tm, CONV_W), F32),
        ],
        compiler_params=pltpu.CompilerParams(
            dimension_semantics=("arbitrary", "arbitrary"), vmem_limit_bytes=VMEM_LIMIT),
        name="proj",
    )(x2, g_mix, w_head, w_head, w_sigm, w_head, w_small, conv_w)


def _ret_body(decays, nchunk, qk_ref, v_ref, gate_ref, gr_ref, cos_ref, sin_ref, dmat_ref, xi_ref,
              zeta_ref, gn_ref, wbr_ref, s_ref, stage_ref):
    lane = lax.broadcasted_iota(jnp.int32, (CHUNK, LANES), 1)
    first_half = (lane % RET_DK) < (RET_DK // 2)
    head_lo = lane < RET_DK

    def rot(x, cos_t, sin_t):
        partner = jnp.where(first_half, pltpu.roll(x, LANES - RET_DK // 2, 1), pltpu.roll(x, RET_DK // 2, 1))
        return x * cos_t + partner * sin_t

    qb, kb, qx, kz = {}, {}, {}, {}
    for c in range(nchunk):
        rows = slice(c * CHUNK, (c + 1) * CHUNK)
        cos_t = cos_ref[rows, :]
        sin_t = sin_ref[rows, :]
        for p in range(RET_HEADS // 2):
            q = rot(qk_ref[rows, p * LANES:(p + 1) * LANES].astype(F32), cos_t, sin_t)
            k = rot(qk_ref[rows, RET_QK + p * LANES:RET_QK + (p + 1) * LANES].astype(F32), cos_t, sin_t)
            k = k * (RET_DK ** -0.5)
            qb[c, p] = q.astype(BF16)
            kb[c, p] = k.astype(BF16)
            qx[c, p] = (q * xi_ref[p]).astype(BF16)
            kz[c, p] = (k * zeta_ref[p]).astype(BF16)

    items = [(c, h) for c in range(nchunk) for h in range(RET_HEADS)]

    def head_mask(h):
        return head_lo if h % 2 == 0 else jnp.logical_not(head_lo)

    def v_of(c, h):
        return v_ref[c * CHUNK:(c + 1) * CHUNK, h * RET_DV:(h + 1) * RET_DV]

    scores16 = {}
    for c, h in items:
        qm = jnp.where(head_mask(h), qb[c, h // 2], jnp.zeros((CHUNK, LANES), BF16))
        scores16[c, h] = (_dot_nt(qm, kb[c, h // 2]) * dmat_ref[h]).astype(BF16)
    kv = {(c, h): _dot_tn(kz[c, h // 2], v_of(c, h)) for c, h in items}

    for c in range(nchunk):
        rows = slice(c * CHUNK, (c + 1) * CHUNK)
        s = [s_ref[h] for h in range(RET_HEADS)]
        ro = []
        for h in range(RET_HEADS):
            qxm = jnp.where(head_mask(h), qx[c, h // 2], jnp.zeros((CHUNK, LANES), BF16))
            ro.append(_dot(jnp.concatenate([scores16[c, h], qxm], axis=1),
                           jnp.concatenate([v_of(c, h), s[h].astype(BF16)], axis=0)))
        for h in range(RET_HEADS):
            s_ref[h] = s[h] * decays[h] + kv[c, h]
        for h in range(RET_HEADS):
            cols = slice(h * RET_DV, (h + 1) * RET_DV)
            mu = jnp.mean(ro[h], axis=-1, keepdims=True)
            d = ro[h] - mu
            var = jnp.mean(d * d, axis=-1, keepdims=True)
            y = (d * lax.rsqrt(var + GN_EPS)) * gn_ref[:, cols] * gate_ref[rows, cols].astype(F32)
            stage_ref[rows, cols] = y.astype(BF16)

    return gr_ref[...].astype(F32) * _dot(stage_ref[...], wbr_ref[...])


def _ret_tables():
    h = np.arange(RET_HEADS, dtype=np.float64)
    gamma = 1.0 - 2.0 ** (-5.0 - h)
    log_g = np.log(gamma)
    idx = np.arange(CHUNK, dtype=np.float64)
    diff = idx[:, None] - idx[None, :]
    dmat = np.where(diff >= 0, np.exp(log_g[:, None, None] * np.maximum(diff, 0.0)[None]), 0.0)
    zeta = np.exp(log_g[:, None] * (CHUNK - 1.0 - idx)[None, :])
    xi = np.exp(log_g[:, None] * (idx + 1.0)[None, :])
    decays = tuple(float(v) for v in np.exp(log_g * CHUNK))

    def pair_table(t):
        t = t.reshape(RET_HEADS // 2, 2, CHUNK)
        return np.repeat(np.transpose(t, (0, 2, 1)), RET_DK, axis=2)

    return (jnp.asarray(dmat, F32), jnp.asarray(pair_table(xi), F32), jnp.asarray(pair_table(zeta), F32), decays)


def _rope_tables(t_len):
    half = RET_DK // 2
    inv = ROPE_BASE ** (-np.arange(0, RET_DK, 2, dtype=np.float64) / RET_DK)
    ang = np.arange(t_len, dtype=np.float64)[:, None] * inv[None, :]
    cos, sin = np.cos(ang), np.sin(ang)
    cos_t = np.concatenate([cos, cos], axis=1)
    sin_t = np.concatenate([-sin, sin], axis=1)
    assert cos_t.shape[1] == 2 * half
    reps = LANES // RET_DK
    return jnp.asarray(np.tile(cos_t, (1, reps)), F32), jnp.asarray(np.tile(sin_t, (1, reps)), F32)


INV_FULL_LEVELS = 3


def _odd_block_rows(x, b):
    return jnp.concatenate([x[(2 * i + 1) * b:(2 * i + 2) * b] for i in range(CHUNK // (2 * b))], axis=0)


def _with_odd_block_rows(x, odd, b):
    parts = []
    for i in range(CHUNK // (2 * b)):
        parts += [x[2 * i * b:(2 * i + 1) * b], odd[i * b:(i + 1) * b]]
    return jnp.concatenate(parts, axis=0)


def _inv_unit_lower_many(a_list, qmask_ref, qmask_odd_ref):
    row = lax.broadcasted_iota(jnp.int32, (CHUNK, CHUNK), 0)
    col = lax.broadcasted_iota(jnp.int32, (CHUNK, CHUNK), 1)
    eye = jnp.where(row == col, 1.0, 0.0).astype(F32)
    a16 = [a.astype(BF16) for a in a_list]
    d = [eye - a * qmask_ref[0] for a in a_list]
    level = 1
    b = 2
    while b < CHUNK:
        d16 = [x.astype(BF16) for x in d]
        if level < INV_FULL_LEVELS:
            mask = qmask_ref[level]
            t16 = [_dot(x, a).astype(BF16) for x, a in zip(d16, a16)]
            d = [x - _dot(t, y) * mask for x, t, y in zip(d, t16, d16)]
        else:
            mask = qmask_odd_ref[level - INV_FULL_LEVELS]
            t16 = [_dot(_odd_block_rows(x, b), a).astype(BF16) for x, a in zip(d16, a16)]
            d = [_with_odd_block_rows(x, _odd_block_rows(x, b) - _dot(t, y) * mask, b)
                 for x, t, y in zip(d, t16, d16)]
        level += 1
        b *= 2
    return d


def _dn_body(nchunk, mr, q_ref, k_ref, v_ref, z_ref, gd_ref, x_ref, ps_ref, alog_ref, dtb_ref,
             tri_ref, qmask_ref, qmask_odd_ref, ng_ref, wbr_ref, wo_ref, h_ref, s_ref, stage_ref):
    ps = ps_ref[...]
    beta_all = _sigmoid(ps)
    g_all = -jnp.exp(alog_ref[...]) * _softplus(ps + dtb_ref[...])

    row = lax.broadcasted_iota(jnp.int32, (CHUNK, CHUNK), 0)
    col = lax.broadcasted_iota(jnp.int32, (CHUNK, CHUNK), 1)
    causal = row >= col
    strict = row > col
    tri = tri_ref[...]

    def lanes(x, j):
        return jnp.broadcast_to(x[:, j:j + 1], (CHUNK, LANES))

    g_cum, g_cum_t, exp_g_all, exp_rest_all = [], [], [], []
    for c in range(nchunk):
        g = _dot_split_rhs(tri, g_all[c * CHUNK:(c + 1) * CHUNK])
        g_cum.append(g)
        g_cum_t.append(g.T)
        exp_g_all.append(jnp.exp(g))
        exp_rest_all.append(jnp.exp(g[CHUNK - 1:CHUNK, :] - g))

    def decay_of(c, h, mask):
        gc = lanes(g_cum[c], SMALL_DECAY0 + h)
        gr = jnp.broadcast_to(g_cum_t[c][SMALL_DECAY0 + h:SMALL_DECAY0 + h + 1, :], (CHUNK, CHUNK))
        return jnp.where(mask, jnp.exp(gc - gr), 0.0)

    def block(ref, c, h):
        return ref[c * CHUNK:(c + 1) * CHUNK, h * DN_DV:(h + 1) * DN_DV]

    items = [(c, h) for c in range(nchunk) for h in range(DN_HEADS)]

    a_list, rhs16 = [], []
    for c, h in items:
        k16 = block(k_ref, c, h)
        beta = lanes(beta_all[c * CHUNK:(c + 1) * CHUNK], SMALL_BETA0 + h)
        k_beta = k16.astype(F32) * beta
        a_list.append(_dot_nt(k_beta.astype(BF16), k16) * decay_of(c, h, strict))
        exp_g = lanes(exp_g_all[c], SMALL_DECAY0 + h)
        rhs16.append(jnp.concatenate([block(v_ref, c, h).astype(F32) * beta, k_beta * exp_g], axis=1).astype(BF16))

    minv = _inv_unit_lower_many(a_list, qmask_ref, qmask_odd_ref)
    sol = [_dot(m.astype(BF16), r) for m, r in zip(minv, rhs16)]

    for c in range(nchunk):
        qa16, kdec16, sdec = [], [], []
        for h in range(DN_HEADS):
            q16 = block(q_ref, c, h)
            k16 = block(k_ref, c, h)
            exp_g = lanes(exp_g_all[c], SMALL_DECAY0 + h)
            attn = _dot_nt(q16, k16) * decay_of(c, h, causal)
            qa16.append(jnp.concatenate([(q16.astype(F32) * exp_g).astype(BF16), attn.astype(BF16)], axis=1))
            kdec16.append((k16.astype(F32) * lanes(exp_rest_all[c], SMALL_DECAY0 + h)).astype(BF16))
            sdec.append(exp_g[CHUNK - 1:CHUNK, :])
        idx = [c * DN_HEADS + h for h in range(DN_HEADS)]
        s = [s_ref[h] for h in range(DN_HEADS)]
        s16 = [x.astype(BF16) for x in s]
        ws = [_dot(sol[i][:, DN_DV:].astype(BF16), s16[h]) for h, i in enumerate(idx)]
        vn16 = [(sol[i][:, :DN_DV] - ws[h]).astype(BF16) for h, i in enumerate(idx)]
        o = [_dot(qa16[h], jnp.concatenate([s16[h], vn16[h]], axis=0)) for h in range(DN_HEADS)]
        for h in range(DN_HEADS):
            s_ref[h] = s[h] * sdec[h] + _dot_tn(kdec16[h], vn16[h])
        for h in range(DN_HEADS):
            on = o[h] * lax.rsqrt(jnp.mean(o[h] * o[h], axis=-1, keepdims=True) + EPS) * ng_ref[...]
            stage_ref[c * CHUNK:(c + 1) * CHUNK, h * DN_DV:(h + 1) * DN_DV] = (
                on * block(z_ref, c, h).astype(F32)).astype(BF16)

    y_dn = _dot(stage_ref[...], wbr_ref[...])
    merged = mr + gd_ref[...].astype(F32) * y_dn
    h_ref[...] = x_ref[...] + _dot(merged.astype(BF16), wo_ref[...])


def _dn_tables():
    idx = np.arange(CHUNK)
    r, c = idx[:, None], idx[None, :]
    qmasks, qmasks_odd = [], []
    b = 1
    while b < CHUNK:
        m = ((r // (2 * b)) == (c // (2 * b))) & ((r // b) % 2 == 1) & ((c // b) % 2 == 0)
        if len(qmasks) < INV_FULL_LEVELS:
            qmasks.append(m)
        else:
            qmasks_odd.append(m[(idx // b) % 2 == 1])
        b *= 2
    tri = (r >= c).astype(np.float32)
    return (jnp.asarray(tri, BF16), jnp.asarray(np.stack(qmasks), F32), jnp.asarray(np.stack(qmasks_odd), F32))


def _mixer_kernel(decays, nchunk, n_ret_in, n_dn_in, *refs):
    ret_in = refs[:n_ret_in]
    dn_in = refs[n_ret_in:n_ret_in + n_dn_in]
    h_ref, s_ret_ref, stage_ret_ref, s_dn_ref, stage_dn_ref = refs[n_ret_in + n_dn_in:]

    @pl.when(pl.program_id(1) == 0)
    def _():
        s_ret_ref[...] = jnp.zeros_like(s_ret_ref)
        s_dn_ref[...] = jnp.zeros_like(s_dn_ref)

    mr = _ret_body(decays, nchunk, *ret_in, s_ret_ref, stage_ret_ref)
    _dn_body(nchunk, mr, *dn_in, h_ref, s_dn_ref, stage_dn_ref)


def _mixer_call(p_plain, p_silu, p_sigm, p_conv, p_small, x2, cos_t, sin_t, ret_norm_g, w_ret_br, alog_row, dtb_row,
                dn_norm_g, w_dn_br, w_o, batch, t_len, tt):
    m = batch * t_len
    nt = t_len // tt
    dmat, xi, zeta, decays = _ret_tables()
    tri, qmask, qmask_odd = _dn_tables()
    row = lambda cb: pl.BlockSpec((tt, COLBLK), lambda b, t: (b * nt + t, cb))
    tok = lambda n: pl.BlockSpec((tt, n), lambda b, t: (b * nt + t, 0))
    pos = pl.BlockSpec((tt, LANES), lambda b, t: (t, 0))
    const = lambda shape: pl.BlockSpec(shape, lambda b, t: (0,) * len(shape))
    ret_specs = [
        row(0), row(1), row(0), row(0), pos, pos,
        const((RET_HEADS, CHUNK, CHUNK)), const((RET_HEADS // 2, CHUNK, LANES)), const((RET_HEADS // 2, CHUNK, LANES)),
        const((1, RET_V)), const((RET_V, D_MODEL)),
    ]
    ret_args = [p_plain, p_plain, p_silu, p_sigm, cos_t, sin_t, dmat, xi, zeta, ret_norm_g, w_ret_br]
    dn_specs = [
        row(0), row(1), row(2), row(1), row(1), tok(D_MODEL), tok(LANES),
        const((1, LANES)), const((1, LANES)), const((CHUNK, CHUNK)), const(qmask.shape), const(qmask_odd.shape),
        const((1, DN_DV)), const((DN_V, D_MODEL)), const((D_MODEL, D_MODEL)),
    ]
    dn_args = [p_conv, p_conv, p_conv, p_silu, p_sigm, x2, p_small, alog_row, dtb_row, tri, qmask, qmask_odd,
               dn_norm_g, w_dn_br, w_o]
    return pl.pallas_call(
        functools.partial(_mixer_kernel, decays, tt // CHUNK, len(ret_specs), len(dn_specs)),
        grid=(batch, nt),
        in_specs=ret_specs + dn_specs,
        out_specs=tok(D_MODEL),
        out_shape=jax.ShapeDtypeStruct((m, D_MODEL), F32),
        scratch_shapes=[
            pltpu.VMEM((RET_HEADS, LANES, RET_DV), F32),
            pltpu.VMEM((tt, RET_V), BF16),
            pltpu.VMEM((DN_HEADS, DN_DK, DN_DV), F32),
            pltpu.VMEM((tt, DN_V), BF16),
        ],
        compiler_params=pltpu.CompilerParams(
            dimension_semantics=("arbitrary", "arbitrary"), vmem_limit_bytes=VMEM_LIMIT),
        name="mixer",
    )(*ret_args, *dn_args)


FFN_COLS = 256
FFN_STEPS = D_FF // FFN_COLS


def _ffn_kernel(h_ref, g_ref, wup_ref, cw_ref, cb_ref, wdn_ref, gf_ref, o_ref, carry_ref, up_ref, act_ref):
    @pl.when(pl.program_id(1) == 0)
    def _():
        carry_ref[...] = jnp.zeros_like(carry_ref)

    tm = h_ref.shape[0]
    h = h_ref[...]
    u = ((h * lax.rsqrt(jnp.mean(h * h, axis=-1, keepdims=True) + EPS)) * g_ref[...]).astype(BF16)

    def conv_branch(c0, buf):
        cols = slice(c0, c0 + FFN_COLS)
        up = _dot(u, wup_ref[:, cols])
        buf[0:SUBLANES, :] = carry_ref[:, cols]
        buf[SUBLANES:SUBLANES + tm, :] = up
        carry_ref[:, cols] = up[tm - SUBLANES:]
        w = cw_ref[:, cols]
        y = up * w[FFN_CONV - 1:FFN_CONV] + cb_ref[:, cols]
        for s in range(1, FFN_CONV):
            y = y + buf[SUBLANES - s:SUBLANES - s + tm, :] * w[FFN_CONV - 1 - s:FFN_CONV - s]
        return y

    for j in range(FFN_STEPS):
        a = conv_branch(j * FFN_COLS, up_ref.at[j % 2, 0])
        b = conv_branch(D_FF + j * FFN_COLS, up_ref.at[j % 2, 1])
        act_ref[:, j * FFN_COLS:(j + 1) * FFN_COLS] = (_silu(a) * b).astype(BF16)

    acc = h + _dot(act_ref[...], wdn_ref[...])
    o_ref[...] = (acc * lax.rsqrt(jnp.mean(acc * acc, axis=-1, keepdims=True) + EPS)) * gf_ref[...]


def _ffn_call(h2, g_ffn, w_up, conv_w, conv_b, w_down, g_final, batch, t_len, tm):
    m = batch * t_len
    nt = t_len // tm
    tok = pl.BlockSpec((tm, D_MODEL), lambda b, t: (b * nt + t, 0))
    const2 = lambda shape: pl.BlockSpec(shape, lambda b, t: (0, 0))
    resident = lambda shape: pl.BlockSpec(shape, lambda b, t: (0, 0), pipeline_mode=pl.Buffered(1))
    return pl.pallas_call(
        _ffn_kernel,
        grid=(batch, nt),
        in_specs=[
            tok, const2((1, D_MODEL)),
            resident((D_MODEL, 2 * D_FF)),
            const2((FFN_CONV, 2 * D_FF)), const2((1, 2 * D_FF)),
            resident((D_FF, D_MODEL)),
            const2((1, D_MODEL)),
        ],
        out_specs=tok,
        out_shape=jax.ShapeDtypeStruct((m, D_MODEL), F32),
        scratch_shapes=[
            pltpu.VMEM((SUBLANES, 2 * D_FF), F32),
            pltpu.VMEM((2, 2, SUBLANES + tm, FFN_COLS), F32),
            pltpu.VMEM((tm, D_FF), BF16),
        ],
        compiler_params=pltpu.CompilerParams(
            dimension_semantics=("arbitrary", "arbitrary"), vmem_limit_bytes=VMEM_LIMIT),
        name="convffn",
    )(h2, g_ffn, w_up, conv_w, conv_b, w_down, g_final)


def _pick_tile(t_len, want):
    tile = min(want, t_len)
    assert t_len % tile == 0 and tile % CHUNK == 0
    return tile


PROJ_TILE = 1024
MIXER_TILE = 512
FFN_TILE = 512


def kernel(x, g_mix, w_in, ret_norm_g, dn_conv_w, dn_a_log, dn_dt_bias, dn_norm_g, w_ret_br, w_dn_br, w_o, g_ffn,
           w_up, ffn_conv_w, ffn_conv_b, w_down, g_final):
    batch, t_len, d_model = x.shape
    assert d_model == D_MODEL and g_mix.shape[0] == 1 and t_len % CHUNK == 0
    m = batch * t_len
    x2 = x.astype(F32).reshape(m, D_MODEL)

    w_t = jnp.swapaxes(w_in[0], 0, 1)
    w_head = _cast_call(w_t, COL_SMALL)
    w_small, w_sigm = _cast_tail_call(w_t)

    conv_w = jnp.transpose(dn_conv_w[0].reshape(SHORT_CONV, PROJ_STEPS, CONV_W), (1, 0, 2))
    p_plain, p_silu, p_sigm, p_conv, p_small = _proj_call(x2, g_mix, w_head, w_sigm, w_small, conv_w, t_len,
                                                          _pick_tile(t_len, PROJ_TILE))

    cos_t, sin_t = _rope_tables(t_len)
    alog_row = jnp.zeros((1, LANES), F32).at[0, SMALL_DECAY0:SMALL_DECAY0 + DN_HEADS].set(dn_a_log[0])
    dtb_row = jnp.zeros((1, LANES), F32).at[0, SMALL_DECAY0:SMALL_DECAY0 + DN_HEADS].set(dn_dt_bias[0])
    h = _mixer_call(p_plain, p_silu, p_sigm, p_conv, p_small, x2, cos_t, sin_t, ret_norm_g, w_ret_br[0].astype(BF16),
                    alog_row, dtb_row, dn_norm_g, w_dn_br[0].astype(BF16), w_o[0].astype(BF16), batch, t_len,
                    _pick_tile(t_len, MIXER_TILE))

    out = _ffn_call(h, g_ffn, w_up[0].astype(BF16), ffn_conv_w[0], ffn_conv_b, w_down[0].astype(BF16),
                    g_final.reshape(1, D_MODEL), batch, t_len, _pick_tile(t_len, FFN_TILE))
    return out.reshape(batch, t_len, D_MODEL).astype(x.dtype)
```

```python
import functools

import numpy as np
import jax
import jax.numpy as jnp
from jax import lax
from jax.experimental import pallas as pl
from jax.experimental.pallas import tpu as pltpu

D_MODEL = 1024
RET_HEADS = 8
RET_DK = 64
RET_DV = 128
DN_HEADS = 8
DN_DK = 128
DN_DV = 128
CHUNK = 128
SHORT_CONV = 4
FFN_CONV = 3
D_FF = 2816
ROPE_BASE = 10000.0
EPS = 1e-6
GN_EPS = 1e-5

RET_QK = RET_HEADS * RET_DK
RET_V = RET_HEADS * RET_DV
DN_QK = DN_HEADS * DN_DK
DN_V = DN_HEADS * DN_DV

LANES = 128
SUBLANES = 8
COLBLK = 1024
SMALL_BETA0 = 0
SMALL_DECAY0 = 8

VMEM_LIMIT = 56 * 1024 * 1024

F32 = jnp.float32
BF16 = jnp.bfloat16


def _dot(a, b):
    return jnp.dot(a, b, preferred_element_type=F32)


def _dot_nt(a, b):
    return lax.dot_general(a, b, (((1,), (1,)), ((), ())), preferred_element_type=F32)


def _dot_tn(a, b):
    return lax.dot_general(a, b, (((0,), (0,)), ((), ())), preferred_element_type=F32)


def _split3(x):
    hi = x.astype(BF16)
    r1 = x - hi.astype(F32)
    mid = r1.astype(BF16)
    lo = (r1 - mid.astype(F32)).astype(BF16)
    return hi, mid, lo


def _dot_split_rhs(a, x):
    hi, mid, lo = _split3(x)
    return _dot(a, hi) + _dot(a, mid) + _dot(a, lo)


def _sigmoid(x):
    return 0.5 + 0.5 * jnp.tanh(0.5 * x)


def _silu(x):
    h = 0.5 * x
    return h + h * jnp.tanh(h)


def _softplus(x):
    return jnp.maximum(x, 0.0) + jnp.log(1.0 + jnp.exp(-jnp.abs(x)))


PROJ_STEPS = 4
PLAIN_W = (2 * RET_QK + RET_V) // PROJ_STEPS
SILU_W = (RET_V + DN_V) // PROJ_STEPS
SIGM_W = 2 * D_MODEL // PROJ_STEPS
CONV_W = (2 * DN_QK + DN_V) // PROJ_STEPS
COL_RGATE = 2 * RET_QK + RET_V
COL_CONV = COL_RGATE + RET_V
COL_DZ = COL_CONV + 2 * DN_QK + DN_V
COL_SMALL = COL_DZ + DN_V
COL_SIGM = COL_SMALL + 2 * DN_HEADS
CONV_GROUPS = CONV_W // DN_DK
PROJ_ROWS = 256


def _proj_kernel(tiles_per_seq, x_ref, g_ref, w1_ref, w2a_ref, w2b_ref, w3_ref, ws_ref, convw_ref,
                 o1_ref, o2a_ref, o2b_ref, o3_ref, ps_ref, u_ref, carry_ref, buf_ref):
    i = pl.program_id(0)
    s = pl.program_id(1)
    tm = x_ref.shape[0]

    @pl.when(s == 0)
    def _():
        x = x_ref[...]
        r = lax.rsqrt(jnp.mean(x * x, axis=-1, keepdims=True) + EPS)
        u = ((x * r) * g_ref[...]).astype(BF16)
        u_ref[...] = u
        ps_ref[...] = _dot(u, ws_ref[...])

    @pl.when(jnp.logical_and(i == 0, s == 0))
    def _():
        carry_ref[...] = jnp.zeros_like(carry_ref)

    prev = carry_ref[s]
    buf_ref[0:SUBLANES, :] = jnp.where(i % tiles_per_seq == 0, jnp.zeros_like(prev), prev)
    w = convw_ref[s]

    def products(rb):
        r0 = rb * PROJ_ROWS
        u = u_ref[r0:r0 + PROJ_ROWS, :]
        acc = _dot(u, w3_ref[...])
        buf_ref[SUBLANES + r0:SUBLANES + r0 + PROJ_ROWS, :] = acc
        return acc, _dot(u, w2a_ref[...]), _dot(u, w2b_ref[...]), _dot(u, w1_ref[...])

    def elementwise(rb, acc, ga, gb, p):
        r0 = rb * PROJ_ROWS
        rows = slice(r0, r0 + PROJ_ROWS)
        y = acc * w[SHORT_CONV - 1:SHORT_CONV]
        for t in range(1, SHORT_CONV):
            y = y + buf_ref[SUBLANES + r0 - t:SUBLANES + r0 - t + PROJ_ROWS, :] * w[SHORT_CONV - 1 - t:SHORT_CONV - t]
        y = _silu(y)
        for l in range(CONV_GROUPS):
            head_group = s * CONV_GROUPS + l
            cols = slice(l * DN_DK, (l + 1) * DN_DK)
            yl = y[:, cols]
            r = lax.rsqrt(jnp.sum(yl * yl, axis=-1, keepdims=True) + EPS)
            r = r * jnp.where(head_group < DN_HEADS, DN_DK ** -0.5, 1.0)
            scale = jnp.where(head_group < 2 * DN_HEADS, r, 1.0)
            o3_ref[rows, cols] = (yl * scale).astype(BF16)
        o2a_ref[rows, :] = _silu(ga).astype(BF16)
        o2b_ref[rows, :] = _sigmoid(gb).astype(BF16)
        o1_ref[rows, :] = p.astype(BF16)

    nrb = tm // PROJ_ROWS
    pending = products(0)
    for rb in range(1, nrb):
        nxt = products(rb)
        elementwise(rb - 1, *pending)
        pending = nxt
    elementwise(nrb - 1, *pending)
    carry_ref[s] = buf_ref[tm:tm + SUBLANES, :]


def _cast_kernel(w_ref, o_ref):
    o_ref[...] = w_ref[...].T.astype(BF16)


def _cast_call(w_t, ncols):
    assert ncols % COLBLK == 0
    return pl.pallas_call(
        _cast_kernel,
        grid=(ncols // COLBLK,),
        in_specs=[pl.BlockSpec((COLBLK, D_MODEL), lambda j: (j, 0))],
        out_specs=pl.BlockSpec((D_MODEL, COLBLK), lambda j: (0, j)),
        out_shape=jax.ShapeDtypeStruct((D_MODEL, ncols), BF16),
        compiler_params=pltpu.CompilerParams(dimension_semantics=("arbitrary",), vmem_limit_bytes=VMEM_LIMIT),
        name="cast_w_in",
    )(w_t)


def _cast_tail_kernel(ws_ref, wg_ref, os_ref, og_ref):
    rows = lax.broadcasted_iota(jnp.int32, ws_ref.shape, 0)
    small = jnp.where(rows < COL_SIGM - COL_SMALL, ws_ref[...], 0.0)
    os_ref[...] = small.T.astype(BF16)
    og_ref[...] = wg_ref[...].T.astype(BF16)


def _cast_tail_call(w_t):
    nblk = (w_t.shape[0] - COL_SIGM) // COLBLK
    return pl.pallas_call(
        _cast_tail_kernel,
        grid=(nblk,),
        in_specs=[
            pl.BlockSpec((pl.Element(LANES), pl.Element(D_MODEL)), lambda j: (COL_SMALL, 0)),
            pl.BlockSpec((pl.Element(COLBLK), pl.Element(D_MODEL)), lambda j: (pl.multiple_of(COL_SIGM + j * COLBLK, SUBLANES), 0)),
        ],
        out_specs=[
            pl.BlockSpec((D_MODEL, LANES), lambda j: (0, 0)),
            pl.BlockSpec((D_MODEL, COLBLK), lambda j: (0, j)),
        ],
        out_shape=[
            jax.ShapeDtypeStruct((D_MODEL, LANES), BF16),
            jax.ShapeDtypeStruct((D_MODEL, nblk * COLBLK), BF16),
        ],
        compiler_params=pltpu.CompilerParams(dimension_semantics=("arbitrary",), vmem_limit_bytes=VMEM_LIMIT),
        name="cast_w_tail",
    )(w_t, w_t)


def _proj_call(x2, g_mix, w_head, w_sigm, w_small, conv_w, t_len, tm):
    m = x2.shape[0]
    assert t_len % tm == 0 and w_head.shape[1] == COL_SMALL
    half = PROJ_STEPS // 2
    ospec = lambda n: pl.BlockSpec((tm, n), lambda i, s: (i, s))
    return pl.pallas_call(
        functools.partial(_proj_kernel, t_len // tm),
        grid=(m // tm, PROJ_STEPS),
        in_specs=[
            pl.BlockSpec((tm, D_MODEL), lambda i, s: (i, 0)),
            pl.BlockSpec((1, D_MODEL), lambda i, s: (0, 0)),
            pl.BlockSpec((D_MODEL, PLAIN_W), lambda i, s: (0, s)),
            pl.BlockSpec((D_MODEL, SILU_W),
                         lambda i, s: (0, jnp.where(s < half, COL_RGATE // SILU_W + s, COL_DZ // SILU_W + s - half))),
            pl.BlockSpec((D_MODEL, SIGM_W), lambda i, s: (0, s)),
            pl.BlockSpec((D_MODEL, CONV_W), lambda i, s: (0, COL_CONV // CONV_W + s)),
            pl.BlockSpec((D_MODEL, LANES), lambda i, s: (0, 0)),
            pl.BlockSpec((PROJ_STEPS, SHORT_CONV, CONV_W), lambda i, s: (0, 0, 0)),
        ],
        out_specs=[
            ospec(PLAIN_W), ospec(SILU_W), ospec(SIGM_W), ospec(CONV_W),
            pl.BlockSpec((tm, LANES), lambda i, s: (i, 0)),
        ],
        out_shape=[
            jax.ShapeDtypeStruct((m, PROJ_STEPS * PLAIN_W), BF16),
            jax.ShapeDtypeStruct((m, PROJ_STEPS * SILU_W), BF16),
            jax.ShapeDtypeStruct((m, PROJ_STEPS * SIGM_W), BF16),
            jax.ShapeDtypeStruct((m, PROJ_STEPS * CONV_W), BF16),
            jax.ShapeDtypeStruct((m, LANES), F32),
        ],
        scratch_shapes=[
            pltpu.VMEM((tm, D_MODEL), BF16),
            pltpu.VMEM((PROJ_STEPS, SUBLANES, CONV_W), F32),
            pltpu.VMEM((SUBLANES + tm, CONV_W), F32),
        ],
        compiler_params=pltpu.CompilerParams(
            dimension_semantics=("arbitrary", "arbitrary"), vmem_limit_bytes=VMEM_LIMIT),
        name="proj",
    )(x2, g_mix, w_head, w_head, w_sigm, w_head, w_small, conv_w)


def _ret_body(decays, nchunk, qk_ref, v_ref, gate_ref, gr_ref, cos_ref, sin_ref, dmat_ref, xi_ref,
              zeta_ref, gn_ref, wbr_ref, s_ref, stage_ref):
    lane = lax.broadcasted_iota(jnp.int32, (CHUNK, LANES), 1)
    first_half = (lane % RET_DK) < (RET_DK // 2)
    head_lo = lane < RET_DK

    def rot(x, cos_t, sin_t):
        partner = jnp.where(first_half, pltpu.roll(x, LANES - RET_DK // 2, 1), pltpu.roll(x, RET_DK // 2, 1))
        return x * cos_t + partner * sin_t

    qb, kb, qx, kz = {}, {}, {}, {}
    for c in range(nchunk):
        rows = slice(c * CHUNK, (c + 1) * CHUNK)
        cos_t = cos_ref[rows, :]
        sin_t = sin_ref[rows, :]
        for p in range(RET_HEADS // 2):
            q = rot(qk_ref[rows, p * LANES:(p + 1) * LANES].astype(F32), cos_t, sin_t)
            k = rot(qk_ref[rows, RET_QK + p * LANES:RET_QK + (p + 1) * LANES].astype(F32), cos_t, sin_t)
            k = k * (RET_DK ** -0.5)
            qb[c, p] = q.astype(BF16)
            kb[c, p] = k.astype(BF16)
            qx[c, p] = (q * xi_ref[p]).astype(BF16)
            kz[c, p] = (k * zeta_ref[p]).astype(BF16)

    items = [(c, h) for c in range(nchunk) for h in range(RET_HEADS)]

    def head_mask(h):
        return head_lo if h % 2 == 0 else jnp.logical_not(head_lo)

    def v_of(c, h):
        return v_ref[c * CHUNK:(c + 1) * CHUNK, h * RET_DV:(h + 1) * RET_DV]

    scores16 = {}
    for c, h in items:
        qm = jnp.where(head_mask(h), qb[c, h // 2], jnp.zeros((CHUNK, LANES), BF16))
        scores16[c, h] = (_dot_nt(qm, kb[c, h // 2]) * dmat_ref[h]).astype(BF16)
    kv = {(c, h): _dot_tn(kz[c, h // 2], v_of(c, h)) for c, h in items}

    for c in range(nchunk):
        rows = slice(c * CHUNK, (c + 1) * CHUNK)
        s = [s_ref[h] for h in range(RET_HEADS)]
        ro = []
        for h in range(RET_HEADS):
            qxm = jnp.where(head_mask(h), qx[c, h // 2], jnp.zeros((CHUNK, LANES), BF16))
            ro.append(_dot(jnp.concatenate([scores16[c, h], qxm], axis=1),
                           jnp.concatenate([v_of(c, h), s[h].astype(BF16)], axis=0)))
        for h in range(RET_HEADS):
            s_ref[h] = s[h] * decays[h] + kv[c, h]
        for h in range(RET_HEADS):
            cols = slice(h * RET_DV, (h + 1) * RET_DV)
            mu = jnp.mean(ro[h], axis=-1, keepdims=True)
            d = ro[h] - mu
            var = jnp.mean(d * d, axis=-1, keepdims=True)
            y = (d * lax.rsqrt(var + GN_EPS)) * gn_ref[:, cols] * gate_ref[rows, cols].astype(F32)
            stage_ref[rows, cols] = y.astype(BF16)

    return gr_ref[...].astype(F32) * _dot(stage_ref[...], wbr_ref[...])


def _ret_tables():
    h = np.arange(RET_HEADS, dtype=np.float64)
    gamma = 1.0 - 2.0 ** (-5.0 - h)
    log_g = np.log(gamma)
    idx = np.arange(CHUNK, dtype=np.float64)
    diff = idx[:, None] - idx[None, :]
    dmat = np.where(diff >= 0, np.exp(log_g[:, None, None] * np.maximum(diff, 0.0)[None]), 0.0)
    zeta = np.exp(log_g[:, None] * (CHUNK - 1.0 - idx)[None, :])
    xi = np.exp(log_g[:, None] * (idx + 1.0)[None, :])
    decays = tuple(float(v) for v in np.exp(log_g * CHUNK))

    def pair_table(t):
        t = t.reshape(RET_HEADS // 2, 2, CHUNK)
        return np.repeat(np.transpose(t, (0, 2, 1)), RET_DK, axis=2)

    return (jnp.asarray(dmat, F32), jnp.asarray(pair_table(xi), F32), jnp.asarray(pair_table(zeta), F32), decays)


def _rope_tables(t_len):
    half = RET_DK // 2
    inv = ROPE_BASE ** (-np.arange(0, RET_DK, 2, dtype=np.float64) / RET_DK)
    ang = np.arange(t_len, dtype=np.float64)[:, None] * inv[None, :]
    cos, sin = np.cos(ang), np.sin(ang)
    cos_t = np.concatenate([cos, cos], axis=1)
    sin_t = np.concatenate([-sin, sin], axis=1)
    assert cos_t.shape[1] == 2 * half
    reps = LANES // RET_DK
    return jnp.asarray(np.tile(cos_t, (1, reps)), F32), jnp.asarray(np.tile(sin_t, (1, reps)), F32)


INV_FULL_LEVELS = 3


def _odd_block_rows(x, b):
    return jnp.concatenate([x[(2 * i + 1) * b:(2 * i + 2) * b] for i in range(CHUNK // (2 * b))], axis=0)


def _with_odd_block_rows(x, odd, b):
    parts = []
    for i in range(CHUNK // (2 * b)):
        parts += [x[2 * i * b:(2 * i + 1) * b], odd[i * b:(i + 1) * b]]
    return jnp.concatenate(parts, axis=0)


def _inv_unit_lower_many(a_list, qmask_ref, qmask_odd_ref):
    row = lax.broadcasted_iota(jnp.int32, (CHUNK, CHUNK), 0)
    col = lax.broadcasted_iota(jnp.int32, (CHUNK, CHUNK), 1)
    eye = jnp.where(row == col, 1.0, 0.0).astype(F32)
    a16 = [a.astype(BF16) for a in a_list]
    d = [eye - a * qmask_ref[0] for a in a_list]
    level = 1
    b = 2
    while b < CHUNK:
        d16 = [x.astype(BF16) for x in d]
        if level < INV_FULL_LEVELS:
            mask = qmask_ref[level]
            t16 = [_dot(x, a).astype(BF16) for x, a in zip(d16, a16)]
            d = [x - _dot(t, y) * mask for x, t, y in zip(d, t16, d16)]
        else:
            mask = qmask_odd_ref[level - INV_FULL_LEVELS]
            t16 = [_dot(_odd_block_rows(x, b), a).astype(BF16) for x, a in zip(d16, a16)]
            d = [_with_odd_block_rows(x, _odd_block_rows(x, b) - _dot(t, y) * mask, b)
                 for x, t, y in zip(d, t16, d16)]
        level += 1
        b *= 2
    return d


def _dn_body(nchunk, mr, q_ref, k_ref, v_ref, z_ref, gd_ref, x_ref, ps_ref, alog_ref, dtb_ref,
             tri_ref, qmask_ref, qmask_odd_ref, ng_ref, wbr_ref, wo_ref, h_ref, s_ref, stage_ref):
    ps = ps_ref[...]
    beta_all = _sigmoid(ps)
    g_all = -jnp.exp(alog_ref[...]) * _softplus(ps + dtb_ref[...])

    row = lax.broadcasted_iota(jnp.int32, (CHUNK, CHUNK), 0)
    col = lax.broadcasted_iota(jnp.int32, (CHUNK, CHUNK), 1)
    causal = row >= col
    strict = row > col
    tri = tri_ref[...]

    def lanes(x, j):
        return jnp.broadcast_to(x[:, j:j + 1], (CHUNK, LANES))

    g_cum, g_cum_t, exp_g_all, exp_rest_all = [], [], [], []
    for c in range(nchunk):
        g = _dot_split_rhs(tri, g_all[c * CHUNK:(c + 1) * CHUNK])
        g_cum.append(g)
        g_cum_t.append(g.T)
        exp_g_all.append(jnp.exp(g))
        exp_rest_all.append(jnp.exp(g[CHUNK - 1:CHUNK, :] - g))

    def decay_of(c, h, mask):
        gc = lanes(g_cum[c], SMALL_DECAY0 + h)
        gr = jnp.broadcast_to(g_cum_t[c][SMALL_DECAY0 + h:SMALL_DECAY0 + h + 1, :], (CHUNK, CHUNK))
        return jnp.where(mask, jnp.exp(gc - gr), 0.0)

    def block(ref, c, h):
        return ref[c * CHUNK:(c + 1) * CHUNK, h * DN_DV:(h + 1) * DN_DV]

    items = [(c, h) for c in range(nchunk) for h in range(DN_HEADS)]

    a_list, rhs16 = [], []
    for c, h in items:
        k16 = block(k_ref, c, h)
        beta = lanes(beta_all[c * CHUNK:(c + 1) * CHUNK], SMALL_BETA0 + h)
        k_beta = k16.astype(F32) * beta
        a_list.append(_dot_nt(k_beta.astype(BF16), k16) * decay_of(c, h, strict))
        exp_g = lanes(exp_g_all[c], SMALL_DECAY0 + h)
        rhs16.append(jnp.concatenate([block(v_ref, c, h).astype(F32) * beta, k_beta * exp_g], axis=1).astype(BF16))

    minv = _inv_unit_lower_many(a_list, qmask_ref, qmask_odd_ref)
    sol = [_dot(m.astype(BF16), r) for m, r in zip(minv, rhs16)]

    for c in range(nchunk):
        qa16, kdec16, sdec = [], [], []
        for h in range(DN_HEADS):
            q16 = block(q_ref, c, h)
            k16 = block(k_ref, c, h)
            exp_g = lanes(exp_g_all[c], SMALL_DECAY0 + h)
            attn = _dot_nt(q16, k16) * decay_of(c, h, causal)
            qa16.append(jnp.concatenate([(q16.astype(F32) * exp_g).astype(BF16), attn.astype(BF16)], axis=1))
            kdec16.append((k16.astype(F32) * lanes(exp_rest_all[c], SMALL_DECAY0 + h)).astype(BF16))
            sdec.append(exp_g[CHUNK - 1:CHUNK, :])
        idx = [c * DN_HEADS + h for h in range(DN_HEADS)]
        s = [s_ref[h] for h in range(DN_HEADS)]
        s16 = [x.astype(BF16) for x in s]
        ws = [_dot(sol[i][:, DN_DV:].astype(BF16), s16[h]) for h, i in enumerate(idx)]
        vn16 = [(sol[i][:, :DN_DV] - ws[h]).astype(BF16) for h, i in enumerate(idx)]
        o = [_dot(qa16[h], jnp.concatenate([s16[h], vn16[h]], axis=0)) for h in range(DN_HEADS)]
        for h in range(DN_HEADS):
            s_ref[h] = s[h] * sdec[h] + _dot_tn(kdec16[h], vn16[h])
        for h in range(DN_HEADS):
            on = o[h] * lax.rsqrt(jnp.mean(o[h] * o[h], axis=-1, keepdims=True) + EPS) * ng_ref[...]
            stage_ref[c * CHUNK:(c + 1) * CHUNK, h * DN_DV:(h + 1) * DN_DV] = (
                on * block(z_ref, c, h).astype(F32)).astype(BF16)

    y_dn = _dot(stage_ref[...], wbr_ref[...])
    merged = mr + gd_ref[...].astype(F32) * y_dn
    h_ref[...] = x_ref[...] + _dot(merged.astype(BF16), wo_ref[...])


def _dn_tables():
    idx = np.arange(CHUNK)
    r, c = idx[:, None], idx[None, :]
    qmasks, qmasks_odd = [], []
    b = 1
    while b < CHUNK:
        m = ((r // (2 * b)) == (c // (2 * b))) & ((r // b) % 2 == 1) & ((c // b) % 2 == 0)
        if len(qmasks) < INV_FULL_LEVELS:
            qmasks.append(m)
        else:
            qmasks_odd.append(m[(idx // b) % 2 == 1])
        b *= 2
    tri = (r >= c).astype(np.float32)
    return (jnp.asarray(tri, BF16), jnp.asarray(np.stack(qmasks), F32), jnp.asarray(np.stack(qmasks_odd), F32))


def _mixer_kernel(decays, nchunk, plain_ref, silu_ref, sigm_ref, conv_ref, x_ref, ps_ref, cos_ref, sin_ref,
                  dmat_ref, xi_ref, zeta_ref, gn_ref, wrb_ref, alog_ref, dtb_ref, tri_ref, qmask_ref, qmask_odd_ref,
                  ng_ref, wdb_ref, wo_ref, h_ref, s_ret_ref, stage_ret_ref, s_dn_ref, stage_dn_ref):
    blk = lambda ref, j: ref.at[:, j * COLBLK:(j + 1) * COLBLK]

    @pl.when(pl.program_id(1) == 0)
    def _():
        s_ret_ref[...] = jnp.zeros_like(s_ret_ref)
        s_dn_ref[...] = jnp.zeros_like(s_dn_ref)

    mr = _ret_body(decays, nchunk, blk(plain_ref, 0), blk(plain_ref, 1), blk(silu_ref, 0), blk(sigm_ref, 0),
                   cos_ref, sin_ref, dmat_ref, xi_ref, zeta_ref, gn_ref, wrb_ref, s_ret_ref, stage_ret_ref)
    _dn_body(nchunk, mr, blk(conv_ref, 0), blk(conv_ref, 1), blk(conv_ref, 2), blk(silu_ref, 1), blk(sigm_ref, 1),
             x_ref, ps_ref, alog_ref, dtb_ref, tri_ref, qmask_ref, qmask_odd_ref, ng_ref, wdb_ref, wo_ref,
             h_ref, s_dn_ref, stage_dn_ref)


def _mixer_call(p_plain, p_silu, p_sigm, p_conv, p_small, x2, cos_t, sin_t, ret_norm_g, w_ret_br, alog_row, dtb_row,
                dn_norm_g, w_dn_br, w_o, batch, t_len, tt):
    m = batch * t_len
    nt = t_len // tt
    dmat, xi, zeta, decays = _ret_tables()
    tri, qmask, qmask_odd = _dn_tables()
    tok = lambda a: pl.BlockSpec((tt, a.shape[1]), lambda b, t: (b * nt + t, 0))
    pos = pl.BlockSpec((tt, LANES), lambda b, t: (t, 0))
    const = lambda a: pl.BlockSpec(a.shape, lambda b, t: (0,) * a.ndim)
    tile_args = [p_plain, p_silu, p_sigm, p_conv, x2, p_small]
    shared_args = [dmat, xi, zeta, ret_norm_g, w_ret_br, alog_row, dtb_row, tri, qmask, qmask_odd, dn_norm_g,
                   w_dn_br, w_o]
    return pl.pallas_call(
        functools.partial(_mixer_kernel, decays, tt // CHUNK),
        grid=(batch, nt),
        in_specs=[tok(a) for a in tile_args] + [pos, pos] + [const(a) for a in shared_args],
        out_specs=pl.BlockSpec((tt, D_MODEL), lambda b, t: (b * nt + t, 0)),
        out_shape=jax.ShapeDtypeStruct((m, D_MODEL), F32),
        scratch_shapes=[
            pltpu.VMEM((RET_HEADS, LANES, RET_DV), F32),
            pltpu.VMEM((tt, RET_V), BF16),
            pltpu.VMEM((DN_HEADS, DN_DK, DN_DV), F32),
            pltpu.VMEM((tt, DN_V), BF16),
        ],
        compiler_params=pltpu.CompilerParams(
            dimension_semantics=("arbitrary", "arbitrary"), vmem_limit_bytes=VMEM_LIMIT),
        name="mixer",
    )(*tile_args, cos_t, sin_t, *shared_args)


FFN_COLS = 256
FFN_STEPS = D_FF // FFN_COLS


def _ffn_kernel(h_ref, g_ref, wup_ref, cw_ref, cb_ref, wdn_ref, gf_ref, o_ref, carry_ref, up_ref, act_ref):
    @pl.when(pl.program_id(1) == 0)
    def _():
        carry_ref[...] = jnp.zeros_like(carry_ref)

    tm = h_ref.shape[0]
    h = h_ref[...]
    u = ((h * lax.rsqrt(jnp.mean(h * h, axis=-1, keepdims=True) + EPS)) * g_ref[...]).astype(BF16)

    def conv_branch(c0, buf):
        cols = slice(c0, c0 + FFN_COLS)
        up = _dot(u, wup_ref[:, cols])
        buf[0:SUBLANES, :] = carry_ref[:, cols]
        buf[SUBLANES:SUBLANES + tm, :] = up
        carry_ref[:, cols] = up[tm - SUBLANES:]
        w = cw_ref[:, cols]
        y = up * w[FFN_CONV - 1:FFN_CONV] + cb_ref[:, cols]
        for s in range(1, FFN_CONV):
            y = y + buf[SUBLANES - s:SUBLANES - s + tm, :] * w[FFN_CONV - 1 - s:FFN_CONV - s]
        return y

    for j in range(FFN_STEPS):
        a = conv_branch(j * FFN_COLS, up_ref.at[j % 2, 0])
        b = conv_branch(D_FF + j * FFN_COLS, up_ref.at[j % 2, 1])
        act_ref[:, j * FFN_COLS:(j + 1) * FFN_COLS] = (_silu(a) * b).astype(BF16)

    acc = h + _dot(act_ref[...], wdn_ref[...])
    o_ref[...] = (acc * lax.rsqrt(jnp.mean(acc * acc, axis=-1, keepdims=True) + EPS)) * gf_ref[...]


def _ffn_call(h2, g_ffn, w_up, conv_w, conv_b, w_down, g_final, batch, t_len, tm):
    m = batch * t_len
    nt = t_len // tm
    tok = pl.BlockSpec((tm, D_MODEL), lambda b, t: (b * nt + t, 0))
    const2 = lambda shape: pl.BlockSpec(shape, lambda b, t: (0, 0))
    resident = lambda shape: pl.BlockSpec(shape, lambda b, t: (0, 0), pipeline_mode=pl.Buffered(1))
    return pl.pallas_call(
        _ffn_kernel,
        grid=(batch, nt),
        in_specs=[
            tok, const2((1, D_MODEL)),
            resident((D_MODEL, 2 * D_FF)),
            const2((FFN_CONV, 2 * D_FF)), const2((1, 2 * D_FF)),
            resident((D_FF, D_MODEL)),
            const2((1, D_MODEL)),
        ],
        out_specs=tok,
        out_shape=jax.ShapeDtypeStruct((m, D_MODEL), F32),
        scratch_shapes=[
            pltpu.VMEM((SUBLANES, 2 * D_FF), F32),
            pltpu.VMEM((2, 2, SUBLANES + tm, FFN_COLS), F32),
            pltpu.VMEM((tm, D_FF), BF16),
        ],
        compiler_params=pltpu.CompilerParams(
            dimension_semantics=("arbitrary", "arbitrary"), vmem_limit_bytes=VMEM_LIMIT),
        name="convffn",
    )(h2, g_ffn, w_up, conv_w, conv_b, w_down, g_final)


def _pick_tile(t_len, want):
    tile = min(want, t_len)
    assert t_len % tile == 0 and tile % CHUNK == 0
    return tile


PROJ_TILE = 1024
MIXER_TILE = 512
FFN_TILE = 512


def kernel(x, g_mix, w_in, ret_norm_g, dn_conv_w, dn_a_log, dn_dt_bias, dn_norm_g, w_ret_br, w_dn_br, w_o, g_ffn,
           w_up, ffn_conv_w, ffn_conv_b, w_down, g_final):
    batch, t_len, d_model = x.shape
    assert d_model == D_MODEL and g_mix.shape[0] == 1 and t_len % CHUNK == 0
    m = batch * t_len
    x2 = x.astype(F32).reshape(m, D_MODEL)

    w_t = jnp.swapaxes(w_in[0], 0, 1)
    w_head = _cast_call(w_t, COL_SMALL)
    w_small, w_sigm = _cast_tail_call(w_t)

    conv_w = jnp.transpose(dn_conv_w[0].reshape(SHORT_CONV, PROJ_STEPS, CONV_W), (1, 0, 2))
    p_plain, p_silu, p_sigm, p_conv, p_small = _proj_call(x2, g_mix, w_head, w_sigm, w_small, conv_w, t_len,
                                                          _pick_tile(t_len, PROJ_TILE))

    cos_t, sin_t = _rope_tables(t_len)
    alog_row = jnp.zeros((1, LANES), F32).at[0, SMALL_DECAY0:SMALL_DECAY0 + DN_HEADS].set(dn_a_log[0])
    dtb_row = jnp.zeros((1, LANES), F32).at[0, SMALL_DECAY0:SMALL_DECAY0 + DN_HEADS].set(dn_dt_bias[0])
    h = _mixer_call(p_plain, p_silu, p_sigm, p_conv, p_small, x2, cos_t, sin_t, ret_norm_g, w_ret_br[0].astype(BF16),
                    alog_row, dtb_row, dn_norm_g, w_dn_br[0].astype(BF16), w_o[0].astype(BF16), batch, t_len,
                    _pick_tile(t_len, MIXER_TILE))

    out = _ffn_call(h, g_ffn, w_up[0].astype(BF16), ffn_conv_w[0], ffn_conv_b, w_down[0].astype(BF16),
                    g_final.reshape(1, D_MODEL), batch, t_len, _pick_tile(t_len, FFN_TILE))
    return out.reshape(batch, t_len, D_MODEL).astype(x.dtype)
```

```python
import functools

import numpy as np
import jax
import jax.numpy as jnp
from jax import lax
from jax.experimental import pallas as pl
from jax.experimental.pallas import tpu as pltpu

D_MODEL = 1024
RET_HEADS = 8
RET_DK = 64
RET_DV = 128
DN_HEADS = 8
DN_DK = 128
DN_DV = 128
CHUNK = 128
SHORT_CONV = 4
FFN_CONV = 3
D_FF = 2816
ROPE_BASE = 10000.0
EPS = 1e-6
GN_EPS = 1e-5

RET_QK = RET_HEADS * RET_DK
RET_V = RET_HEADS * RET_DV
DN_QK = DN_HEADS * DN_DK
DN_V = DN_HEADS * DN_DV

LANES = 128
SUBLANES = 8
COLBLK = 1024
SMALL_BETA0 = 0
SMALL_DECAY0 = 8

VMEM_LIMIT = 56 * 1024 * 1024

F32 = jnp.float32
BF16 = jnp.bfloat16


def _dot(a, b):
    return jnp.dot(a, b, preferred_element_type=F32)


def _dot_nt(a, b):
    return lax.dot_general(a, b, (((1,), (1,)), ((), ())), preferred_element_type=F32)


def _dot_tn(a, b):
    return lax.dot_general(a, b, (((0,), (0,)), ((), ())), preferred_element_type=F32)


def _split3(x):
    hi = x.astype(BF16)
    r1 = x - hi.astype(F32)
    mid = r1.astype(BF16)
    lo = (r1 - mid.astype(F32)).astype(BF16)
    return hi, mid, lo


def _dot_split_rhs(a, x):
    hi, mid, lo = _split3(x)
    return _dot(a, hi) + _dot(a, mid) + _dot(a, lo)


def _sigmoid(x):
    return 0.5 + 0.5 * jnp.tanh(0.5 * x)


def _silu(x):
    h = 0.5 * x
    return h + h * jnp.tanh(h)


def _softplus(x):
    return jnp.maximum(x, 0.0) + jnp.log(1.0 + jnp.exp(-jnp.abs(x)))


PROJ_STEPS = 4
PLAIN_W = (2 * RET_QK + RET_V) // PROJ_STEPS
SILU_W = (RET_V + DN_V) // PROJ_STEPS
SIGM_W = 2 * D_MODEL // PROJ_STEPS
CONV_W = (2 * DN_QK + DN_V) // PROJ_STEPS
COL_RGATE = 2 * RET_QK + RET_V
COL_CONV = COL_RGATE + RET_V
COL_DZ = COL_CONV + 2 * DN_QK + DN_V
COL_SMALL = COL_DZ + DN_V
COL_SIGM = COL_SMALL + 2 * DN_HEADS
CONV_GROUPS = CONV_W // DN_DK
PROJ_ROWS = 256


def _proj_kernel(tiles_per_seq, x_ref, g_ref, w1_ref, w2a_ref, w2b_ref, w3_ref, ws_ref, convw_ref,
                 o1_ref, o2a_ref, o2b_ref, o3_ref, ps_ref, u_ref, carry_ref, buf_ref):
    i = pl.program_id(0)
    s = pl.program_id(1)
    tm = x_ref.shape[0]

    @pl.when(s == 0)
    def _():
        x = x_ref[...]
        r = lax.rsqrt(jnp.mean(x * x, axis=-1, keepdims=True) + EPS)
        u = ((x * r) * g_ref[...]).astype(BF16)
        u_ref[...] = u
        ps_ref[...] = _dot(u, ws_ref[...])

    @pl.when(jnp.logical_and(i == 0, s == 0))
    def _():
        carry_ref[...] = jnp.zeros_like(carry_ref)

    prev = carry_ref[s]
    buf_ref[0:SUBLANES, :] = jnp.where(i % tiles_per_seq == 0, jnp.zeros_like(prev), prev)
    w = convw_ref[s]

    def products(rb):
        r0 = rb * PROJ_ROWS
        u = u_ref[r0:r0 + PROJ_ROWS, :]
        acc = _dot(u, w3_ref[...])
        buf_ref[SUBLANES + r0:SUBLANES + r0 + PROJ_ROWS, :] = acc
        return acc, _dot(u, w2a_ref[...]), _dot(u, w2b_ref[...]), _dot(u, w1_ref[...])

    def elementwise(rb, acc, ga, gb, p):
        r0 = rb * PROJ_ROWS
        rows = slice(r0, r0 + PROJ_ROWS)
        y = acc * w[SHORT_CONV - 1:SHORT_CONV]
        for t in range(1, SHORT_CONV):
            y = y + buf_ref[SUBLANES + r0 - t:SUBLANES + r0 - t + PROJ_ROWS, :] * w[SHORT_CONV - 1 - t:SHORT_CONV - t]
        y = _silu(y)
        for l in range(CONV_GROUPS):
            head_group = s * CONV_GROUPS + l
            cols = slice(l * DN_DK, (l + 1) * DN_DK)
            yl = y[:, cols]
            r = lax.rsqrt(jnp.sum(yl * yl, axis=-1, keepdims=True) + EPS)
            r = r * jnp.where(head_group < DN_HEADS, DN_DK ** -0.5, 1.0)
            scale = jnp.where(head_group < 2 * DN_HEADS, r, 1.0)
            o3_ref[rows, cols] = (yl * scale).astype(BF16)
        o2a_ref[rows, :] = _silu(ga).astype(BF16)
        o2b_ref[rows, :] = _sigmoid(gb).astype(BF16)
        o1_ref[rows, :] = p.astype(BF16)

    nrb = tm // PROJ_ROWS
    pending = products(0)
    for rb in range(1, nrb):
        nxt = products(rb)
        elementwise(rb - 1, *pending)
        pending = nxt
    elementwise(nrb - 1, *pending)
    carry_ref[s] = buf_ref[tm:tm + SUBLANES, :]


def _cast_kernel(wp_ref, wa_ref, wg_ref, wc_ref, ws_ref, op_ref, oa_ref, og_ref, oc_ref, os_ref):
    op_ref[...] = wp_ref[...].T.astype(BF16)
    oa_ref[...] = wa_ref[...].T.astype(BF16)
    og_ref[...] = wg_ref[...].T.astype(BF16)
    oc_ref[...] = wc_ref[...].T.astype(BF16)
    rows = lax.broadcasted_iota(jnp.int32, ws_ref.shape, 0)
    small = jnp.where(rows < COL_SIGM - COL_SMALL, ws_ref[...], 0.0)
    os_ref[...] = small.T.astype(BF16)


def _cast_call(w_t):
    half = PROJ_STEPS // 2
    elem = lambda n: (pl.Element(n), pl.Element(D_MODEL))
    out = lambda n: pl.BlockSpec((None, D_MODEL, n), lambda s: (s, 0, 0))
    shape = lambda n: jax.ShapeDtypeStruct((PROJ_STEPS, D_MODEL, n), BF16)
    return pl.pallas_call(
        _cast_kernel,
        grid=(PROJ_STEPS,),
        in_specs=[
            pl.BlockSpec((PLAIN_W, D_MODEL), lambda s: (s, 0)),
            pl.BlockSpec((SILU_W, D_MODEL),
                         lambda s: (jnp.where(s < half, COL_RGATE // SILU_W + s, COL_DZ // SILU_W + s - half), 0)),
            pl.BlockSpec(elem(SIGM_W), lambda s: (pl.multiple_of(COL_SIGM + s * SIGM_W, SUBLANES), 0)),
            pl.BlockSpec((CONV_W, D_MODEL), lambda s: (COL_CONV // CONV_W + s, 0)),
            pl.BlockSpec(elem(LANES), lambda s: (COL_SMALL, 0)),
        ],
        out_specs=[out(PLAIN_W), out(SILU_W), out(SIGM_W), out(CONV_W),
                   pl.BlockSpec((D_MODEL, LANES), lambda s: (0, 0))],
        out_shape=[shape(PLAIN_W), shape(SILU_W), shape(SIGM_W), shape(CONV_W),
                   jax.ShapeDtypeStruct((D_MODEL, LANES), BF16)],
        compiler_params=pltpu.CompilerParams(dimension_semantics=("arbitrary",), vmem_limit_bytes=VMEM_LIMIT),
        name="cast_w_in",
    )(w_t, w_t, w_t, w_t, w_t)


def _proj_call(x2, g_mix, w_plain, w_silu, w_sigm, w_conv, w_small, conv_w, t_len, tm):
    m = x2.shape[0]
    assert t_len % tm == 0
    wspec = lambda n: pl.BlockSpec((None, D_MODEL, n), lambda i, s: (s, 0, 0))
    ospec = lambda n: pl.BlockSpec((tm, n), lambda i, s: (i, s))
    return pl.pallas_call(
        functools.partial(_proj_kernel, t_len // tm),
        grid=(m // tm, PROJ_STEPS),
        in_specs=[
            pl.BlockSpec((tm, D_MODEL), lambda i, s: (i, 0)),
            pl.BlockSpec((1, D_MODEL), lambda i, s: (0, 0)),
            wspec(PLAIN_W), wspec(SILU_W), wspec(SIGM_W), wspec(CONV_W),
            pl.BlockSpec((D_MODEL, LANES), lambda i, s: (0, 0)),
            pl.BlockSpec((PROJ_STEPS, SHORT_CONV, CONV_W), lambda i, s: (0, 0, 0)),
        ],
        out_specs=[
            ospec(PLAIN_W), ospec(SILU_W), ospec(SIGM_W), ospec(CONV_W),
            pl.BlockSpec((tm, LANES), lambda i, s: (i, 0)),
        ],
        out_shape=[
            jax.ShapeDtypeStruct((m, PROJ_STEPS * PLAIN_W), BF16),
            jax.ShapeDtypeStruct((m, PROJ_STEPS * SILU_W), BF16),
            jax.ShapeDtypeStruct((m, PROJ_STEPS * SIGM_W), BF16),
            jax.ShapeDtypeStruct((m, PROJ_STEPS * CONV_W), BF16),
            jax.ShapeDtypeStruct((m, LANES), F32),
        ],
        scratch_shapes=[
            pltpu.VMEM((tm, D_MODEL), BF16),
            pltpu.VMEM((PROJ_STEPS, SUBLANES, CONV_W), F32),
            pltpu.VMEM((SUBLANES + tm, CONV_W), F32),
        ],
        compiler_params=pltpu.CompilerParams(
            dimension_semantics=("arbitrary", "arbitrary"), vmem_limit_bytes=VMEM_LIMIT),
        name="proj",
    )(x2, g_mix, w_plain, w_silu, w_sigm, w_conv, w_small, conv_w)


def _ret_body(decays, nchunk, qk_ref, v_ref, gate_ref, gr_ref, cos_ref, sin_ref, dmat_ref, xi_ref,
              zeta_ref, gn_ref, wbr_ref, s_ref, stage_ref):
    lane = lax.broadcasted_iota(jnp.int32, (CHUNK, LANES), 1)
    first_half = (lane % RET_DK) < (RET_DK // 2)
    head_lo = lane < RET_DK

    def rot(x, cos_t, sin_t):
        partner = jnp.where(first_half, pltpu.roll(x, LANES - RET_DK // 2, 1), pltpu.roll(x, RET_DK // 2, 1))
        return x * cos_t + partner * sin_t

    qb, kb, qx, kz = {}, {}, {}, {}
    for c in range(nchunk):
        rows = slice(c * CHUNK, (c + 1) * CHUNK)
        cos_t = cos_ref[rows, :]
        sin_t = sin_ref[rows, :]
        for p in range(RET_HEADS // 2):
            q = rot(qk_ref[rows, p * LANES:(p + 1) * LANES].astype(F32), cos_t, sin_t)
            k = rot(qk_ref[rows, RET_QK + p * LANES:RET_QK + (p + 1) * LANES].astype(F32), cos_t, sin_t)
            k = k * (RET_DK ** -0.5)
            qb[c, p] = q.astype(BF16)
            kb[c, p] = k.astype(BF16)
            qx[c, p] = (q * xi_ref[p]).astype(BF16)
            kz[c, p] = (k * zeta_ref[p]).astype(BF16)

    items = [(c, h) for c in range(nchunk) for h in range(RET_HEADS)]

    def head_mask(h):
        return head_lo if h % 2 == 0 else jnp.logical_not(head_lo)

    def v_of(c, h):
        return v_ref[c * CHUNK:(c + 1) * CHUNK, h * RET_DV:(h + 1) * RET_DV]

    scores16 = {}
    for c, h in items:
        qm = jnp.where(head_mask(h), qb[c, h // 2], jnp.zeros((CHUNK, LANES), BF16))
        scores16[c, h] = (_dot_nt(qm, kb[c, h // 2]) * dmat_ref[h]).astype(BF16)
    kv = {(c, h): _dot_tn(kz[c, h // 2], v_of(c, h)) for c, h in items}

    for c in range(nchunk):
        rows = slice(c * CHUNK, (c + 1) * CHUNK)
        s = [s_ref[h] for h in range(RET_HEADS)]
        ro = []
        for h in range(RET_HEADS):
            qxm = jnp.where(head_mask(h), qx[c, h // 2], jnp.zeros((CHUNK, LANES), BF16))
            ro.append(_dot(jnp.concatenate([scores16[c, h], qxm], axis=1),
                           jnp.concatenate([v_of(c, h), s[h].astype(BF16)], axis=0)))
        for h in range(RET_HEADS):
            s_ref[h] = s[h] * decays[h] + kv[c, h]
        for h in range(RET_HEADS):
            cols = slice(h * RET_DV, (h + 1) * RET_DV)
            mu = jnp.mean(ro[h], axis=-1, keepdims=True)
            d = ro[h] - mu
            var = jnp.mean(d * d, axis=-1, keepdims=True)
            y = (d * lax.rsqrt(var + GN_EPS)) * gn_ref[:, cols] * gate_ref[rows, cols].astype(F32)
            stage_ref[rows, cols] = y.astype(BF16)

    return gr_ref[...].astype(F32) * _dot(stage_ref[...], wbr_ref[...])


def _ret_tables():
    h = np.arange(RET_HEADS, dtype=np.float64)
    gamma = 1.0 - 2.0 ** (-5.0 - h)
    log_g = np.log(gamma)
    idx = np.arange(CHUNK, dtype=np.float64)
    diff = idx[:, None] - idx[None, :]
    dmat = np.where(diff >= 0, np.exp(log_g[:, None, None] * np.maximum(diff, 0.0)[None]), 0.0)
    zeta = np.exp(log_g[:, None] * (CHUNK - 1.0 - idx)[None, :])
    xi = np.exp(log_g[:, None] * (idx + 1.0)[None, :])
    decays = tuple(float(v) for v in np.exp(log_g * CHUNK))

    def pair_table(t):
        t = t.reshape(RET_HEADS // 2, 2, CHUNK)
        return np.repeat(np.transpose(t, (0, 2, 1)), RET_DK, axis=2)

    return (jnp.asarray(dmat, F32), jnp.asarray(pair_table(xi), F32), jnp.asarray(pair_table(zeta), F32), decays)


def _rope_tables(t_len):
    half = RET_DK // 2
    inv = ROPE_BASE ** (-np.arange(0, RET_DK, 2, dtype=np.float64) / RET_DK)
    ang = np.arange(t_len, dtype=np.float64)[:, None] * inv[None, :]
    cos, sin = np.cos(ang), np.sin(ang)
    cos_t = np.concatenate([cos, cos], axis=1)
    sin_t = np.concatenate([-sin, sin], axis=1)
    assert cos_t.shape[1] == 2 * half
    reps = LANES // RET_DK
    return jnp.asarray(np.tile(cos_t, (1, reps)), F32), jnp.asarray(np.tile(sin_t, (1, reps)), F32)


INV_FULL_LEVELS = 3


def _odd_block_rows(x, b):
    return jnp.concatenate([x[(2 * i + 1) * b:(2 * i + 2) * b] for i in range(CHUNK // (2 * b))], axis=0)


def _with_odd_block_rows(x, odd, b):
    parts = []
    for i in range(CHUNK // (2 * b)):
        parts += [x[2 * i * b:(2 * i + 1) * b], odd[i * b:(i + 1) * b]]
    return jnp.concatenate(parts, axis=0)


def _inv_unit_lower_many(a_list, qmask_ref, qmask_odd_ref):
    row = lax.broadcasted_iota(jnp.int32, (CHUNK, CHUNK), 0)
    col = lax.broadcasted_iota(jnp.int32, (CHUNK, CHUNK), 1)
    eye = jnp.where(row == col, 1.0, 0.0).astype(F32)
    a16 = [a.astype(BF16) for a in a_list]
    d = [eye - a * qmask_ref[0] for a in a_list]
    level = 1
    b = 2
    while b < CHUNK:
        d16 = [x.astype(BF16) for x in d]
        if level < INV_FULL_LEVELS:
            mask = qmask_ref[level]
            t16 = [_dot(x, a).astype(BF16) for x, a in zip(d16, a16)]
            d = [x - _dot(t, y) * mask for x, t, y in zip(d, t16, d16)]
        else:
            mask = qmask_odd_ref[level - INV_FULL_LEVELS]
            t16 = [_dot(_odd_block_rows(x, b), a).astype(BF16) for x, a in zip(d16, a16)]
            d = [_with_odd_block_rows(x, _odd_block_rows(x, b) - _dot(t, y) * mask, b)
                 for x, t, y in zip(d, t16, d16)]
        level += 1
        b *= 2
    return d


def _dn_body(nchunk, mr, q_ref, k_ref, v_ref, z_ref, gd_ref, x_ref, ps_ref, alog_ref, dtb_ref,
             tri_ref, qmask_ref, qmask_odd_ref, ng_ref, wbr_ref, wo_ref, h_ref, s_ref, stage_ref):
    ps = ps_ref[...]
    beta_all = _sigmoid(ps)
    g_all = -jnp.exp(alog_ref[...]) * _softplus(ps + dtb_ref[...])

    row = lax.broadcasted_iota(jnp.int32, (CHUNK, CHUNK), 0)
    col = lax.broadcasted_iota(jnp.int32, (CHUNK, CHUNK), 1)
    causal = row >= col
    strict = row > col
    tri = tri_ref[...]

    def lanes(x, j):
        return jnp.broadcast_to(x[:, j:j + 1], (CHUNK, LANES))

    g_cum, g_cum_t, exp_g_all, exp_rest_all = [], [], [], []
    for c in range(nchunk):
        g = _dot_split_rhs(tri, g_all[c * CHUNK:(c + 1) * CHUNK])
        g_cum.append(g)
        g_cum_t.append(g.T)
        exp_g_all.append(jnp.exp(g))
        exp_rest_all.append(jnp.exp(g[CHUNK - 1:CHUNK, :] - g))

    def decay_of(c, h, mask):
        gc = lanes(g_cum[c], SMALL_DECAY0 + h)
        gr = jnp.broadcast_to(g_cum_t[c][SMALL_DECAY0 + h:SMALL_DECAY0 + h + 1, :], (CHUNK, CHUNK))
        return jnp.where(mask, jnp.exp(gc - gr), 0.0)

    def block(ref, c, h):
        return ref[c * CHUNK:(c + 1) * CHUNK, h * DN_DV:(h + 1) * DN_DV]

    items = [(c, h) for c in range(nchunk) for h in range(DN_HEADS)]

    a_list, rhs16 = [], []
    for c, h in items:
        k16 = block(k_ref, c, h)
        beta = lanes(beta_all[c * CHUNK:(c + 1) * CHUNK], SMALL_BETA0 + h)
        k_beta = k16.astype(F32) * beta
        a_list.append(_dot_nt(k_beta.astype(BF16), k16) * decay_of(c, h, strict))
        exp_g = lanes(exp_g_all[c], SMALL_DECAY0 + h)
        rhs16.append(jnp.concatenate([block(v_ref, c, h).astype(F32) * beta, k_beta * exp_g], axis=1).astype(BF16))

    minv = _inv_unit_lower_many(a_list, qmask_ref, qmask_odd_ref)
    sol = [_dot(m.astype(BF16), r) for m, r in zip(minv, rhs16)]

    for c in range(nchunk):
        qa16, kdec16, sdec = [], [], []
        for h in range(DN_HEADS):
            q16 = block(q_ref, c, h)
            k16 = block(k_ref, c, h)
            exp_g = lanes(exp_g_all[c], SMALL_DECAY0 + h)
            attn = _dot_nt(q16, k16) * decay_of(c, h, causal)
            qa16.append(jnp.concatenate([(q16.astype(F32) * exp_g).astype(BF16), attn.astype(BF16)], axis=1))
            kdec16.append((k16.astype(F32) * lanes(exp_rest_all[c], SMALL_DECAY0 + h)).astype(BF16))
            sdec.append(exp_g[CHUNK - 1:CHUNK, :])
        idx = [c * DN_HEADS + h for h in range(DN_HEADS)]
        s = [s_ref[h] for h in range(DN_HEADS)]
        s16 = [x.astype(BF16) for x in s]
        ws = [_dot(sol[i][:, DN_DV:].astype(BF16), s16[h]) for h, i in enumerate(idx)]
        vn16 = [(sol[i][:, :DN_DV] - ws[h]).astype(BF16) for h, i in enumerate(idx)]
        o = [_dot(qa16[h], jnp.concatenate([s16[h], vn16[h]], axis=0)) for h in range(DN_HEADS)]
        for h in range(DN_HEADS):
            s_ref[h] = s[h] * sdec[h] + _dot_tn(kdec16[h], vn16[h])
        for h in range(DN_HEADS):
            on = o[h] * lax.rsqrt(jnp.mean(o[h] * o[h], axis=-1, keepdims=True) + EPS) * ng_ref[...]
            stage_ref[c * CHUNK:(c + 1) * CHUNK, h * DN_DV:(h + 1) * DN_DV] = (
                on * block(z_ref, c, h).astype(F32)).astype(BF16)

    y_dn = _dot(stage_ref[...], wbr_ref[...])
    merged = mr + gd_ref[...].astype(F32) * y_dn
    h_ref[...] = x_ref[...] + _dot(merged.astype(BF16), wo_ref[...])


def _dn_tables():
    idx = np.arange(CHUNK)
    r, c = idx[:, None], idx[None, :]
    qmasks, qmasks_odd = [], []
    b = 1
    while b < CHUNK:
        m = ((r // (2 * b)) == (c // (2 * b))) & ((r // b) % 2 == 1) & ((c // b) % 2 == 0)
        if len(qmasks) < INV_FULL_LEVELS:
            qmasks.append(m)
        else:
            qmasks_odd.append(m[(idx // b) % 2 == 1])
        b *= 2
    tri = (r >= c).astype(np.float32)
    return (jnp.asarray(tri, BF16), jnp.asarray(np.stack(qmasks), F32), jnp.asarray(np.stack(qmasks_odd), F32))


def _mixer_kernel(decays, nchunk, n_ret_in, n_dn_in, *refs):
    ret_in = refs[:n_ret_in]
    dn_in = refs[n_ret_in:n_ret_in + n_dn_in]
    h_ref, s_ret_ref, stage_ret_ref, s_dn_ref, stage_dn_ref = refs[n_ret_in + n_dn_in:]

    @pl.when(pl.program_id(1) == 0)
    def _():
        s_ret_ref[...] = jnp.zeros_like(s_ret_ref)
        s_dn_ref[...] = jnp.zeros_like(s_dn_ref)

    mr = _ret_body(decays, nchunk, *ret_in, s_ret_ref, stage_ret_ref)
    _dn_body(nchunk, mr, *dn_in, h_ref, s_dn_ref, stage_dn_ref)


def _mixer_call(p_plain, p_silu, p_sigm, p_conv, p_small, x2, cos_t, sin_t, ret_norm_g, w_ret_br, alog_row, dtb_row,
                dn_norm_g, w_dn_br, w_o, batch, t_len, tt):
    m = batch * t_len
    nt = t_len // tt
    dmat, xi, zeta, decays = _ret_tables()
    tri, qmask, qmask_odd = _dn_tables()
    row = lambda cb: pl.BlockSpec((tt, COLBLK), lambda b, t: (b * nt + t, cb))
    tok = lambda n: pl.BlockSpec((tt, n), lambda b, t: (b * nt + t, 0))
    pos = pl.BlockSpec((tt, LANES), lambda b, t: (t, 0))
    const = lambda shape: pl.BlockSpec(shape, lambda b, t: (0,) * len(shape))
    ret_specs = [
        row(0), row(1), row(0), row(0), pos, pos,
        const((RET_HEADS, CHUNK, CHUNK)), const((RET_HEADS // 2, CHUNK, LANES)), const((RET_HEADS // 2, CHUNK, LANES)),
        const((1, RET_V)), const((RET_V, D_MODEL)),
    ]
    ret_args = [p_plain, p_plain, p_silu, p_sigm, cos_t, sin_t, dmat, xi, zeta, ret_norm_g, w_ret_br]
    dn_specs = [
        row(0), row(1), row(2), row(1), row(1), tok(D_MODEL), tok(LANES),
        const((1, LANES)), const((1, LANES)), const((CHUNK, CHUNK)), const(qmask.shape), const(qmask_odd.shape),
        const((1, DN_DV)), const((DN_V, D_MODEL)), const((D_MODEL, D_MODEL)),
    ]
    dn_args = [p_conv, p_conv, p_conv, p_silu, p_sigm, x2, p_small, alog_row, dtb_row, tri, qmask, qmask_odd,
               dn_norm_g, w_dn_br, w_o]
    return pl.pallas_call(
        functools.partial(_mixer_kernel, decays, tt // CHUNK, len(ret_specs), len(dn_specs)),
        grid=(batch, nt),
        in_specs=ret_specs + dn_specs,
        out_specs=tok(D_MODEL),
        out_shape=jax.ShapeDtypeStruct((m, D_MODEL), F32),
        scratch_shapes=[
            pltpu.VMEM((RET_HEADS, LANES, RET_DV), F32),
            pltpu.VMEM((tt, RET_V), BF16),
            pltpu.VMEM((DN_HEADS, DN_DK, DN_DV), F32),
            pltpu.VMEM((tt, DN_V), BF16),
        ],
        compiler_params=pltpu.CompilerParams(
            dimension_semantics=("arbitrary", "arbitrary"), vmem_limit_bytes=VMEM_LIMIT),
        name="mixer",
    )(*ret_args, *dn_args)


FFN_COLS = 256
FFN_STEPS = D_FF // FFN_COLS


def _ffn_kernel(h_ref, g_ref, wup_ref, cw_ref, cb_ref, wdn_ref, gf_ref, o_ref, carry_ref, up_ref, act_ref):
    @pl.when(pl.program_id(1) == 0)
    def _():
        carry_ref[...] = jnp.zeros_like(carry_ref)

    tm = h_ref.shape[0]
    h = h_ref[...]
    u = ((h * lax.rsqrt(jnp.mean(h * h, axis=-1, keepdims=True) + EPS)) * g_ref[...]).astype(BF16)

    def conv_branch(c0, buf):
        cols = slice(c0, c0 + FFN_COLS)
        up = _dot(u, wup_ref[:, cols])
        buf[0:SUBLANES, :] = carry_ref[:, cols]
        buf[SUBLANES:SUBLANES + tm, :] = up
        carry_ref[:, cols] = up[tm - SUBLANES:]
        w = cw_ref[:, cols]
        y = up * w[FFN_CONV - 1:FFN_CONV] + cb_ref[:, cols]
        for s in range(1, FFN_CONV):
            y = y + buf[SUBLANES - s:SUBLANES - s + tm, :] * w[FFN_CONV - 1 - s:FFN_CONV - s]
        return y

    for j in range(FFN_STEPS):
        a = conv_branch(j * FFN_COLS, up_ref.at[j % 2, 0])
        b = conv_branch(D_FF + j * FFN_COLS, up_ref.at[j % 2, 1])
        act_ref[:, j * FFN_COLS:(j + 1) * FFN_COLS] = (_silu(a) * b).astype(BF16)

    acc = h + _dot(act_ref[...], wdn_ref[...])
    o_ref[...] = (acc * lax.rsqrt(jnp.mean(acc * acc, axis=-1, keepdims=True) + EPS)) * gf_ref[...]


def _ffn_call(h2, g_ffn, w_up, conv_w, conv_b, w_down, g_final, batch, t_len, tm):
    m = batch * t_len
    nt = t_len // tm
    tok = pl.BlockSpec((tm, D_MODEL), lambda b, t: (b * nt + t, 0))
    const2 = lambda shape: pl.BlockSpec(shape, lambda b, t: (0, 0))
    resident = lambda shape: pl.BlockSpec(shape, lambda b, t: (0, 0), pipeline_mode=pl.Buffered(1))
    return pl.pallas_call(
        _ffn_kernel,
        grid=(batch, nt),
        in_specs=[
            tok, const2((1, D_MODEL)),
            resident((D_MODEL, 2 * D_FF)),
            const2((FFN_CONV, 2 * D_FF)), const2((1, 2 * D_FF)),
            resident((D_FF, D_MODEL)),
            const2((1, D_MODEL)),
        ],
        out_specs=tok,
        out_shape=jax.ShapeDtypeStruct((m, D_MODEL), F32),
        scratch_shapes=[
            pltpu.VMEM((SUBLANES, 2 * D_FF), F32),
            pltpu.VMEM((2, 2, SUBLANES + tm, FFN_COLS), F32),
            pltpu.VMEM((tm, D_FF), BF16),
        ],
        compiler_params=pltpu.CompilerParams(
            dimension_semantics=("arbitrary", "arbitrary"), vmem_limit_bytes=VMEM_LIMIT),
        name="convffn",
    )(h2, g_ffn, w_up, conv_w, conv_b, w_down, g_final)


def _pick_tile(t_len, want):
    tile = min(want, t_len)
    assert t_len % tile == 0 and tile % CHUNK == 0
    return tile


PROJ_TILE = 1024
MIXER_TILE = 512
FFN_TILE = 512


def kernel(x, g_mix, w_in, ret_norm_g, dn_conv_w, dn_a_log, dn_dt_bias, dn_norm_g, w_ret_br, w_dn_br, w_o, g_ffn,
           w_up, ffn_conv_w, ffn_conv_b, w_down, g_final):
    batch, t_len, d_model = x.shape
    assert d_model == D_MODEL and g_mix.shape[0] == 1 and t_len % CHUNK == 0
    m = batch * t_len
    x2 = x.astype(F32).reshape(m, D_MODEL)

    w_t = jnp.swapaxes(w_in[0], 0, 1)
    w_plain, w_silu, w_sigm, w_conv, w_small = _cast_call(w_t)

    conv_w = jnp.transpose(dn_conv_w[0].reshape(SHORT_CONV, PROJ_STEPS, CONV_W), (1, 0, 2))
    p_plain, p_silu, p_sigm, p_conv, p_small = _proj_call(x2, g_mix, w_plain, w_silu, w_sigm, w_conv, w_small, conv_w,
                                                          t_len,
                                                          _pick_tile(t_len, PROJ_TILE))

    cos_t, sin_t = _rope_tables(t_len)
    alog_row = jnp.zeros((1, LANES), F32).at[0, SMALL_DECAY0:SMALL_DECAY0 + DN_HEADS].set(dn_a_log[0])
    dtb_row = jnp.zeros((1, LANES), F32).at[0, SMALL_DECAY0:SMALL_DECAY0 + DN_HEADS].set(dn_dt_bias[0])
    h = _mixer_call(p_plain, p_silu, p_sigm, p_conv, p_small, x2, cos_t, sin_t, ret_norm_g, w_ret_br[0].astype(BF16),
                    alog_row, dtb_row, dn_norm_g, w_dn_br[0].astype(BF16), w_o[0].astype(BF16), batch, t_len,
                    _pick_tile(t_len, MIXER_TILE))

    out = _ffn_call(h, g_ffn, w_up[0].astype(BF16), ffn_conv_w[0], ffn_conv_b, w_down[0].astype(BF16),
                    g_final.reshape(1, D_MODEL), batch, t_len, _pick_tile(t_len, FFN_TILE))
    return out.reshape(batch, t_len, D_MODEL).astype(x.dtype)
```

```python
import functools

import numpy as np
import jax
import jax.numpy as jnp
from jax import lax
from jax.experimental import pallas as pl
from jax.experimental.pallas import tpu as pltpu

D_MODEL = 1024
RET_HEADS = 8
RET_DK = 64
RET_DV = 128
DN_HEADS = 8
DN_DK = 128
DN_DV = 128
CHUNK = 128
SHORT_CONV = 4
FFN_CONV = 3
D_FF = 2816
ROPE_BASE = 10000.0
EPS = 1e-6
GN_EPS = 1e-5

RET_QK = RET_HEADS * RET_DK
RET_V = RET_HEADS * RET_DV
DN_QK = DN_HEADS * DN_DK
DN_V = DN_HEADS * DN_DV

LANES = 128
SUBLANES = 8
COLBLK = 1024
SMALL_BETA0 = 0
SMALL_DECAY0 = 8

VMEM_LIMIT = 56 * 1024 * 1024

F32 = jnp.float32
BF16 = jnp.bfloat16


def _dot(a, b):
    return jnp.dot(a, b, preferred_element_type=F32)


def _dot_nt(a, b):
    return lax.dot_general(a, b, (((1,), (1,)), ((), ())), preferred_element_type=F32)


def _dot_tn(a, b):
    return lax.dot_general(a, b, (((0,), (0,)), ((), ())), preferred_element_type=F32)


def _split3(x):
    hi = x.astype(BF16)
    r1 = x - hi.astype(F32)
    mid = r1.astype(BF16)
    lo = (r1 - mid.astype(F32)).astype(BF16)
    return hi, mid, lo


def _dot_split_rhs(a, x):
    hi, mid, lo = _split3(x)
    return _dot(a, hi) + _dot(a, mid) + _dot(a, lo)


def _sigmoid(x):
    return 0.5 + 0.5 * jnp.tanh(0.5 * x)


def _silu(x):
    h = 0.5 * x
    return h + h * jnp.tanh(h)


def _softplus(x):
    return jnp.maximum(x, 0.0) + jnp.log(1.0 + jnp.exp(-jnp.abs(x)))


PROJ_STEPS = 4
PLAIN_W = (2 * RET_QK + RET_V) // PROJ_STEPS
SILU_W = (RET_V + DN_V) // PROJ_STEPS
SIGM_W = 2 * D_MODEL // PROJ_STEPS
CONV_W = (2 * DN_QK + DN_V) // PROJ_STEPS
COL_RGATE = 2 * RET_QK + RET_V
COL_CONV = COL_RGATE + RET_V
COL_DZ = COL_CONV + 2 * DN_QK + DN_V
COL_SMALL = COL_DZ + DN_V
COL_SIGM = COL_SMALL + 2 * DN_HEADS
CONV_GROUPS = CONV_W // DN_DK
PROJ_ROWS = 256


def _proj_kernel(blocks_per_seq, x_ref, g_ref, w1_ref, w2a_ref, w2b_ref, w3_ref, ws_ref, convw_ref,
                 o1_ref, o2a_ref, o2b_ref, o3_ref, ps_ref, carry_ref, buf_ref):
    i = pl.program_id(0)
    rows = x_ref.shape[0]

    @pl.when(i == 0)
    def _():
        carry_ref[...] = jnp.zeros_like(carry_ref)

    x = x_ref[...]
    u = ((x * lax.rsqrt(jnp.mean(x * x, axis=-1, keepdims=True) + EPS)) * g_ref[...]).astype(BF16)
    ps_ref[...] = _dot(u, ws_ref[...])
    seq_start = i % blocks_per_seq == 0

    def products(cs):
        acc = _dot(u, w3_ref[cs])
        prev = carry_ref[cs]
        buf_ref[cs, 0:SUBLANES, :] = jnp.where(seq_start, jnp.zeros_like(prev), prev)
        buf_ref[cs, SUBLANES:SUBLANES + rows, :] = acc
        carry_ref[cs] = acc[rows - SUBLANES:]
        return acc, _dot(u, w2a_ref[cs]), _dot(u, w2b_ref[cs]), _dot(u, w1_ref[cs])

    def elementwise(cs, acc, ga, gb, p):
        w = convw_ref[cs]
        y = acc * w[SHORT_CONV - 1:SHORT_CONV]
        for t in range(1, SHORT_CONV):
            y = y + buf_ref[cs, SUBLANES - t:SUBLANES - t + rows, :] * w[SHORT_CONV - 1 - t:SHORT_CONV - t]
        y = _silu(y)
        for l in range(CONV_GROUPS):
            head_group = cs * CONV_GROUPS + l
            yl = y[:, l * DN_DK:(l + 1) * DN_DK]
            if head_group < 2 * DN_HEADS:
                scale = DN_DK ** -0.5 if head_group < DN_HEADS else 1.0
                yl = yl * (lax.rsqrt(jnp.sum(yl * yl, axis=-1, keepdims=True) + EPS) * scale)
            o3_ref[:, cs * CONV_W + l * DN_DK:cs * CONV_W + (l + 1) * DN_DK] = yl.astype(BF16)
        o2a_ref[:, cs * SILU_W:(cs + 1) * SILU_W] = _silu(ga).astype(BF16)
        o2b_ref[:, cs * SIGM_W:(cs + 1) * SIGM_W] = _sigmoid(gb).astype(BF16)
        o1_ref[:, cs * PLAIN_W:(cs + 1) * PLAIN_W] = p.astype(BF16)

    pending = products(0)
    for cs in range(1, PROJ_STEPS):
        nxt = products(cs)
        elementwise(cs - 1, *pending)
        pending = nxt
    elementwise(PROJ_STEPS - 1, *pending)


def _cast_kernel(wp_ref, wa_ref, wg_ref, wc_ref, ws_ref, op_ref, oa_ref, og_ref, oc_ref, os_ref):
    op_ref[...] = wp_ref[...].T.astype(BF16)
    oa_ref[...] = wa_ref[...].T.astype(BF16)
    og_ref[...] = wg_ref[...].T.astype(BF16)
    oc_ref[...] = wc_ref[...].T.astype(BF16)
    rows = lax.broadcasted_iota(jnp.int32, ws_ref.shape, 0)
    small = jnp.where(rows < COL_SIGM - COL_SMALL, ws_ref[...], 0.0)
    os_ref[...] = small.T.astype(BF16)


def _cast_call(w_t):
    half = PROJ_STEPS // 2
    elem = lambda n: (pl.Element(n), pl.Element(D_MODEL))
    out = lambda n: pl.BlockSpec((None, D_MODEL, n), lambda s: (s, 0, 0))
    shape = lambda n: jax.ShapeDtypeStruct((PROJ_STEPS, D_MODEL, n), BF16)
    return pl.pallas_call(
        _cast_kernel,
        grid=(PROJ_STEPS,),
        in_specs=[
            pl.BlockSpec((PLAIN_W, D_MODEL), lambda s: (s, 0)),
            pl.BlockSpec((SILU_W, D_MODEL),
                         lambda s: (jnp.where(s < half, COL_RGATE // SILU_W + s, COL_DZ // SILU_W + s - half), 0)),
            pl.BlockSpec(elem(SIGM_W), lambda s: (pl.multiple_of(COL_SIGM + s * SIGM_W, SUBLANES), 0)),
            pl.BlockSpec((CONV_W, D_MODEL), lambda s: (COL_CONV // CONV_W + s, 0)),
            pl.BlockSpec(elem(LANES), lambda s: (COL_SMALL, 0)),
        ],
        out_specs=[out(PLAIN_W), out(SILU_W), out(SIGM_W), out(CONV_W),
                   pl.BlockSpec((D_MODEL, LANES), lambda s: (0, 0))],
        out_shape=[shape(PLAIN_W), shape(SILU_W), shape(SIGM_W), shape(CONV_W),
                   jax.ShapeDtypeStruct((D_MODEL, LANES), BF16)],
        compiler_params=pltpu.CompilerParams(dimension_semantics=("arbitrary",), vmem_limit_bytes=VMEM_LIMIT),
        name="cast_w_in",
    )(w_t, w_t, w_t, w_t, w_t)


def _proj_call(x2, g_mix, w_plain, w_silu, w_sigm, w_conv, w_small, conv_w, t_len, rows):
    m = x2.shape[0]
    assert t_len % rows == 0
    resident = lambda a: pl.BlockSpec(a.shape, lambda i: (0,) * a.ndim, pipeline_mode=pl.Buffered(1))
    tok = lambda n: pl.BlockSpec((rows, n), lambda i: (i, 0))
    widths = (PROJ_STEPS * PLAIN_W, PROJ_STEPS * SILU_W, PROJ_STEPS * SIGM_W, PROJ_STEPS * CONV_W)
    return pl.pallas_call(
        functools.partial(_proj_kernel, t_len // rows),
        grid=(m // rows,),
        in_specs=[tok(D_MODEL), resident(g_mix), resident(w_plain), resident(w_silu), resident(w_sigm),
                  resident(w_conv), resident(w_small), resident(conv_w)],
        out_specs=[tok(n) for n in widths] + [tok(LANES)],
        out_shape=[jax.ShapeDtypeStruct((m, n), BF16) for n in widths] + [jax.ShapeDtypeStruct((m, LANES), F32)],
        scratch_shapes=[
            pltpu.VMEM((PROJ_STEPS, SUBLANES, CONV_W), F32),
            pltpu.VMEM((PROJ_STEPS, SUBLANES + rows, CONV_W), F32),
        ],
        compiler_params=pltpu.CompilerParams(dimension_semantics=("arbitrary",), vmem_limit_bytes=VMEM_LIMIT),
        name="proj",
    )(x2, g_mix, w_plain, w_silu, w_sigm, w_conv, w_small, conv_w)


def _ret_body(decays, nchunk, qk_ref, v_ref, gate_ref, gr_ref, cos_ref, sin_ref, dmat_ref, xi_ref,
              zeta_ref, gn_ref, wbr_ref, s_ref, stage_ref):
    lane = lax.broadcasted_iota(jnp.int32, (CHUNK, LANES), 1)
    first_half = (lane % RET_DK) < (RET_DK // 2)
    head_lo = lane < RET_DK

    def rot(x, cos_t, sin_t):
        partner = jnp.where(first_half, pltpu.roll(x, LANES - RET_DK // 2, 1), pltpu.roll(x, RET_DK // 2, 1))
        return x * cos_t + partner * sin_t

    qb, kb, qx, kz = {}, {}, {}, {}
    for c in range(nchunk):
        rows = slice(c * CHUNK, (c + 1) * CHUNK)
        cos_t = cos_ref[rows, :]
        sin_t = sin_ref[rows, :]
        for p in range(RET_HEADS // 2):
            q = rot(qk_ref[rows, p * LANES:(p + 1) * LANES].astype(F32), cos_t, sin_t)
            k = rot(qk_ref[rows, RET_QK + p * LANES:RET_QK + (p + 1) * LANES].astype(F32), cos_t, sin_t)
            k = k * (RET_DK ** -0.5)
            qb[c, p] = q.astype(BF16)
            kb[c, p] = k.astype(BF16)
            qx[c, p] = (q * xi_ref[p]).astype(BF16)
            kz[c, p] = (k * zeta_ref[p]).astype(BF16)

    items = [(c, h) for c in range(nchunk) for h in range(RET_HEADS)]

    def head_mask(h):
        return head_lo if h % 2 == 0 else jnp.logical_not(head_lo)

    def v_of(c, h):
        return v_ref[c * CHUNK:(c + 1) * CHUNK, h * RET_DV:(h + 1) * RET_DV]

    scores16 = {}
    for c, h in items:
        qm = jnp.where(head_mask(h), qb[c, h // 2], jnp.zeros((CHUNK, LANES), BF16))
        scores16[c, h] = (_dot_nt(qm, kb[c, h // 2]) * dmat_ref[h]).astype(BF16)
    kv = {(c, h): _dot_tn(kz[c, h // 2], v_of(c, h)) for c, h in items}

    for c in range(nchunk):
        rows = slice(c * CHUNK, (c + 1) * CHUNK)
        s = [s_ref[h] for h in range(RET_HEADS)]
        ro = []
        for h in range(RET_HEADS):
            qxm = jnp.where(head_mask(h), qx[c, h // 2], jnp.zeros((CHUNK, LANES), BF16))
            ro.append(_dot(jnp.concatenate([scores16[c, h], qxm], axis=1),
                           jnp.concatenate([v_of(c, h), s[h].astype(BF16)], axis=0)))
        for h in range(RET_HEADS):
            s_ref[h] = s[h] * decays[h] + kv[c, h]
        for h in range(RET_HEADS):
            cols = slice(h * RET_DV, (h + 1) * RET_DV)
            mu = jnp.mean(ro[h], axis=-1, keepdims=True)
            d = ro[h] - mu
            var = jnp.mean(d * d, axis=-1, keepdims=True)
            y = (d * lax.rsqrt(var + GN_EPS)) * gn_ref[:, cols] * gate_ref[rows, cols].astype(F32)
            stage_ref[rows, cols] = y.astype(BF16)

    return gr_ref[...].astype(F32) * _dot(stage_ref[...], wbr_ref[...])


def _ret_tables():
    h = np.arange(RET_HEADS, dtype=np.float64)
    gamma = 1.0 - 2.0 ** (-5.0 - h)
    log_g = np.log(gamma)
    idx = np.arange(CHUNK, dtype=np.float64)
    diff = idx[:, None] - idx[None, :]
    dmat = np.where(diff >= 0, np.exp(log_g[:, None, None] * np.maximum(diff, 0.0)[None]), 0.0)
    zeta = np.exp(log_g[:, None] * (CHUNK - 1.0 - idx)[None, :])
    xi = np.exp(log_g[:, None] * (idx + 1.0)[None, :])
    decays = tuple(float(v) for v in np.exp(log_g * CHUNK))

    def pair_table(t):
        t = t.reshape(RET_HEADS // 2, 2, CHUNK)
        return np.repeat(np.transpose(t, (0, 2, 1)), RET_DK, axis=2)

    return (jnp.asarray(dmat, F32), jnp.asarray(pair_table(xi), F32), jnp.asarray(pair_table(zeta), F32), decays)


def _rope_tables(t_len):
    half = RET_DK // 2
    inv = ROPE_BASE ** (-np.arange(0, RET_DK, 2, dtype=np.float64) / RET_DK)
    ang = np.arange(t_len, dtype=np.float64)[:, None] * inv[None, :]
    cos, sin = np.cos(ang), np.sin(ang)
    cos_t = np.concatenate([cos, cos], axis=1)
    sin_t = np.concatenate([-sin, sin], axis=1)
    assert cos_t.shape[1] == 2 * half
    reps = LANES // RET_DK
    return jnp.asarray(np.tile(cos_t, (1, reps)), F32), jnp.asarray(np.tile(sin_t, (1, reps)), F32)


INV_FULL_LEVELS = 3


def _odd_block_rows(x, b):
    return jnp.concatenate([x[(2 * i + 1) * b:(2 * i + 2) * b] for i in range(CHUNK // (2 * b))], axis=0)


def _with_odd_block_rows(x, odd, b):
    parts = []
    for i in range(CHUNK // (2 * b)):
        parts += [x[2 * i * b:(2 * i + 1) * b], odd[i * b:(i + 1) * b]]
    return jnp.concatenate(parts, axis=0)


def _inv_unit_lower_many(a_list, qmask_ref, qmask_odd_ref):
    row = lax.broadcasted_iota(jnp.int32, (CHUNK, CHUNK), 0)
    col = lax.broadcasted_iota(jnp.int32, (CHUNK, CHUNK), 1)
    eye = jnp.where(row == col, 1.0, 0.0).astype(F32)
    a16 = [a.astype(BF16) for a in a_list]
    d = [eye - a * qmask_ref[0] for a in a_list]
    level = 1
    b = 2
    while b < CHUNK:
        d16 = [x.astype(BF16) for x in d]
        if level < INV_FULL_LEVELS:
            mask = qmask_ref[level]
            t16 = [_dot(x, a).astype(BF16) for x, a in zip(d16, a16)]
            d = [x - _dot(t, y) * mask for x, t, y in zip(d, t16, d16)]
        else:
            mask = qmask_odd_ref[level - INV_FULL_LEVELS]
            t16 = [_dot(_odd_block_rows(x, b), a).astype(BF16) for x, a in zip(d16, a16)]
            d = [_with_odd_block_rows(x, _odd_block_rows(x, b) - _dot(t, y) * mask, b)
                 for x, t, y in zip(d, t16, d16)]
        level += 1
        b *= 2
    return d


def _dn_body(nchunk, mr, q_ref, k_ref, v_ref, z_ref, gd_ref, x_ref, ps_ref, alog_ref, dtb_ref,
             tri_ref, qmask_ref, qmask_odd_ref, ng_ref, wbr_ref, wo_ref, h_ref, s_ref, stage_ref):
    ps = ps_ref[...]
    beta_all = _sigmoid(ps)
    g_all = -jnp.exp(alog_ref[...]) * _softplus(ps + dtb_ref[...])

    row = lax.broadcasted_iota(jnp.int32, (CHUNK, CHUNK), 0)
    col = lax.broadcasted_iota(jnp.int32, (CHUNK, CHUNK), 1)
    causal = row >= col
    strict = row > col
    tri = tri_ref[...]

    def lanes(x, j):
        return jnp.broadcast_to(x[:, j:j + 1], (CHUNK, LANES))

    g_cum, g_cum_t, exp_g_all, exp_rest_all = [], [], [], []
    for c in range(nchunk):
        g = _dot_split_rhs(tri, g_all[c * CHUNK:(c + 1) * CHUNK])
        g_cum.append(g)
        g_cum_t.append(g.T)
        exp_g_all.append(jnp.exp(g))
        exp_rest_all.append(jnp.exp(g[CHUNK - 1:CHUNK, :] - g))

    def decay_of(c, h, mask):
        gc = lanes(g_cum[c], SMALL_DECAY0 + h)
        gr = jnp.broadcast_to(g_cum_t[c][SMALL_DECAY0 + h:SMALL_DECAY0 + h + 1, :], (CHUNK, CHUNK))
        return jnp.where(mask, jnp.exp(gc - gr), 0.0)

    def block(ref, c, h):
        return ref[c * CHUNK:(c + 1) * CHUNK, h * DN_DV:(h + 1) * DN_DV]

    items = [(c, h) for c in range(nchunk) for h in range(DN_HEADS)]

    a_list, rhs16 = [], []
    for c, h in items:
        k16 = block(k_ref, c, h)
        beta = lanes(beta_all[c * CHUNK:(c + 1) * CHUNK], SMALL_BETA0 + h)
        k_beta = k16.astype(F32) * beta
        a_list.append(_dot_nt(k_beta.astype(BF16), k16) * decay_of(c, h, strict))
        exp_g = lanes(exp_g_all[c], SMALL_DECAY0 + h)
        rhs16.append(jnp.concatenate([block(v_ref, c, h).astype(F32) * beta, k_beta * exp_g], axis=1).astype(BF16))

    minv = _inv_unit_lower_many(a_list, qmask_ref, qmask_odd_ref)
    sol = [_dot(m.astype(BF16), r) for m, r in zip(minv, rhs16)]

    for c in range(nchunk):
        qa16, kdec16, sdec = [], [], []
        for h in range(DN_HEADS):
            q16 = block(q_ref, c, h)
            k16 = block(k_ref, c, h)
            exp_g = lanes(exp_g_all[c], SMALL_DECAY0 + h)
            attn = _dot_nt(q16, k16) * decay_of(c, h, causal)
            qa16.append(jnp.concatenate([(q16.astype(F32) * exp_g).astype(BF16), attn.astype(BF16)], axis=1))
            kdec16.append((k16.astype(F32) * lanes(exp_rest_all[c], SMALL_DECAY0 + h)).astype(BF16))
            sdec.append(exp_g[CHUNK - 1:CHUNK, :])
        idx = [c * DN_HEADS + h for h in range(DN_HEADS)]
        s = [s_ref[h] for h in range(DN_HEADS)]
        s16 = [x.astype(BF16) for x in s]
        ws = [_dot(sol[i][:, DN_DV:].astype(BF16), s16[h]) for h, i in enumerate(idx)]
        vn16 = [(sol[i][:, :DN_DV] - ws[h]).astype(BF16) for h, i in enumerate(idx)]
        o = [_dot(qa16[h], jnp.concatenate([s16[h], vn16[h]], axis=0)) for h in range(DN_HEADS)]
        for h in range(DN_HEADS):
            s_ref[h] = s[h] * sdec[h] + _dot_tn(kdec16[h], vn16[h])
        for h in range(DN_HEADS):
            on = o[h] * lax.rsqrt(jnp.mean(o[h] * o[h], axis=-1, keepdims=True) + EPS) * ng_ref[...]
            stage_ref[c * CHUNK:(c + 1) * CHUNK, h * DN_DV:(h + 1) * DN_DV] = (
                on * block(z_ref, c, h).astype(F32)).astype(BF16)

    y_dn = _dot(stage_ref[...], wbr_ref[...])
    merged = mr + gd_ref[...].astype(F32) * y_dn
    h_ref[...] = x_ref[...] + _dot(merged.astype(BF16), wo_ref[...])


def _dn_tables():
    idx = np.arange(CHUNK)
    r, c = idx[:, None], idx[None, :]
    qmasks, qmasks_odd = [], []
    b = 1
    while b < CHUNK:
        m = ((r // (2 * b)) == (c // (2 * b))) & ((r // b) % 2 == 1) & ((c // b) % 2 == 0)
        if len(qmasks) < INV_FULL_LEVELS:
            qmasks.append(m)
        else:
            qmasks_odd.append(m[(idx // b) % 2 == 1])
        b *= 2
    tri = (r >= c).astype(np.float32)
    return (jnp.asarray(tri, BF16), jnp.asarray(np.stack(qmasks), F32), jnp.asarray(np.stack(qmasks_odd), F32))


def _mixer_kernel(decays, nchunk, n_ret_in, n_dn_in, *refs):
    ret_in = refs[:n_ret_in]
    dn_in = refs[n_ret_in:n_ret_in + n_dn_in]
    h_ref, s_ret_ref, stage_ret_ref, s_dn_ref, stage_dn_ref = refs[n_ret_in + n_dn_in:]

    @pl.when(pl.program_id(1) == 0)
    def _():
        s_ret_ref[...] = jnp.zeros_like(s_ret_ref)
        s_dn_ref[...] = jnp.zeros_like(s_dn_ref)

    mr = _ret_body(decays, nchunk, *ret_in, s_ret_ref, stage_ret_ref)
    _dn_body(nchunk, mr, *dn_in, h_ref, s_dn_ref, stage_dn_ref)


def _mixer_call(p_plain, p_silu, p_sigm, p_conv, p_small, x2, cos_t, sin_t, ret_norm_g, w_ret_br, alog_row, dtb_row,
                dn_norm_g, w_dn_br, w_o, batch, t_len, tt):
    m = batch * t_len
    nt = t_len // tt
    dmat, xi, zeta, decays = _ret_tables()
    tri, qmask, qmask_odd = _dn_tables()
    row = lambda cb: pl.BlockSpec((tt, COLBLK), lambda b, t: (b * nt + t, cb))
    tok = lambda n: pl.BlockSpec((tt, n), lambda b, t: (b * nt + t, 0))
    pos = pl.BlockSpec((tt, LANES), lambda b, t: (t, 0))
    const = lambda shape: pl.BlockSpec(shape, lambda b, t: (0,) * len(shape))
    ret_specs = [
        row(0), row(1), row(0), row(0), pos, pos,
        const((RET_HEADS, CHUNK, CHUNK)), const((RET_HEADS // 2, CHUNK, LANES)), const((RET_HEADS // 2, CHUNK, LANES)),
        const((1, RET_V)), const((RET_V, D_MODEL)),
    ]
    ret_args = [p_plain, p_plain, p_silu, p_sigm, cos_t, sin_t, dmat, xi, zeta, ret_norm_g, w_ret_br]
    dn_specs = [
        row(0), row(1), row(2), row(1), row(1), tok(D_MODEL), tok(LANES),
        const((1, LANES)), const((1, LANES)), const((CHUNK, CHUNK)), const(qmask.shape), const(qmask_odd.shape),
        const((1, DN_DV)), const((DN_V, D_MODEL)), const((D_MODEL, D_MODEL)),
    ]
    dn_args = [p_conv, p_conv, p_conv, p_silu, p_sigm, x2, p_small, alog_row, dtb_row, tri, qmask, qmask_odd,
               dn_norm_g, w_dn_br, w_o]
    return pl.pallas_call(
        functools.partial(_mixer_kernel, decays, tt // CHUNK, len(ret_specs), len(dn_specs)),
        grid=(batch, nt),
        in_specs=ret_specs + dn_specs,
        out_specs=tok(D_MODEL),
        out_shape=jax.ShapeDtypeStruct((m, D_MODEL), F32),
        scratch_shapes=[
            pltpu.VMEM((RET_HEADS, LANES, RET_DV), F32),
            pltpu.VMEM((tt, RET_V), BF16),
            pltpu.VMEM((DN_HEADS, DN_DK, DN_DV), F32),
            pltpu.VMEM((tt, DN_V), BF16),
        ],
        compiler_params=pltpu.CompilerParams(
            dimension_semantics=("arbitrary", "arbitrary"), vmem_limit_bytes=VMEM_LIMIT),
        name="mixer",
    )(*ret_args, *dn_args)


FFN_COLS = 256
FFN_STEPS = D_FF // FFN_COLS


def _ffn_kernel(h_ref, g_ref, wup_ref, cw_ref, cb_ref, wdn_ref, gf_ref, o_ref, carry_ref, up_ref, act_ref):
    @pl.when(pl.program_id(1) == 0)
    def _():
        carry_ref[...] = jnp.zeros_like(carry_ref)

    tm = h_ref.shape[0]
    h = h_ref[...]
    u = ((h * lax.rsqrt(jnp.mean(h * h, axis=-1, keepdims=True) + EPS)) * g_ref[...]).astype(BF16)

    def conv_branch(c0, buf):
        cols = slice(c0, c0 + FFN_COLS)
        up = _dot(u, wup_ref[:, cols])
        buf[0:SUBLANES, :] = carry_ref[:, cols]
        buf[SUBLANES:SUBLANES + tm, :] = up
        carry_ref[:, cols] = up[tm - SUBLANES:]
        w = cw_ref[:, cols]
        y = up * w[FFN_CONV - 1:FFN_CONV] + cb_ref[:, cols]
        for s in range(1, FFN_CONV):
            y = y + buf[SUBLANES - s:SUBLANES - s + tm, :] * w[FFN_CONV - 1 - s:FFN_CONV - s]
        return y

    for j in range(FFN_STEPS):
        a = conv_branch(j * FFN_COLS, up_ref.at[j % 2, 0])
        b = conv_branch(D_FF + j * FFN_COLS, up_ref.at[j % 2, 1])
        act_ref[:, j * FFN_COLS:(j + 1) * FFN_COLS] = (_silu(a) * b).astype(BF16)

    acc = h + _dot(act_ref[...], wdn_ref[...])
    o_ref[...] = (acc * lax.rsqrt(jnp.mean(acc * acc, axis=-1, keepdims=True) + EPS)) * gf_ref[...]


def _ffn_call(h2, g_ffn, w_up, conv_w, conv_b, w_down, g_final, batch, t_len, tm):
    m = batch * t_len
    nt = t_len // tm
    tok = pl.BlockSpec((tm, D_MODEL), lambda b, t: (b * nt + t, 0))
    const2 = lambda shape: pl.BlockSpec(shape, lambda b, t: (0, 0))
    resident = lambda shape: pl.BlockSpec(shape, lambda b, t: (0, 0), pipeline_mode=pl.Buffered(1))
    return pl.pallas_call(
        _ffn_kernel,
        grid=(batch, nt),
        in_specs=[
            tok, const2((1, D_MODEL)),
            resident((D_MODEL, 2 * D_FF)),
            const2((FFN_CONV, 2 * D_FF)), const2((1, 2 * D_FF)),
            resident((D_FF, D_MODEL)),
            const2((1, D_MODEL)),
        ],
        out_specs=tok,
        out_shape=jax.ShapeDtypeStruct((m, D_MODEL), F32),
        scratch_shapes=[
            pltpu.VMEM((SUBLANES, 2 * D_FF), F32),
            pltpu.VMEM((2, 2, SUBLANES + tm, FFN_COLS), F32),
            pltpu.VMEM((tm, D_FF), BF16),
        ],
        compiler_params=pltpu.CompilerParams(
            dimension_semantics=("arbitrary", "arbitrary"), vmem_limit_bytes=VMEM_LIMIT),
        name="convffn",
    )(h2, g_ffn, w_up, conv_w, conv_b, w_down, g_final)


def _pick_tile(t_len, want):
    tile = min(want, t_len)
    assert t_len % tile == 0 and tile % CHUNK == 0
    return tile


MIXER_TILE = 512
FFN_TILE = 512


def kernel(x, g_mix, w_in, ret_norm_g, dn_conv_w, dn_a_log, dn_dt_bias, dn_norm_g, w_ret_br, w_dn_br, w_o, g_ffn,
           w_up, ffn_conv_w, ffn_conv_b, w_down, g_final):
    batch, t_len, d_model = x.shape
    assert d_model == D_MODEL and g_mix.shape[0] == 1 and t_len % CHUNK == 0
    m = batch * t_len
    x2 = x.astype(F32).reshape(m, D_MODEL)

    w_t = jnp.swapaxes(w_in[0], 0, 1)
    w_plain, w_silu, w_sigm, w_conv, w_small = _cast_call(w_t)

    conv_w = jnp.transpose(dn_conv_w[0].reshape(SHORT_CONV, PROJ_STEPS, CONV_W), (1, 0, 2))
    p_plain, p_silu, p_sigm, p_conv, p_small = _proj_call(x2, g_mix, w_plain, w_silu, w_sigm, w_conv, w_small, conv_w,
                                                          t_len, PROJ_ROWS)

    cos_t, sin_t = _rope_tables(t_len)
    alog_row = jnp.zeros((1, LANES), F32).at[0, SMALL_DECAY0:SMALL_DECAY0 + DN_HEADS].set(dn_a_log[0])
    dtb_row = jnp.zeros((1, LANES), F32).at[0, SMALL_DECAY0:SMALL_DECAY0 + DN_HEADS].set(dn_dt_bias[0])
    h = _mixer_call(p_plain, p_silu, p_sigm, p_conv, p_small, x2, cos_t, sin_t, ret_norm_g, w_ret_br[0].astype(BF16),
                    alog_row, dtb_row, dn_norm_g, w_dn_br[0].astype(BF16), w_o[0].astype(BF16), batch, t_len,
                    _pick_tile(t_len, MIXER_TILE))

    out = _ffn_call(h, g_ffn, w_up[0].astype(BF16), ffn_conv_w[0], ffn_conv_b, w_down[0].astype(BF16),
                    g_final.reshape(1, D_MODEL), batch, t_len, _pick_tile(t_len, FFN_TILE))
    return out.reshape(batch, t_len, D_MODEL).astype(x.dtype)
```

```python
import functools

import numpy as np
import jax
import jax.numpy as jnp
from jax import lax
from jax.experimental import pallas as pl
from jax.experimental.pallas import tpu as pltpu

D_MODEL = 1024
RET_HEADS = 8
RET_DK = 64
RET_DV = 128
DN_HEADS = 8
DN_DK = 128
DN_DV = 128
CHUNK = 128
SHORT_CONV = 4
FFN_CONV = 3
D_FF = 2816
ROPE_BASE = 10000.0
EPS = 1e-6
GN_EPS = 1e-5

RET_QK = RET_HEADS * RET_DK
RET_V = RET_HEADS * RET_DV
DN_QK = DN_HEADS * DN_DK
DN_V = DN_HEADS * DN_DV

LANES = 128
SUBLANES = 8
COLBLK = 1024
SMALL_BETA0 = 0
SMALL_DECAY0 = 8

VMEM_LIMIT = 56 * 1024 * 1024

F32 = jnp.float32
BF16 = jnp.bfloat16


def _dot(a, b):
    return jnp.dot(a, b, preferred_element_type=F32)


def _dot_nt(a, b):
    return lax.dot_general(a, b, (((1,), (1,)), ((), ())), preferred_element_type=F32)


def _dot_tn(a, b):
    return lax.dot_general(a, b, (((0,), (0,)), ((), ())), preferred_element_type=F32)


def _split3(x):
    hi = x.astype(BF16)
    r1 = x - hi.astype(F32)
    mid = r1.astype(BF16)
    lo = (r1 - mid.astype(F32)).astype(BF16)
    return hi, mid, lo


def _dot_split_rhs(a, x):
    hi, mid, lo = _split3(x)
    return _dot(a, hi) + _dot(a, mid) + _dot(a, lo)


def _sigmoid(x):
    return 0.5 + 0.5 * jnp.tanh(0.5 * x)


def _sigmoid_of_half(h):
    return 0.5 + 0.5 * jnp.tanh(h)


def _silu_of_half(h):
    return h + h * jnp.tanh(h)


def _softplus(x):
    return jnp.maximum(x, 0.0) + jnp.log(1.0 + jnp.exp(-jnp.abs(x)))


PROJ_STEPS = 4
PLAIN_W = (2 * RET_QK + RET_V) // PROJ_STEPS
SILU_W = (RET_V + DN_V) // PROJ_STEPS
SIGM_W = 2 * D_MODEL // PROJ_STEPS
CONV_W = (2 * DN_QK + DN_V) // PROJ_STEPS
COL_RGATE = 2 * RET_QK + RET_V
COL_CONV = COL_RGATE + RET_V
COL_DZ = COL_CONV + 2 * DN_QK + DN_V
COL_SMALL = COL_DZ + DN_V
COL_SIGM = COL_SMALL + 2 * DN_HEADS
CONV_GROUPS = CONV_W // DN_DK
PROJ_ROWS = 256


def _proj_kernel(blocks_per_seq, x_ref, g_ref, w1_ref, w2a_ref, w2b_ref, w3_ref, ws_ref, convw_ref,
                 o1_ref, o2a_ref, o2b_ref, o3_ref, ps_ref, carry_ref, buf_ref, carry2_ref, buf2_ref):
    i = pl.program_id(0)
    rows = x_ref.shape[0]

    @pl.when(i == 0)
    def _():
        carry_ref[...] = jnp.zeros_like(carry_ref)
        carry2_ref[...] = jnp.zeros_like(carry2_ref)

    x = x_ref[...]
    u = ((x * lax.rsqrt(jnp.mean(x * x, axis=-1, keepdims=True) + EPS)) * g_ref[...]).astype(BF16)
    ps_ref[...] = _dot(u, ws_ref[...])
    seq_start = i % blocks_per_seq == 0

    def products(cs):
        acc = _dot(u, w3_ref[cs])
        prev = carry_ref[cs]
        buf_ref[cs, 0:SUBLANES, :] = jnp.where(seq_start, jnp.zeros_like(prev), prev)
        buf_ref[cs, SUBLANES:SUBLANES + rows, :] = acc
        carry_ref[cs] = acc[rows - SUBLANES:]
        return acc, _dot(u, w2a_ref[cs]), _dot(u, w2b_ref[cs]), _dot(u, w1_ref[cs])

    def elementwise(cs, acc, ga, gb, p):
        w = convw_ref[cs]
        assert SHORT_CONV == 4
        sx = buf_ref[cs, SUBLANES - 1:SUBLANES - 1 + rows, :]
        far = acc * w[1:2] + sx * w[0:1]
        prev2 = carry2_ref[cs]
        buf2_ref[cs, 0:SUBLANES, :] = jnp.where(seq_start, jnp.zeros_like(prev2), prev2)
        buf2_ref[cs, SUBLANES:SUBLANES + rows, :] = far
        carry2_ref[cs] = far[rows - SUBLANES:]
        y = (acc * w[3:4] + sx * w[2:3]) + buf2_ref[cs, SUBLANES - 2:SUBLANES - 2 + rows, :]
        y = _silu_of_half(y)
        for l in range(CONV_GROUPS):
            head_group = cs * CONV_GROUPS + l
            yl = y[:, l * DN_DK:(l + 1) * DN_DK]
            if head_group < 2 * DN_HEADS:
                scale = DN_DK ** -0.5 if head_group < DN_HEADS else 1.0
                yl = yl * (lax.rsqrt(jnp.sum(yl * yl, axis=-1, keepdims=True) + EPS) * scale)
            o3_ref[:, cs * CONV_W + l * DN_DK:cs * CONV_W + (l + 1) * DN_DK] = yl.astype(BF16)
        o2a_ref[:, cs * SILU_W:(cs + 1) * SILU_W] = _silu_of_half(ga).astype(BF16)
        o2b_ref[:, cs * SIGM_W:(cs + 1) * SIGM_W] = _sigmoid_of_half(gb).astype(BF16)
        o1_ref[:, cs * PLAIN_W:(cs + 1) * PLAIN_W] = p.astype(BF16)

    pending = products(0)
    for cs in range(1, PROJ_STEPS):
        nxt = products(cs)
        elementwise(cs - 1, *pending)
        pending = nxt
    elementwise(PROJ_STEPS - 1, *pending)


def _cast_kernel(wp_ref, wa_ref, wg_ref, wc_ref, ws_ref, op_ref, oa_ref, og_ref, oc_ref, os_ref):
    op_ref[...] = wp_ref[...].T.astype(BF16)
    oa_ref[...] = (0.5 * wa_ref[...].T).astype(BF16)
    og_ref[...] = (0.5 * wg_ref[...].T).astype(BF16)
    oc_ref[...] = wc_ref[...].T.astype(BF16)
    rows = lax.broadcasted_iota(jnp.int32, ws_ref.shape, 0)
    small = jnp.where(rows < COL_SIGM - COL_SMALL, ws_ref[...], 0.0)
    os_ref[...] = small.T.astype(BF16)


def _cast_call(w_t):
    half = PROJ_STEPS // 2
    elem = lambda n: (pl.Element(n), pl.Element(D_MODEL))
    out = lambda n: pl.BlockSpec((None, D_MODEL, n), lambda s: (s, 0, 0))
    shape = lambda n: jax.ShapeDtypeStruct((PROJ_STEPS, D_MODEL, n), BF16)
    return pl.pallas_call(
        _cast_kernel,
        grid=(PROJ_STEPS,),
        in_specs=[
            pl.BlockSpec((PLAIN_W, D_MODEL), lambda s: (s, 0)),
            pl.BlockSpec((SILU_W, D_MODEL),
                         lambda s: (jnp.where(s < half, COL_RGATE // SILU_W + s, COL_DZ // SILU_W + s - half), 0)),
            pl.BlockSpec(elem(SIGM_W), lambda s: (pl.multiple_of(COL_SIGM + s * SIGM_W, SUBLANES), 0)),
            pl.BlockSpec((CONV_W, D_MODEL), lambda s: (COL_CONV // CONV_W + s, 0)),
            pl.BlockSpec(elem(LANES), lambda s: (COL_SMALL, 0)),
        ],
        out_specs=[out(PLAIN_W), out(SILU_W), out(SIGM_W), out(CONV_W),
                   pl.BlockSpec((D_MODEL, LANES), lambda s: (0, 0))],
        out_shape=[shape(PLAIN_W), shape(SILU_W), shape(SIGM_W), shape(CONV_W),
                   jax.ShapeDtypeStruct((D_MODEL, LANES), BF16)],
        compiler_params=pltpu.CompilerParams(dimension_semantics=("arbitrary",), vmem_limit_bytes=VMEM_LIMIT),
        name="cast_w_in",
    )(w_t, w_t, w_t, w_t, w_t)


def _proj_call(x2, g_mix, w_plain, w_silu, w_sigm, w_conv, w_small, conv_w, t_len, rows):
    m = x2.shape[0]
    assert t_len % rows == 0
    resident = lambda a: pl.BlockSpec(a.shape, lambda i: (0,) * a.ndim, pipeline_mode=pl.Buffered(1))
    tok = lambda n: pl.BlockSpec((rows, n), lambda i: (i, 0))
    widths = (PROJ_STEPS * PLAIN_W, PROJ_STEPS * SILU_W, PROJ_STEPS * SIGM_W, PROJ_STEPS * CONV_W)
    return pl.pallas_call(
        functools.partial(_proj_kernel, t_len // rows),
        grid=(m // rows,),
        in_specs=[tok(D_MODEL), resident(g_mix), resident(w_plain), resident(w_silu), resident(w_sigm),
                  resident(w_conv), resident(w_small), resident(conv_w)],
        out_specs=[tok(n) for n in widths] + [tok(LANES)],
        out_shape=[jax.ShapeDtypeStruct((m, n), BF16) for n in widths] + [jax.ShapeDtypeStruct((m, LANES), F32)],
        scratch_shapes=[
            pltpu.VMEM((PROJ_STEPS, SUBLANES, CONV_W), F32),
            pltpu.VMEM((PROJ_STEPS, SUBLANES + rows, CONV_W), F32),
            pltpu.VMEM((PROJ_STEPS, SUBLANES, CONV_W), F32),
            pltpu.VMEM((PROJ_STEPS, SUBLANES + rows, CONV_W), F32),
        ],
        compiler_params=pltpu.CompilerParams(dimension_semantics=("arbitrary",), vmem_limit_bytes=VMEM_LIMIT),
        name="proj",
    )(x2, g_mix, w_plain, w_silu, w_sigm, w_conv, w_small, conv_w)


def _ret_body(decays, nchunk, qk_ref, v_ref, gate_ref, gr_ref, cos_ref, sin_ref, dmat_ref, xi_ref,
              zeta_ref, gn_ref, wbr_ref, s_ref, stage_ref):
    lane = lax.broadcasted_iota(jnp.int32, (CHUNK, LANES), 1)
    first_half = (lane % RET_DK) < (RET_DK // 2)
    head_lo = lane < RET_DK

    def rot(x, cos_t, sin_t):
        partner = jnp.where(first_half, pltpu.roll(x, LANES - RET_DK // 2, 1), pltpu.roll(x, RET_DK // 2, 1))
        return x * cos_t + partner * sin_t

    qb, kb, qx, kz = {}, {}, {}, {}
    for c in range(nchunk):
        rows = slice(c * CHUNK, (c + 1) * CHUNK)
        cos_t = cos_ref[rows, :]
        sin_t = sin_ref[rows, :]
        for p in range(RET_HEADS // 2):
            q = rot(qk_ref[rows, p * LANES:(p + 1) * LANES].astype(F32), cos_t, sin_t)
            k = rot(qk_ref[rows, RET_QK + p * LANES:RET_QK + (p + 1) * LANES].astype(F32), cos_t, sin_t)
            k = k * (RET_DK ** -0.5)
            qb[c, p] = q.astype(BF16)
            kb[c, p] = k.astype(BF16)
            qx[c, p] = (q * xi_ref[p]).astype(BF16)
            kz[c, p] = (k * zeta_ref[p]).astype(BF16)

    items = [(c, h) for c in range(nchunk) for h in range(RET_HEADS)]

    def head_mask(h):
        return head_lo if h % 2 == 0 else jnp.logical_not(head_lo)

    def v_of(c, h):
        return v_ref[c * CHUNK:(c + 1) * CHUNK, h * RET_DV:(h + 1) * RET_DV]

    scores16 = {}
    for c, h in items:
        qm = jnp.where(head_mask(h), qb[c, h // 2], jnp.zeros((CHUNK, LANES), BF16))
        scores16[c, h] = (_dot_nt(qm, kb[c, h // 2]) * dmat_ref[h]).astype(BF16)
    kv = {(c, h): _dot_tn(kz[c, h // 2], v_of(c, h)) for c, h in items}

    for c in range(nchunk):
        rows = slice(c * CHUNK, (c + 1) * CHUNK)
        s = [s_ref[h] for h in range(RET_HEADS)]
        ro = []
        for h in range(RET_HEADS):
            qxm = jnp.where(head_mask(h), qx[c, h // 2], jnp.zeros((CHUNK, LANES), BF16))
            ro.append(_dot(jnp.concatenate([scores16[c, h], qxm], axis=1),
                           jnp.concatenate([v_of(c, h), s[h].astype(BF16)], axis=0)))
        for h in range(RET_HEADS):
            s_ref[h] = s[h] * decays[h] + kv[c, h]
        for h in range(RET_HEADS):
            cols = slice(h * RET_DV, (h + 1) * RET_DV)
            mu = jnp.mean(ro[h], axis=-1, keepdims=True)
            d = ro[h] - mu
            var = jnp.mean(d * d, axis=-1, keepdims=True)
            y = (d * lax.rsqrt(var + GN_EPS)) * gn_ref[:, cols] * gate_ref[rows, cols].astype(F32)
            stage_ref[rows, cols] = y.astype(BF16)

    return gr_ref[...].astype(F32) * _dot(stage_ref[...], wbr_ref[...])


def _ret_tables():
    h = np.arange(RET_HEADS, dtype=np.float64)
    gamma = 1.0 - 2.0 ** (-5.0 - h)
    log_g = np.log(gamma)
    idx = np.arange(CHUNK, dtype=np.float64)
    diff = idx[:, None] - idx[None, :]
    dmat = np.where(diff >= 0, np.exp(log_g[:, None, None] * np.maximum(diff, 0.0)[None]), 0.0)
    zeta = np.exp(log_g[:, None] * (CHUNK - 1.0 - idx)[None, :])
    xi = np.exp(log_g[:, None] * (idx + 1.0)[None, :])
    decays = tuple(float(v) for v in np.exp(log_g * CHUNK))

    def pair_table(t):
        t = t.reshape(RET_HEADS // 2, 2, CHUNK)
        return np.repeat(np.transpose(t, (0, 2, 1)), RET_DK, axis=2)

    return (jnp.asarray(dmat, F32), jnp.asarray(pair_table(xi), F32), jnp.asarray(pair_table(zeta), F32), decays)


def _rope_tables(t_len):
    half = RET_DK // 2
    inv = ROPE_BASE ** (-np.arange(0, RET_DK, 2, dtype=np.float64) / RET_DK)
    ang = np.arange(t_len, dtype=np.float64)[:, None] * inv[None, :]
    cos, sin = np.cos(ang), np.sin(ang)
    cos_t = np.concatenate([cos, cos], axis=1)
    sin_t = np.concatenate([-sin, sin], axis=1)
    assert cos_t.shape[1] == 2 * half
    reps = LANES // RET_DK
    return jnp.asarray(np.tile(cos_t, (1, reps)), F32), jnp.asarray(np.tile(sin_t, (1, reps)), F32)


INV_FULL_LEVELS = 4


def _odd_block_rows(x, b):
    return jnp.concatenate([x[(2 * i + 1) * b:(2 * i + 2) * b] for i in range(CHUNK // (2 * b))], axis=0)


def _with_odd_block_rows(x, odd, b):
    parts = []
    for i in range(CHUNK // (2 * b)):
        parts += [x[2 * i * b:(2 * i + 1) * b], odd[i * b:(i + 1) * b]]
    return jnp.concatenate(parts, axis=0)


def _inv_unit_lower_many(a_list, qmask_ref, qmask_odd_ref):
    row = lax.broadcasted_iota(jnp.int32, (CHUNK, CHUNK), 0)
    col = lax.broadcasted_iota(jnp.int32, (CHUNK, CHUNK), 1)
    eye = jnp.where(row == col, 1.0, 0.0).astype(F32)
    a16 = [a.astype(BF16) for a in a_list]
    d16 = [(eye - a * qmask_ref[0]).astype(BF16) for a in a_list]
    level = 1
    b = 2
    while b < CHUNK:
        if level < INV_FULL_LEVELS:
            mask = qmask_ref[level]
            t16 = [_dot(x, a).astype(BF16) for x, a in zip(d16, a16)]
            d16 = [x - (_dot(t, x) * mask).astype(BF16) for x, t in zip(d16, t16)]
        else:
            mask = qmask_odd_ref[level - INV_FULL_LEVELS]
            odd = [_odd_block_rows(x, b) for x in d16]
            t16 = [_dot(o, a).astype(BF16) for o, a in zip(odd, a16)]
            d16 = [_with_odd_block_rows(x, o - (_dot(t, x) * mask).astype(BF16), b)
                   for x, o, t in zip(d16, odd, t16)]
        level += 1
        b *= 2
    return d16


def _dn_body(nchunk, mr, q_ref, k_ref, v_ref, z_ref, gd_ref, x_ref, ps_ref, alog_ref, dtb_ref,
             tri_ref, qmask_ref, qmask_odd_ref, ng_ref, wbr_ref, wo_ref, h_ref, s_ref, stage_ref):
    ps = ps_ref[...]
    beta_all = _sigmoid(ps)
    g_all = -jnp.exp(alog_ref[...]) * _softplus(ps + dtb_ref[...])

    row = lax.broadcasted_iota(jnp.int32, (CHUNK, CHUNK), 0)
    col = lax.broadcasted_iota(jnp.int32, (CHUNK, CHUNK), 1)
    causal = row >= col
    strict = row > col
    tri = tri_ref[...]

    def lanes(x, j):
        return jnp.broadcast_to(x[:, j:j + 1], (CHUNK, LANES))

    g_cum, g_cum_t, exp_g_all, exp_rest_all = [], [], [], []
    for c in range(nchunk):
        g = _dot_split_rhs(tri, g_all[c * CHUNK:(c + 1) * CHUNK])
        g_cum.append(g)
        g_cum_t.append(g.T)
        exp_g_all.append(jnp.exp(g))
        exp_rest_all.append(jnp.exp(g[CHUNK - 1:CHUNK, :] - g))

    def decay_of(c, h, mask):
        gc = lanes(g_cum[c], SMALL_DECAY0 + h)
        gr = jnp.broadcast_to(g_cum_t[c][SMALL_DECAY0 + h:SMALL_DECAY0 + h + 1, :], (CHUNK, CHUNK))
        return jnp.where(mask, jnp.exp(gc - gr), 0.0)

    def block(ref, c, h):
        return ref[c * CHUNK:(c + 1) * CHUNK, h * DN_DV:(h + 1) * DN_DV]

    items = [(c, h) for c in range(nchunk) for h in range(DN_HEADS)]

    a_list, rhs16 = [], []
    for c, h in items:
        k16 = block(k_ref, c, h)
        beta = lanes(beta_all[c * CHUNK:(c + 1) * CHUNK], SMALL_BETA0 + h)
        k_beta = k16.astype(F32) * beta
        a_list.append(_dot_nt(k_beta.astype(BF16), k16) * decay_of(c, h, strict))
        exp_g = lanes(exp_g_all[c], SMALL_DECAY0 + h)
        rhs16.append(jnp.concatenate([block(v_ref, c, h).astype(F32) * beta, k_beta * exp_g], axis=1).astype(BF16))

    minv = _inv_unit_lower_many(a_list, qmask_ref, qmask_odd_ref)
    sol = [_dot(m, r) for m, r in zip(minv, rhs16)]

    for c in range(nchunk):
        qa16, kdec16, sdec = [], [], []
        for h in range(DN_HEADS):
            q16 = block(q_ref, c, h)
            k16 = block(k_ref, c, h)
            exp_g = lanes(exp_g_all[c], SMALL_DECAY0 + h)
            attn = _dot_nt(q16, k16) * decay_of(c, h, causal)
            qa16.append(jnp.concatenate([(q16.astype(F32) * exp_g).astype(BF16), attn.astype(BF16)], axis=1))
            kdec16.append((k16.astype(F32) * lanes(exp_rest_all[c], SMALL_DECAY0 + h)).astype(BF16))
            sdec.append(exp_g[CHUNK - 1:CHUNK, :])
        idx = [c * DN_HEADS + h for h in range(DN_HEADS)]
        s = [s_ref[h] for h in range(DN_HEADS)]
        s16 = [x.astype(BF16) for x in s]
        ws = [_dot(sol[i][:, DN_DV:].astype(BF16), s16[h]) for h, i in enumerate(idx)]
        vn16 = [(sol[i][:, :DN_DV] - ws[h]).astype(BF16) for h, i in enumerate(idx)]
        o = [_dot(qa16[h], jnp.concatenate([s16[h], vn16[h]], axis=0)) for h in range(DN_HEADS)]
        for h in range(DN_HEADS):
            s_ref[h] = s[h] * sdec[h] + _dot_tn(kdec16[h], vn16[h])
        for h in range(DN_HEADS):
            on = o[h] * lax.rsqrt(jnp.mean(o[h] * o[h], axis=-1, keepdims=True) + EPS) * ng_ref[...]
            stage_ref[c * CHUNK:(c + 1) * CHUNK, h * DN_DV:(h + 1) * DN_DV] = (
                on * block(z_ref, c, h).astype(F32)).astype(BF16)

    y_dn = _dot(stage_ref[...], wbr_ref[...])
    merged = mr + gd_ref[...].astype(F32) * y_dn
    h_ref[...] = x_ref[...] + _dot(merged.astype(BF16), wo_ref[...])


def _dn_tables():
    idx = np.arange(CHUNK)
    r, c = idx[:, None], idx[None, :]
    qmasks, qmasks_odd = [], []
    b = 1
    while b < CHUNK:
        m = ((r // (2 * b)) == (c // (2 * b))) & ((r // b) % 2 == 1) & ((c // b) % 2 == 0)
        if len(qmasks) < INV_FULL_LEVELS:
            qmasks.append(m)
        else:
            qmasks_odd.append(m[(idx // b) % 2 == 1])
        b *= 2
    tri = (r >= c).astype(np.float32)
    return (jnp.asarray(tri, BF16), jnp.asarray(np.stack(qmasks), F32), jnp.asarray(np.stack(qmasks_odd), F32))


def _mixer_kernel(decays, nchunk, n_ret_in, n_dn_in, *refs):
    ret_in = refs[:n_ret_in]
    dn_in = refs[n_ret_in:n_ret_in + n_dn_in]
    h_ref, s_ret_ref, stage_ret_ref, s_dn_ref, stage_dn_ref = refs[n_ret_in + n_dn_in:]

    @pl.when(pl.program_id(1) == 0)
    def _():
        s_ret_ref[...] = jnp.zeros_like(s_ret_ref)
        s_dn_ref[...] = jnp.zeros_like(s_dn_ref)

    mr = _ret_body(decays, nchunk, *ret_in, s_ret_ref, stage_ret_ref)
    _dn_body(nchunk, mr, *dn_in, h_ref, s_dn_ref, stage_dn_ref)


def _mixer_call(p_plain, p_silu, p_sigm, p_conv, p_small, x2, cos_t, sin_t, ret_norm_g, w_ret_br, alog_row, dtb_row,
                dn_norm_g, w_dn_br, w_o, batch, t_len, tt):
    m = batch * t_len
    nt = t_len // tt
    dmat, xi, zeta, decays = _ret_tables()
    tri, qmask, qmask_odd = _dn_tables()
    row = lambda cb: pl.BlockSpec((tt, COLBLK), lambda b, t: (b * nt + t, cb))
    tok = lambda n: pl.BlockSpec((tt, n), lambda b, t: (b * nt + t, 0))
    pos = pl.BlockSpec((tt, LANES), lambda b, t: (t, 0))
    const = lambda shape: pl.BlockSpec(shape, lambda b, t: (0,) * len(shape))
    ret_specs = [
        row(0), row(1), row(0), row(0), pos, pos,
        const((RET_HEADS, CHUNK, CHUNK)), const((RET_HEADS // 2, CHUNK, LANES)), const((RET_HEADS // 2, CHUNK, LANES)),
        const((1, RET_V)), const((RET_V, D_MODEL)),
    ]
    ret_args = [p_plain, p_plain, p_silu, p_sigm, cos_t, sin_t, dmat, xi, zeta, ret_norm_g, w_ret_br]
    dn_specs = [
        row(0), row(1), row(2), row(1), row(1), tok(D_MODEL), tok(LANES),
        const((1, LANES)), const((1, LANES)), const((CHUNK, CHUNK)), const(qmask.shape), const(qmask_odd.shape),
        const((1, DN_DV)), const((DN_V, D_MODEL)), const((D_MODEL, D_MODEL)),
    ]
    dn_args = [p_conv, p_conv, p_conv, p_silu, p_sigm, x2, p_small, alog_row, dtb_row, tri, qmask, qmask_odd,
               dn_norm_g, w_dn_br, w_o]
    return pl.pallas_call(
        functools.partial(_mixer_kernel, decays, tt // CHUNK, len(ret_specs), len(dn_specs)),
        grid=(batch, nt),
        in_specs=ret_specs + dn_specs,
        out_specs=tok(D_MODEL),
        out_shape=jax.ShapeDtypeStruct((m, D_MODEL), F32),
        scratch_shapes=[
            pltpu.VMEM((RET_HEADS, LANES, RET_DV), F32),
            pltpu.VMEM((tt, RET_V), BF16),
            pltpu.VMEM((DN_HEADS, DN_DK, DN_DV), F32),
            pltpu.VMEM((tt, DN_V), BF16),
        ],
        compiler_params=pltpu.CompilerParams(
            dimension_semantics=("arbitrary", "arbitrary"), vmem_limit_bytes=VMEM_LIMIT),
        name="mixer",
    )(*ret_args, *dn_args)


FFN_COLS = 256
FFN_STEPS = D_FF // FFN_COLS


def _ffn_kernel(h_ref, g_ref, wup_ref, cw_ref, cb_ref, wdn_ref, gf_ref, o_ref, carry_ref, up_ref, act_ref):
    @pl.when(pl.program_id(1) == 0)
    def _():
        carry_ref[...] = jnp.zeros_like(carry_ref)

    tm = h_ref.shape[0]
    h = h_ref[...]
    u = ((h * lax.rsqrt(jnp.mean(h * h, axis=-1, keepdims=True) + EPS)) * g_ref[...]).astype(BF16)

    def conv_branch(c0, buf):
        cols = slice(c0, c0 + FFN_COLS)
        up = _dot(u, wup_ref[:, cols])
        buf[0:SUBLANES, :] = carry_ref[:, cols]
        buf[SUBLANES:SUBLANES + tm, :] = up
        carry_ref[:, cols] = up[tm - SUBLANES:]
        w = cw_ref[:, cols]
        y = up * w[FFN_CONV - 1:FFN_CONV] + cb_ref[:, cols]
        for s in range(1, FFN_CONV):
            y = y + buf[SUBLANES - s:SUBLANES - s + tm, :] * w[FFN_CONV - 1 - s:FFN_CONV - s]
        return y

    for j in range(FFN_STEPS):
        a = conv_branch(j * FFN_COLS, up_ref.at[j % 2, 0])
        b = conv_branch(D_FF + j * FFN_COLS, up_ref.at[j % 2, 1])
        act_ref[:, j * FFN_COLS:(j + 1) * FFN_COLS] = (_silu_of_half(a) * b).astype(BF16)

    acc = h + _dot(act_ref[...], wdn_ref[...])
    o_ref[...] = (acc * lax.rsqrt(jnp.mean(acc * acc, axis=-1, keepdims=True) + EPS)) * gf_ref[...]


def _ffn_call(h2, g_ffn, w_up, conv_w, conv_b, w_down, g_final, batch, t_len, tm):
    m = batch * t_len
    nt = t_len // tm
    tok = pl.BlockSpec((tm, D_MODEL), lambda b, t: (b * nt + t, 0))
    const2 = lambda shape: pl.BlockSpec(shape, lambda b, t: (0, 0))
    resident = lambda shape: pl.BlockSpec(shape, lambda b, t: (0, 0), pipeline_mode=pl.Buffered(1))
    return pl.pallas_call(
        _ffn_kernel,
        grid=(batch, nt),
        in_specs=[
            tok, const2((1, D_MODEL)),
            resident((D_MODEL, 2 * D_FF)),
            const2((FFN_CONV, 2 * D_FF)), const2((1, 2 * D_FF)),
            resident((D_FF, D_MODEL)),
            const2((1, D_MODEL)),
        ],
        out_specs=tok,
        out_shape=jax.ShapeDtypeStruct((m, D_MODEL), F32),
        scratch_shapes=[
            pltpu.VMEM((SUBLANES, 2 * D_FF), F32),
            pltpu.VMEM((2, 2, SUBLANES + tm, FFN_COLS), F32),
            pltpu.VMEM((tm, D_FF), BF16),
        ],
        compiler_params=pltpu.CompilerParams(
            dimension_semantics=("arbitrary", "arbitrary"), vmem_limit_bytes=VMEM_LIMIT),
        name="convffn",
    )(h2, g_ffn, w_up, conv_w, conv_b, w_down, g_final)


def _pick_tile(t_len, want):
    tile = min(want, t_len)
    assert t_len % tile == 0 and tile % CHUNK == 0
    return tile


MIXER_TILE = 512
FFN_TILE = 1024


def kernel(x, g_mix, w_in, ret_norm_g, dn_conv_w, dn_a_log, dn_dt_bias, dn_norm_g, w_ret_br, w_dn_br, w_o, g_ffn,
           w_up, ffn_conv_w, ffn_conv_b, w_down, g_final):
    batch, t_len, d_model = x.shape
    assert d_model == D_MODEL and g_mix.shape[0] == 1 and t_len % CHUNK == 0
    m = batch * t_len
    x2 = x.astype(F32).reshape(m, D_MODEL)

    w_t = jnp.swapaxes(w_in[0], 0, 1)
    w_plain, w_silu, w_sigm, w_conv, w_small = _cast_call(w_t)

    conv_w = 0.5 * jnp.transpose(dn_conv_w[0].reshape(SHORT_CONV, PROJ_STEPS, CONV_W), (1, 0, 2))
    p_plain, p_silu, p_sigm, p_conv, p_small = _proj_call(x2, g_mix, w_plain, w_silu, w_sigm, w_conv, w_small, conv_w,
                                                          t_len, PROJ_ROWS)

    cos_t, sin_t = _rope_tables(t_len)
    alog_row = jnp.zeros((1, LANES), F32).at[0, SMALL_DECAY0:SMALL_DECAY0 + DN_HEADS].set(dn_a_log[0])
    dtb_row = jnp.zeros((1, LANES), F32).at[0, SMALL_DECAY0:SMALL_DECAY0 + DN_HEADS].set(dn_dt_bias[0])
    h = _mixer_call(p_plain, p_silu, p_sigm, p_conv, p_small, x2, cos_t, sin_t, ret_norm_g, w_ret_br[0].astype(BF16),
                    alog_row, dtb_row, dn_norm_g, w_dn_br[0].astype(BF16), w_o[0].astype(BF16), batch, t_len,
                    _pick_tile(t_len, MIXER_TILE))

    ffn_scale = jnp.concatenate([jnp.full((1, D_FF), 0.5, F32), jnp.ones((1, D_FF), F32)], axis=1)
    out = _ffn_call(h, g_ffn, w_up[0].astype(BF16), ffn_conv_w[0] * ffn_scale, ffn_conv_b * ffn_scale,
                    w_down[0].astype(BF16),
                    g_final.reshape(1, D_MODEL), batch, t_len, _pick_tile(t_len, FFN_TILE))
    return out.reshape(batch, t_len, D_MODEL).astype(x.dtype)
```

```python
import functools

import numpy as np
import jax
import jax.numpy as jnp
from jax import lax
from jax.experimental import pallas as pl
from jax.experimental.pallas import tpu as pltpu

D_MODEL = 1024
RET_HEADS = 8
RET_DK = 64
RET_DV = 128
DN_HEADS = 8
DN_DK = 128
DN_DV = 128
CHUNK = 128
SHORT_CONV = 4
FFN_CONV = 3
D_FF = 2816
ROPE_BASE = 10000.0
EPS = 1e-6
GN_EPS = 1e-5

RET_QK = RET_HEADS * RET_DK
RET_V = RET_HEADS * RET_DV
DN_QK = DN_HEADS * DN_DK
DN_V = DN_HEADS * DN_DV

LANES = 128
SUBLANES = 8
COLBLK = 1024
SMALL_BETA0 = 0
SMALL_DECAY0 = 8

VMEM_LIMIT = 56 * 1024 * 1024

F32 = jnp.float32
BF16 = jnp.bfloat16


def _dot(a, b):
    return jnp.dot(a, b, preferred_element_type=F32)


def _dot_nt(a, b):
    return lax.dot_general(a, b, (((1,), (1,)), ((), ())), preferred_element_type=F32)


def _dot_tn(a, b):
    return lax.dot_general(a, b, (((0,), (0,)), ((), ())), preferred_element_type=F32)


def _split3(x):
    hi = x.astype(BF16)
    r1 = x - hi.astype(F32)
    mid = r1.astype(BF16)
    lo = (r1 - mid.astype(F32)).astype(BF16)
    return hi, mid, lo


def _dot_split_rhs(a, x):
    hi, mid, lo = _split3(x)
    return _dot(a, hi) + _dot(a, mid) + _dot(a, lo)


def _sigmoid(x):
    return 0.5 + 0.5 * jnp.tanh(0.5 * x)


def _sigmoid_of_half(h):
    return 0.5 + 0.5 * jnp.tanh(h)


def _silu_of_half(h):
    return h + h * jnp.tanh(h)


def _softplus(x):
    return jnp.maximum(x, 0.0) + jnp.log(1.0 + jnp.exp(-jnp.abs(x)))


PROJ_STEPS = 4
PLAIN_W = (2 * RET_QK + RET_V) // PROJ_STEPS
SILU_W = (RET_V + DN_V) // PROJ_STEPS
SIGM_W = 2 * D_MODEL // PROJ_STEPS
CONV_W = (2 * DN_QK + DN_V) // PROJ_STEPS
COL_RGATE = 2 * RET_QK + RET_V
COL_CONV = COL_RGATE + RET_V
COL_DZ = COL_CONV + 2 * DN_QK + DN_V
COL_SMALL = COL_DZ + DN_V
COL_SIGM = COL_SMALL + 2 * DN_HEADS
CONV_GROUPS = CONV_W // DN_DK
PROJ_ROWS = 256


def _proj_kernel(blocks_per_seq, x_ref, g_ref, w1_ref, w2a_ref, w2b_ref, w3_ref, ws_ref, convw_ref,
                 o1_ref, o2a_ref, o2b_ref, o3_ref, ps_ref, carry_ref, buf_ref, carry2_ref, buf2_ref):
    i = pl.program_id(0)
    rows = x_ref.shape[0]

    @pl.when(i == 0)
    def _():
        carry_ref[...] = jnp.zeros_like(carry_ref)
        carry2_ref[...] = jnp.zeros_like(carry2_ref)

    x = x_ref[...]
    u = ((x * lax.rsqrt(jnp.mean(x * x, axis=-1, keepdims=True) + EPS)) * g_ref[...]).astype(BF16)
    ps_ref[...] = _dot(u, ws_ref[...])
    seq_start = i % blocks_per_seq == 0

    def products(cs):
        acc = _dot(u, w3_ref[cs])
        prev = carry_ref[cs]
        buf_ref[cs, 0:SUBLANES, :] = jnp.where(seq_start, jnp.zeros_like(prev), prev)
        buf_ref[cs, SUBLANES:SUBLANES + rows, :] = acc
        carry_ref[cs] = acc[rows - SUBLANES:]
        return acc, _dot(u, w2a_ref[cs]), _dot(u, w2b_ref[cs]), _dot(u, w1_ref[cs])

    def elementwise(cs, acc, ga, gb, p):
        w = convw_ref[cs]
        assert SHORT_CONV == 4
        sx = buf_ref[cs, SUBLANES - 1:SUBLANES - 1 + rows, :]
        far = acc * w[1:2] + sx * w[0:1]
        prev2 = carry2_ref[cs]
        buf2_ref[cs, 0:SUBLANES, :] = jnp.where(seq_start, jnp.zeros_like(prev2), prev2)
        buf2_ref[cs, SUBLANES:SUBLANES + rows, :] = far
        carry2_ref[cs] = far[rows - SUBLANES:]
        y = (acc * w[3:4] + sx * w[2:3]) + buf2_ref[cs, SUBLANES - 2:SUBLANES - 2 + rows, :]
        y = _silu_of_half(y)
        for l in range(CONV_GROUPS):
            head_group = cs * CONV_GROUPS + l
            yl = y[:, l * DN_DK:(l + 1) * DN_DK]
            if head_group < 2 * DN_HEADS:
                scale = DN_DK ** -0.5 if head_group < DN_HEADS else 1.0
                yl = yl * (lax.rsqrt(jnp.sum(yl * yl, axis=-1, keepdims=True) + EPS) * scale)
            o3_ref[:, cs * CONV_W + l * DN_DK:cs * CONV_W + (l + 1) * DN_DK] = yl.astype(BF16)
        o2a_ref[:, cs * SILU_W:(cs + 1) * SILU_W] = _silu_of_half(ga).astype(BF16)
        o2b_ref[:, cs * SIGM_W:(cs + 1) * SIGM_W] = _sigmoid_of_half(gb).astype(BF16)
        o1_ref[:, cs * PLAIN_W:(cs + 1) * PLAIN_W] = p.astype(BF16)

    pending = products(0)
    for cs in range(1, PROJ_STEPS):
        nxt = products(cs)
        elementwise(cs - 1, *pending)
        pending = nxt
    elementwise(PROJ_STEPS - 1, *pending)


def _cast_kernel(wp_ref, wa_ref, wg_ref, wc_ref, ws_ref, op_ref, oa_ref, og_ref, oc_ref, os_ref):
    op_ref[...] = wp_ref[...].T.astype(BF16)
    oa_ref[...] = (0.5 * wa_ref[...].T).astype(BF16)
    og_ref[...] = (0.5 * wg_ref[...].T).astype(BF16)
    oc_ref[...] = wc_ref[...].T.astype(BF16)
    rows = lax.broadcasted_iota(jnp.int32, ws_ref.shape, 0)
    small = jnp.where(rows < COL_SIGM - COL_SMALL, ws_ref[...], 0.0)
    os_ref[...] = small.T.astype(BF16)


def _cast_call(w_t):
    half = PROJ_STEPS // 2
    elem = lambda n: (pl.Element(n), pl.Element(D_MODEL))
    out = lambda n: pl.BlockSpec((None, D_MODEL, n), lambda s: (s, 0, 0))
    shape = lambda n: jax.ShapeDtypeStruct((PROJ_STEPS, D_MODEL, n), BF16)
    return pl.pallas_call(
        _cast_kernel,
        grid=(PROJ_STEPS,),
        in_specs=[
            pl.BlockSpec((PLAIN_W, D_MODEL), lambda s: (s, 0)),
            pl.BlockSpec((SILU_W, D_MODEL),
                         lambda s: (jnp.where(s < half, COL_RGATE // SILU_W + s, COL_DZ // SILU_W + s - half), 0)),
            pl.BlockSpec(elem(SIGM_W), lambda s: (pl.multiple_of(COL_SIGM + s * SIGM_W, SUBLANES), 0)),
            pl.BlockSpec((CONV_W, D_MODEL), lambda s: (COL_CONV // CONV_W + s, 0)),
            pl.BlockSpec(elem(LANES), lambda s: (COL_SMALL, 0)),
        ],
        out_specs=[out(PLAIN_W), out(SILU_W), out(SIGM_W), out(CONV_W),
                   pl.BlockSpec((D_MODEL, LANES), lambda s: (0, 0))],
        out_shape=[shape(PLAIN_W), shape(SILU_W), shape(SIGM_W), shape(CONV_W),
                   jax.ShapeDtypeStruct((D_MODEL, LANES), BF16)],
        compiler_params=pltpu.CompilerParams(dimension_semantics=("arbitrary",), vmem_limit_bytes=VMEM_LIMIT),
        name="cast_w_in",
    )(w_t, w_t, w_t, w_t, w_t)


def _proj_call(x2, g_mix, w_plain, w_silu, w_sigm, w_conv, w_small, conv_w, t_len, rows):
    m = x2.shape[0]
    assert t_len % rows == 0
    resident = lambda a: pl.BlockSpec(a.shape, lambda i: (0,) * a.ndim, pipeline_mode=pl.Buffered(1))
    tok = lambda n: pl.BlockSpec((rows, n), lambda i: (i, 0))
    widths = (PROJ_STEPS * PLAIN_W, PROJ_STEPS * SILU_W, PROJ_STEPS * SIGM_W, PROJ_STEPS * CONV_W)
    return pl.pallas_call(
        functools.partial(_proj_kernel, t_len // rows),
        grid=(m // rows,),
        in_specs=[tok(D_MODEL), resident(g_mix), resident(w_plain), resident(w_silu), resident(w_sigm),
                  resident(w_conv), resident(w_small), resident(conv_w)],
        out_specs=[tok(n) for n in widths] + [tok(LANES)],
        out_shape=[jax.ShapeDtypeStruct((m, n), BF16) for n in widths] + [jax.ShapeDtypeStruct((m, LANES), F32)],
        scratch_shapes=[
            pltpu.VMEM((PROJ_STEPS, SUBLANES, CONV_W), F32),
            pltpu.VMEM((PROJ_STEPS, SUBLANES + rows, CONV_W), F32),
            pltpu.VMEM((PROJ_STEPS, SUBLANES, CONV_W), F32),
            pltpu.VMEM((PROJ_STEPS, SUBLANES + rows, CONV_W), F32),
        ],
        compiler_params=pltpu.CompilerParams(dimension_semantics=("arbitrary",), vmem_limit_bytes=VMEM_LIMIT),
        name="proj",
    )(x2, g_mix, w_plain, w_silu, w_sigm, w_conv, w_small, conv_w)


def _ret_body(decays, nchunk, qk_ref, v_ref, gate_ref, gr_ref, cos_ref, sin_ref, dmat_ref, xi_ref,
              zeta_ref, gn_ref, wbr_ref, s_ref, stage_ref):
    lane = lax.broadcasted_iota(jnp.int32, (CHUNK, LANES), 1)
    first_half = (lane % RET_DK) < (RET_DK // 2)
    head_lo = lane < RET_DK

    def rot(x, cos_t, sin_t):
        partner = jnp.where(first_half, pltpu.roll(x, LANES - RET_DK // 2, 1), pltpu.roll(x, RET_DK // 2, 1))
        return x * cos_t + partner * sin_t

    qb, kb, qx, kz = {}, {}, {}, {}
    for c in range(nchunk):
        rows = slice(c * CHUNK, (c + 1) * CHUNK)
        cos_t = cos_ref[rows, :]
        sin_t = sin_ref[rows, :]
        for p in range(RET_HEADS // 2):
            q = rot(qk_ref[rows, p * LANES:(p + 1) * LANES].astype(F32), cos_t, sin_t)
            k = rot(qk_ref[rows, RET_QK + p * LANES:RET_QK + (p + 1) * LANES].astype(F32), cos_t, sin_t)
            k = k * (RET_DK ** -0.5)
            qb[c, p] = q.astype(BF16)
            kb[c, p] = k.astype(BF16)
            qx[c, p] = (q * xi_ref[p]).astype(BF16)
            kz[c, p] = (k * zeta_ref[p]).astype(BF16)

    items = [(c, h) for c in range(nchunk) for h in range(RET_HEADS)]

    def head_mask(h):
        return head_lo if h % 2 == 0 else jnp.logical_not(head_lo)

    def v_of(c, h):
        return v_ref[c * CHUNK:(c + 1) * CHUNK, h * RET_DV:(h + 1) * RET_DV]

    scores16 = {}
    for c, h in items:
        qm = jnp.where(head_mask(h), qb[c, h // 2], jnp.zeros((CHUNK, LANES), BF16))
        scores16[c, h] = (_dot_nt(qm, kb[c, h // 2]) * dmat_ref[h]).astype(BF16)
    kv = {(c, h): _dot_tn(kz[c, h // 2], v_of(c, h)) for c, h in items}

    for c in range(nchunk):
        rows = slice(c * CHUNK, (c + 1) * CHUNK)
        s = [s_ref[h] for h in range(RET_HEADS)]
        ro = []
        for h in range(RET_HEADS):
            qxm = jnp.where(head_mask(h), qx[c, h // 2], jnp.zeros((CHUNK, LANES), BF16))
            ro.append(_dot(jnp.concatenate([scores16[c, h], qxm], axis=1),
                           jnp.concatenate([v_of(c, h), s[h].astype(BF16)], axis=0)))
        for h in range(RET_HEADS):
            s_ref[h] = s[h] * decays[h] + kv[c, h]
        for h in range(RET_HEADS):
            cols = slice(h * RET_DV, (h + 1) * RET_DV)
            mu = jnp.mean(ro[h], axis=-1, keepdims=True)
            d = ro[h] - mu
            var = jnp.mean(d * d, axis=-1, keepdims=True)
            y = (d * lax.rsqrt(var + GN_EPS)) * gn_ref[:, cols] * gate_ref[rows, cols].astype(F32)
            stage_ref[rows, cols] = y.astype(BF16)

    return gr_ref[...].astype(F32) * _dot(stage_ref[...], wbr_ref[...])


def _ret_tables():
    h = np.arange(RET_HEADS, dtype=np.float64)
    gamma = 1.0 - 2.0 ** (-5.0 - h)
    log_g = np.log(gamma)
    idx = np.arange(CHUNK, dtype=np.float64)
    diff = idx[:, None] - idx[None, :]
    dmat = np.where(diff >= 0, np.exp(log_g[:, None, None] * np.maximum(diff, 0.0)[None]), 0.0)
    zeta = np.exp(log_g[:, None] * (CHUNK - 1.0 - idx)[None, :])
    xi = np.exp(log_g[:, None] * (idx + 1.0)[None, :])
    decays = tuple(float(v) for v in np.exp(log_g * CHUNK))

    def pair_table(t):
        t = t.reshape(RET_HEADS // 2, 2, CHUNK)
        return np.repeat(np.transpose(t, (0, 2, 1)), RET_DK, axis=2)

    return (jnp.asarray(dmat, F32), jnp.asarray(pair_table(xi), F32), jnp.asarray(pair_table(zeta), F32), decays)


def _rope_tables(t_len):
    half = RET_DK // 2
    inv = ROPE_BASE ** (-np.arange(0, RET_DK, 2, dtype=np.float64) / RET_DK)
    ang = np.arange(t_len, dtype=np.float64)[:, None] * inv[None, :]
    cos, sin = np.cos(ang), np.sin(ang)
    cos_t = np.concatenate([cos, cos], axis=1)
    sin_t = np.concatenate([-sin, sin], axis=1)
    assert cos_t.shape[1] == 2 * half
    reps = LANES // RET_DK
    return jnp.asarray(np.tile(cos_t, (1, reps)), F32), jnp.asarray(np.tile(sin_t, (1, reps)), F32)


INV_FULL_LEVELS = 4


def _odd_block_rows(x, b):
    return jnp.concatenate([x[(2 * i + 1) * b:(2 * i + 2) * b] for i in range(CHUNK // (2 * b))], axis=0)


def _with_odd_block_rows(x, odd, b):
    parts = []
    for i in range(CHUNK // (2 * b)):
        parts += [x[2 * i * b:(2 * i + 1) * b], odd[i * b:(i + 1) * b]]
    return jnp.concatenate(parts, axis=0)


def _inv_unit_lower_many(a_list, qmask_ref, qmask_odd_ref):
    row = lax.broadcasted_iota(jnp.int32, (CHUNK, CHUNK), 0)
    col = lax.broadcasted_iota(jnp.int32, (CHUNK, CHUNK), 1)
    eye = jnp.where(row == col, 1.0, 0.0).astype(F32)
    a16 = [a.astype(BF16) for a in a_list]
    d16 = [(eye - a * qmask_ref[0]).astype(BF16) for a in a_list]
    level = 1
    b = 2
    while b < CHUNK:
        if level < INV_FULL_LEVELS:
            mask = qmask_ref[level]
            t16 = [_dot(x, a).astype(BF16) for x, a in zip(d16, a16)]
            d16 = [x - (_dot(t, x) * mask).astype(BF16) for x, t in zip(d16, t16)]
        else:
            mask = qmask_odd_ref[level - INV_FULL_LEVELS]
            odd = [_odd_block_rows(x, b) for x in d16]
            t16 = [_dot(o, a).astype(BF16) for o, a in zip(odd, a16)]
            d16 = [_with_odd_block_rows(x, o - (_dot(t, x) * mask).astype(BF16), b)
                   for x, o, t in zip(d16, odd, t16)]
        level += 1
        b *= 2
    return d16


def _dn_body(nchunk, mr, q_ref, k_ref, v_ref, z_ref, gd_ref, x_ref, ps_ref, alog_ref, dtb_ref,
             tri_ref, qmask_ref, qmask_odd_ref, ng_ref, wbr_ref, wo_ref, h_ref, s_ref, stage_ref):
    ps = ps_ref[...]
    beta_all = _sigmoid(ps)
    g_all = -jnp.exp(alog_ref[...]) * _softplus(ps + dtb_ref[...])

    row = lax.broadcasted_iota(jnp.int32, (CHUNK, CHUNK), 0)
    col = lax.broadcasted_iota(jnp.int32, (CHUNK, CHUNK), 1)
    causal = row >= col
    strict = row > col
    tri = tri_ref[...]

    def lanes(x, j):
        return jnp.broadcast_to(x[:, j:j + 1], (CHUNK, LANES))

    g_cum, g_cum_t, exp_g_all, exp_rest_all = [], [], [], []
    for c in range(nchunk):
        g = _dot_split_rhs(tri, g_all[c * CHUNK:(c + 1) * CHUNK])
        g_cum.append(g)
        g_cum_t.append(g.T)
        exp_g_all.append(jnp.exp(g))
        exp_rest_all.append(jnp.exp(g[CHUNK - 1:CHUNK, :] - g))

    def decay_of(c, h, mask):
        gc = lanes(g_cum[c], SMALL_DECAY0 + h)
        gr = jnp.broadcast_to(g_cum_t[c][SMALL_DECAY0 + h:SMALL_DECAY0 + h + 1, :], (CHUNK, CHUNK))
        return jnp.where(mask, jnp.exp(gc - gr), 0.0)

    def block(ref, c, h):
        return ref[c * CHUNK:(c + 1) * CHUNK, h * DN_DV:(h + 1) * DN_DV]

    items = [(c, h) for c in range(nchunk) for h in range(DN_HEADS)]

    a_list, vb16, kbg16, decay = [], [], [], {}
    for c, h in items:
        k16 = block(k_ref, c, h)
        beta = lanes(beta_all[c * CHUNK:(c + 1) * CHUNK], SMALL_BETA0 + h)
        k_beta = k16.astype(F32) * beta
        decay[c, h] = decay_of(c, h, causal)
        a_list.append(_dot_nt(k_beta.astype(BF16), k16) * jnp.where(strict, decay[c, h], 0.0))
        exp_g = lanes(exp_g_all[c], SMALL_DECAY0 + h)
        vb16.append((block(v_ref, c, h).astype(F32) * beta).astype(BF16))
        kbg16.append((k_beta * exp_g).astype(BF16))

    minv = _inv_unit_lower_many(a_list, qmask_ref, qmask_odd_ref)
    u_sol = [_dot(m, r) for m, r in zip(minv, vb16)]
    w16 = [_dot(m, r).astype(BF16) for m, r in zip(minv, kbg16)]

    for c in range(nchunk):
        qa16, kdec16, sdec = [], [], []
        for h in range(DN_HEADS):
            q16 = block(q_ref, c, h)
            k16 = block(k_ref, c, h)
            exp_g = lanes(exp_g_all[c], SMALL_DECAY0 + h)
            attn = _dot_nt(q16, k16) * decay[c, h]
            qa16.append(jnp.concatenate([(q16.astype(F32) * exp_g).astype(BF16), attn.astype(BF16)], axis=1))
            kdec16.append((k16.astype(F32) * lanes(exp_rest_all[c], SMALL_DECAY0 + h)).astype(BF16))
            sdec.append(exp_g[CHUNK - 1:CHUNK, :])
        idx = [c * DN_HEADS + h for h in range(DN_HEADS)]
        s = [s_ref[h] for h in range(DN_HEADS)]
        s16 = [x.astype(BF16) for x in s]
        ws = [_dot(w16[i], s16[h]) for h, i in enumerate(idx)]
        vn16 = [(u_sol[i] - ws[h]).astype(BF16) for h, i in enumerate(idx)]
        o = [_dot(qa16[h], jnp.concatenate([s16[h], vn16[h]], axis=0)) for h in range(DN_HEADS)]
        for h in range(DN_HEADS):
            s_ref[h] = s[h] * sdec[h] + _dot_tn(kdec16[h], vn16[h])
        for h in range(DN_HEADS):
            on = o[h] * lax.rsqrt(jnp.mean(o[h] * o[h], axis=-1, keepdims=True) + EPS) * ng_ref[...]
            stage_ref[c * CHUNK:(c + 1) * CHUNK, h * DN_DV:(h + 1) * DN_DV] = (
                on * block(z_ref, c, h).astype(F32)).astype(BF16)

    y_dn = _dot(stage_ref[...], wbr_ref[...])
    merged = mr + gd_ref[...].astype(F32) * y_dn
    h_ref[...] = x_ref[...] + _dot(merged.astype(BF16), wo_ref[...])


def _dn_tables():
    idx = np.arange(CHUNK)
    r, c = idx[:, None], idx[None, :]
    qmasks, qmasks_odd = [], []
    b = 1
    while b < CHUNK:
        m = ((r // (2 * b)) == (c // (2 * b))) & ((r // b) % 2 == 1) & ((c // b) % 2 == 0)
        if len(qmasks) < INV_FULL_LEVELS:
            qmasks.append(m)
        else:
            qmasks_odd.append(m[(idx // b) % 2 == 1])
        b *= 2
    tri = (r >= c).astype(np.float32)
    return (jnp.asarray(tri, BF16), jnp.asarray(np.stack(qmasks), F32), jnp.asarray(np.stack(qmasks_odd), F32))


def _mixer_kernel(decays, nchunk, n_ret_in, n_dn_in, *refs):
    ret_in = refs[:n_ret_in]
    dn_in = refs[n_ret_in:n_ret_in + n_dn_in]
    h_ref, s_ret_ref, stage_ret_ref, s_dn_ref, stage_dn_ref = refs[n_ret_in + n_dn_in:]

    @pl.when(pl.program_id(1) == 0)
    def _():
        s_ret_ref[...] = jnp.zeros_like(s_ret_ref)
        s_dn_ref[...] = jnp.zeros_like(s_dn_ref)

    mr = _ret_body(decays, nchunk, *ret_in, s_ret_ref, stage_ret_ref)
    _dn_body(nchunk, mr, *dn_in, h_ref, s_dn_ref, stage_dn_ref)


def _mixer_call(p_plain, p_silu, p_sigm, p_conv, p_small, x2, cos_t, sin_t, ret_norm_g, w_ret_br, alog_row, dtb_row,
                dn_norm_g, w_dn_br, w_o, batch, t_len, tt):
    m = batch * t_len
    nt = t_len // tt
    dmat, xi, zeta, decays = _ret_tables()
    tri, qmask, qmask_odd = _dn_tables()
    row = lambda cb: pl.BlockSpec((tt, COLBLK), lambda b, t: (b * nt + t, cb))
    tok = lambda n: pl.BlockSpec((tt, n), lambda b, t: (b * nt + t, 0))
    pos = pl.BlockSpec((tt, LANES), lambda b, t: (t, 0))
    const = lambda shape: pl.BlockSpec(shape, lambda b, t: (0,) * len(shape))
    ret_specs = [
        row(0), row(1), row(0), row(0), pos, pos,
        const((RET_HEADS, CHUNK, CHUNK)), const((RET_HEADS // 2, CHUNK, LANES)), const((RET_HEADS // 2, CHUNK, LANES)),
        const((1, RET_V)), const((RET_V, D_MODEL)),
    ]
    ret_args = [p_plain, p_plain, p_silu, p_sigm, cos_t, sin_t, dmat, xi, zeta, ret_norm_g, w_ret_br]
    dn_specs = [
        row(0), row(1), row(2), row(1), row(1), tok(D_MODEL), tok(LANES),
        const((1, LANES)), const((1, LANES)), const((CHUNK, CHUNK)), const(qmask.shape), const(qmask_odd.shape),
        const((1, DN_DV)), const((DN_V, D_MODEL)), const((D_MODEL, D_MODEL)),
    ]
    dn_args = [p_conv, p_conv, p_conv, p_silu, p_sigm, x2, p_small, alog_row, dtb_row, tri, qmask, qmask_odd,
               dn_norm_g, w_dn_br, w_o]
    return pl.pallas_call(
        functools.partial(_mixer_kernel, decays, tt // CHUNK, len(ret_specs), len(dn_specs)),
        grid=(batch, nt),
        in_specs=ret_specs + dn_specs,
        out_specs=tok(D_MODEL),
        out_shape=jax.ShapeDtypeStruct((m, D_MODEL), F32),
        scratch_shapes=[
            pltpu.VMEM((RET_HEADS, LANES, RET_DV), F32),
            pltpu.VMEM((tt, RET_V), BF16),
            pltpu.VMEM((DN_HEADS, DN_DK, DN_DV), F32),
            pltpu.VMEM((tt, DN_V), BF16),
        ],
        compiler_params=pltpu.CompilerParams(
            dimension_semantics=("arbitrary", "arbitrary"), vmem_limit_bytes=VMEM_LIMIT),
        name="mixer",
    )(*ret_args, *dn_args)


FFN_COLS = 256
FFN_STEPS = D_FF // FFN_COLS


def _ffn_kernel(h_ref, g_ref, wup_ref, cw_ref, cb_ref, wdn_ref, gf_ref, o_ref, carry_ref, up_ref, act_ref):
    @pl.when(pl.program_id(1) == 0)
    def _():
        carry_ref[...] = jnp.zeros_like(carry_ref)

    tm = h_ref.shape[0]
    h = h_ref[...]
    u = ((h * lax.rsqrt(jnp.mean(h * h, axis=-1, keepdims=True) + EPS)) * g_ref[...]).astype(BF16)

    def conv_branch(c0, buf):
        cols = slice(c0, c0 + FFN_COLS)
        up = _dot(u, wup_ref[:, cols])
        buf[0:SUBLANES, :] = carry_ref[:, cols]
        buf[SUBLANES:SUBLANES + tm, :] = up
        carry_ref[:, cols] = up[tm - SUBLANES:]
        w = cw_ref[:, cols]
        y = up * w[FFN_CONV - 1:FFN_CONV] + cb_ref[:, cols]
        for s in range(1, FFN_CONV):
            y = y + buf[SUBLANES - s:SUBLANES - s + tm, :] * w[FFN_CONV - 1 - s:FFN_CONV - s]
        return y

    for j in range(FFN_STEPS):
        a = conv_branch(j * FFN_COLS, up_ref.at[j % 2, 0])
        b = conv_branch(D_FF + j * FFN_COLS, up_ref.at[j % 2, 1])
        act_ref[:, j * FFN_COLS:(j + 1) * FFN_COLS] = (_silu_of_half(a) * b).astype(BF16)

    acc = h + _dot(act_ref[...], wdn_ref[...])
    o_ref[...] = (acc * lax.rsqrt(jnp.mean(acc * acc, axis=-1, keepdims=True) + EPS)) * gf_ref[...]


def _ffn_call(h2, g_ffn, w_up, conv_w, conv_b, w_down, g_final, batch, t_len, tm):
    m = batch * t_len
    nt = t_len // tm
    tok = pl.BlockSpec((tm, D_MODEL), lambda b, t: (b * nt + t, 0))
    const2 = lambda shape: pl.BlockSpec(shape, lambda b, t: (0, 0))
    resident = lambda shape: pl.BlockSpec(shape, lambda b, t: (0, 0), pipeline_mode=pl.Buffered(1))
    return pl.pallas_call(
        _ffn_kernel,
        grid=(batch, nt),
        in_specs=[
            tok, const2((1, D_MODEL)),
            resident((D_MODEL, 2 * D_FF)),
            const2((FFN_CONV, 2 * D_FF)), const2((1, 2 * D_FF)),
            resident((D_FF, D_MODEL)),
            const2((1, D_MODEL)),
        ],
        out_specs=tok,
        out_shape=jax.ShapeDtypeStruct((m, D_MODEL), F32),
        scratch_shapes=[
            pltpu.VMEM((SUBLANES, 2 * D_FF), F32),
            pltpu.VMEM((2, 2, SUBLANES + tm, FFN_COLS), F32),
            pltpu.VMEM((tm, D_FF), BF16),
        ],
        compiler_params=pltpu.CompilerParams(
            dimension_semantics=("arbitrary", "arbitrary"), vmem_limit_bytes=VMEM_LIMIT),
        name="convffn",
    )(h2, g_ffn, w_up, conv_w, conv_b, w_down, g_final)


def _pick_tile(t_len, want):
    tile = min(want, t_len)
    assert t_len % tile == 0 and tile % CHUNK == 0
    return tile


MIXER_TILE = 512
FFN_TILE = 1024


def kernel(x, g_mix, w_in, ret_norm_g, dn_conv_w, dn_a_log, dn_dt_bias, dn_norm_g, w_ret_br, w_dn_br, w_o, g_ffn,
           w_up, ffn_conv_w, ffn_conv_b, w_down, g_final):
    batch, t_len, d_model = x.shape
    assert d_model == D_MODEL and g_mix.shape[0] == 1 and t_len % CHUNK == 0
    m = batch * t_len
    x2 = x.astype(F32).reshape(m, D_MODEL)

    w_t = jnp.swapaxes(w_in[0], 0, 1)
    w_plain, w_silu, w_sigm, w_conv, w_small = _cast_call(w_t)

    conv_w = 0.5 * jnp.transpose(dn_conv_w[0].reshape(SHORT_CONV, PROJ_STEPS, CONV_W), (1, 0, 2))
    p_plain, p_silu, p_sigm, p_conv, p_small = _proj_call(x2, g_mix, w_plain, w_silu, w_sigm, w_conv, w_small, conv_w,
                                                          t_len, PROJ_ROWS)

    cos_t, sin_t = _rope_tables(t_len)
    alog_row = jnp.zeros((1, LANES), F32).at[0, SMALL_DECAY0:SMALL_DECAY0 + DN_HEADS].set(dn_a_log[0])
    dtb_row = jnp.zeros((1, LANES), F32).at[0, SMALL_DECAY0:SMALL_DECAY0 + DN_HEADS].set(dn_dt_bias[0])
    h = _mixer_call(p_plain, p_silu, p_sigm, p_conv, p_small, x2, cos_t, sin_t, ret_norm_g, w_ret_br[0].astype(BF16),
                    alog_row, dtb_row, dn_norm_g, w_dn_br[0].astype(BF16), w_o[0].astype(BF16), batch, t_len,
                    _pick_tile(t_len, MIXER_TILE))

    ffn_scale = jnp.concatenate([jnp.full((1, D_FF), 0.5, F32), jnp.ones((1, D_FF), F32)], axis=1)
    out = _ffn_call(h, g_ffn, w_up[0].astype(BF16), ffn_conv_w[0] * ffn_scale, ffn_conv_b * ffn_scale,
                    w_down[0].astype(BF16),
                    g_final.reshape(1, D_MODEL), batch, t_len, _pick_tile(t_len, FFN_TILE))
    return out.reshape(batch, t_len, D_MODEL).astype(x.dtype)
```
